```python
import math
import jax, jax.numpy as jnp
from jax import lax
import numpy as np

D_MODEL = 4096
BATCH = 2
SEQ = 4096
DEPTH = 1
DEC_BATCH = 128
DEC_SEQ = 4
PAST_LEN = 2048
PAGE_SIZE = 128

D_MIX = D_MODEL
D_ATT = D_MIX // 2
HEAD_DIM_A = 128
N_HEADS_A = D_ATT // HEAD_DIM_A
D_RWKV = D_MIX - D_ATT
HEAD_DIM_B = 64
N_HEADS_B = D_RWKV // HEAD_DIM_B
DILATION_PAIRS = ((128, 1), (512, 4), (2048, 16))
MAX_WINDOW = max(w for w, _ in DILATION_PAIRS)
NUM_BUCKETS = 32
REL_MAX_DIST = MAX_WINDOW
DECAY_LORA = 128
AAA_LORA = 128
GATE_LORA = 480
D_FF = 11008
CONV_WIDTH = 3
PLE_DIM = 256
RMS_EPS = 1e-6
GN_EPS = 64e-5
NEG_INF = -1e30
B_COLS = 3 * D_RWKV + DECAY_LORA + AAA_LORA + GATE_LORA
IN_COLS = 3 * D_ATT + B_COLS

kernel_name = 'hybrid_dilated_rwkv7_convffn_step'


def _rms(x, g, eps=RMS_EPS):
    xf = x.astype(jnp.float32)
    y = xf * lax.rsqrt(jnp.mean(xf * xf, axis=-1, keepdims=True) + eps)
    return (y * g.astype(jnp.float32)).astype(x.dtype)


def _rel_bucket(dist):
    max_exact = NUM_BUCKETS // 2
    d_f = jnp.maximum(dist, 1).astype(jnp.float32)
    large = max_exact + (jnp.log(d_f / max_exact) / math.log(REL_MAX_DIST / max_exact)
                         * (NUM_BUCKETS - max_exact)).astype(jnp.int32)
    large = jnp.minimum(large, NUM_BUCKETS - 1)
    return jnp.where(dist < max_exact, dist, large)


def _branch_prompt(q, k, v, rel_bias, dil, sub):
    b, t, h, dh = q.shape
    nb = -(-t // (dil * sub))
    tp = nb * sub * dil
    pad = lambda a: jnp.pad(a, ((0, 0), (0, tp - t), (0, 0), (0, 0))).reshape(b, nb, sub, dil, h, dh)
    qb, kb, vb = pad(q), pad(k), pad(v)
    with_prev = lambda a: jnp.concatenate(
        [jnp.pad(a[:, :-1], ((0, 0), (1, 0), (0, 0), (0, 0), (0, 0), (0, 0))), a], axis=2)
    kk, vv = with_prev(kb), with_prev(vb)
    logits = jnp.einsum('bnqrhd,bnkrhd->bnrhqk', qb, kk, preferred_element_type=jnp.float32) / math.sqrt(dh)
    qi = jnp.arange(sub)[:, None]
    ki = jnp.arange(2 * sub)[None, :]
    dsub = qi + sub - ki
    bias = rel_bias[_rel_bucket(dil * jnp.clip(dsub, 0, sub))].astype(jnp.float32)
    key_idx = jnp.arange(nb)[:, None] * sub + ki - sub
    mask = ((dsub >= 0) & (dsub <= sub))[None] & (key_idx >= 0)[:, None, :]
    logits = jnp.where(mask[None, :, None, None], logits + jnp.transpose(bias, (2, 0, 1)), NEG_INF)
    m = jnp.max(logits, axis=-1, keepdims=True)
    e = jnp.exp(logits - m)
    s = jnp.sum(e, axis=-1)
    o = jnp.einsum('bnrhqk,bnkrhd->bnqrhd', e.astype(v.dtype), vv, preferred_element_type=jnp.float32)
    o = o / jnp.transpose(s, (0, 1, 4, 2, 3))[..., None]
    lse = jnp.transpose(m[..., 0] + jnp.log(s), (0, 1, 4, 2, 3))
    return o.reshape(b, tp, h, dh)[:, :t], lse.reshape(b, tp, h)[:, :t]


def _branch_sample(q, k, v, cache_k, cache_v, rel_bias, dil, sub):
    n, s_len, h, dh = q.shape
    wb = cache_k.shape[1]
    j = jnp.arange(sub + 1)
    idx = wb + jnp.arange(s_len)[:, None] - dil * j[None, :]
    valid = idx >= 0
    from_new = (idx >= wb)[None, :, :, None, None]
    ic = jnp.clip(idx, 0, wb - 1)
    inew = jnp.clip(idx - wb, 0, s_len - 1)
    kg = jnp.where(from_new, k[:, inew], cache_k[:, ic].astype(k.dtype))
    vg = jnp.where(from_new, v[:, inew], cache_v[:, ic].astype(v.dtype))
    logits = jnp.einsum('nshd,nsjhd->nhsj', q, kg, preferred_element_type=jnp.float32) / math.sqrt(dh)
    bias = rel_bias[_rel_bucket(dil * j)].astype(jnp.float32).T
    logits = jnp.where(valid[None, None], logits + bias[None, :, None, :], NEG_INF)
    m = jnp.max(logits, axis=-1, keepdims=True)
    e = jnp.exp(logits - m)
    s = jnp.sum(e, axis=-1)
    o = jnp.einsum('nhsj,nsjhd->nshd', e.astype(v.dtype), vg, preferred_element_type=jnp.float32)
    o = o / jnp.transpose(s, (0, 2, 1))[..., None]
    lse = jnp.transpose(m[..., 0] + jnp.log(s), (0, 2, 1))
    return o, lse


def _dilated_attention(q, k, v, rel_bias, cache_k, cache_v):
    outs, lses = [], []
    for window, dil in DILATION_PAIRS:
        sub = window // dil
        if cache_k is None:
            o, lse = _branch_prompt(q, k, v, rel_bias, dil, sub)
        else:
            o, lse = _branch_sample(q, k, v, cache_k, cache_v, rel_bias, dil, sub)
        outs.append(o)
        lses.append(lse)
    wts = jax.nn.softmax(jnp.stack(lses), axis=0)
    o = jnp.sum(wts[..., None] * jnp.stack(outs), axis=0)
    n, t = q.shape[:2]
    return o.reshape(n, t, D_ATT).astype(q.dtype)


def _rwkv7(pb, shift_prev, wkv_prev, mu, w0, w2, a0, a2, g2, k_k, k_a, r_k, ln_w, ln_b):
    n, t, _ = pb.shape
    f32 = jnp.float32
    prev = jnp.concatenate([shift_prev[:, None].astype(pb.dtype), pb[:, :-1]], axis=1)
    xs = pb + mu * (prev - pb)
    c1, c2, c3 = D_RWKV, 2 * D_RWKV, 3 * D_RWKV
    c4, c5 = c3 + DECAY_LORA, c3 + DECAY_LORA + AAA_LORA
    r, k, v = xs[..., :c1], xs[..., c1:c2], xs[..., c2:c3]
    wd, ad, gd = xs[..., c3:c4], xs[..., c4:c5], xs[..., c5:]
    w_log = -jax.nn.softplus(-(w0 + jnp.tanh(wd) @ w2).astype(f32)) - 0.5
    decay = jnp.exp(-jnp.exp(w_log))
    a = jax.nn.sigmoid((a0 + ad @ a2).astype(f32))
    g = jax.nn.sigmoid(gd) @ g2
    hd = lambda z: z.astype(f32).reshape(n, t, N_HEADS_B, HEAD_DIM_B)
    r_h, k_h, v_h, a_h, w_h = hd(r), hd(k), hd(v), hd(a), hd(decay)
    kk = k_h * k_k.astype(f32).reshape(N_HEADS_B, HEAD_DIM_B)
    kk = kk / jnp.maximum(jnp.linalg.norm(kk, axis=-1, keepdims=True), 1e-12)
    k_h = k_h * (1.0 + (a_h - 1.0) * k_a.astype(f32).reshape(N_HEADS_B, HEAD_DIM_B))

    def step(state, inp):
        r_t, k_t, v_t, kk_t, a_t, w_t = inp
        sa = jnp.einsum('nhvk,nhk->nhv', state, -kk_t)
        state = (state * w_t[:, :, None, :] + sa[..., None] * (kk_t * a_t)[:, :, None, :]
                 + v_t[..., None] * k_t[:, :, None, :])
        return state, jnp.einsum('nhvk,nhk->nhv', state, r_t)

    sf = lambda z: jnp.swapaxes(z, 0, 1)
    wkv_new, ys = lax.scan(step, wkv_prev.astype(f32), (sf(r_h), sf(k_h), sf(v_h), sf(kk), sf(a_h), sf(w_h)))
    y = sf(ys)
    mean = jnp.mean(y, axis=-1, keepdims=True)
    var = jnp.mean(jnp.square(y - mean), axis=-1, keepdims=True)
    y = ((y - mean) * lax.rsqrt(var + GN_EPS)).reshape(n, t, D_RWKV) * ln_w + ln_b
    bonus = jnp.sum(r_h * k_h * r_k.astype(f32), axis=-1, keepdims=True) * v_h
    y = (y + bonus.reshape(n, t, D_RWKV)) * g
    return y.astype(pb.dtype), pb[:, -1], wkv_new


def _conv_ffn(x, conv_prev, w_gate, w_up, conv_w, conv_b, w_down):
    t = x.shape[1]
    u = x @ w_gate
    ext = jnp.concatenate([conv_prev.astype(u.dtype), u], axis=1)
    c = conv_b + sum(conv_w[j] * ext[:, j:j + t] for j in range(CONV_WIDTH))
    y = (jax.nn.silu(c) * (x @ w_up)) @ w_down
    return y, ext[:, t:]


def _layer(h, p, cache_k, cache_v, shift_prev, wkv_prev, conv_prev, rel_bias,
           g_mix, w_in, q_norm_g, k_norm_g, mu_shift, w0, w2, a0, a2, g2, k_k, k_a, r_k,
           ln_x_w, ln_x_b, w_out, g_ffn, w_gate, w_up, conv_w, conv_b, w_down, g_ple, w_ple, w_ple_gate):
    n, t, _ = h.shape
    proj = _rms(h, g_mix) @ w_in
    heads = lambda z: z.reshape(n, t, N_HEADS_A, HEAD_DIM_A)
    q = _rms(heads(proj[..., :D_ATT]), q_norm_g)
    k = _rms(heads(proj[..., D_ATT:2 * D_ATT]), k_norm_g)
    v = heads(proj[..., 2 * D_ATT:3 * D_ATT])
    o_att = _dilated_attention(q, k, v, rel_bias, cache_k, cache_v)
    if cache_k is None:
        keep = min(MAX_WINDOW, t)
        k_rows, v_rows = k[:, t - keep:], v[:, t - keep:]
    else:
        k_rows, v_rows = k, v
    o_rwkv, shift_new, wkv_new = _rwkv7(proj[..., 3 * D_ATT:], shift_prev, wkv_prev, mu_shift, w0, w2,
                                        a0, a2, g2, k_k, k_a, r_k, ln_x_w, ln_x_b)
    h = h + jnp.concatenate([o_att, o_rwkv], axis=-1) @ w_out
    f, conv_new = _conv_ffn(_rms(h, g_ffn), conv_prev, w_gate, w_up, conv_w, conv_b, w_down)
    h = h + f
    h = h + (p.astype(h.dtype) @ w_ple) * jax.nn.sigmoid(_rms(h, g_ple) @ w_ple_gate)
    return h, k_rows, v_rows, shift_new, wkv_new, conv_new


def setup_inputs(seed: int = 0) -> dict:
    key = jax.random.key(seed)
    keys = iter(jax.random.split(key, 40))
    f32 = jnp.float32

    def nrm(shape, scale):
        return jax.random.normal(next(keys), shape, f32) * scale

    def gain(shape):
        return 1.0 + nrm(shape, 0.05)

    win_buf = min(MAX_WINDOW, PAST_LEN)
    return {
        'x_prompt': nrm((BATCH, SEQ, D_MODEL), 1.0),
        'x_sample': nrm((DEC_BATCH, DEC_SEQ, D_MODEL), 1.0),
        'cache_k': nrm((DEPTH, DEC_BATCH, win_buf, N_HEADS_A, HEAD_DIM_A), 1.0),
        'cache_v': nrm((DEPTH, DEC_BATCH, win_buf, N_HEADS_A, HEAD_DIM_A), 1.0),
        'state_shift': nrm((DEPTH, DEC_BATCH, B_COLS), 1.0),
        'state_wkv': nrm((DEPTH, DEC_BATCH, N_HEADS_B, HEAD_DIM_B, HEAD_DIM_B), 0.3),
        'state_conv': nrm((DEPTH, DEC_BATCH, CONV_WIDTH - 1, D_FF), 1.0),
        'p_prompt': nrm((DEPTH, BATCH, SEQ, PLE_DIM), 1.0),
        'p_sample': nrm((DEPTH, DEC_BATCH, DEC_SEQ, PLE_DIM), 1.0),
        'rel_bias': nrm((NUM_BUCKETS, N_HEADS_A), 0.5),
        'g_mix': gain((DEPTH, D_MODEL)),
        'w_in': nrm((DEPTH, D_MODEL, IN_COLS), D_MODEL ** -0.5),
        'q_norm_g': gain((DEPTH, HEAD_DIM_A)),
        'k_norm_g': gain((DEPTH, HEAD_DIM_A)),
        'mu_shift': jax.random.uniform(next(keys), (DEPTH, B_COLS), f32),
        'w0': -1.0 + nrm((DEPTH, D_RWKV), 0.5),
        'w2': nrm((DEPTH, DECAY_LORA, D_RWKV), 0.5 * DECAY_LORA ** -0.5),
        'a0': nrm((DEPTH, D_RWKV), 0.5),
        'a2': nrm((DEPTH, AAA_LORA, D_RWKV), AAA_LORA ** -0.5),
        'g2': nrm((DEPTH, GATE_LORA, D_RWKV), GATE_LORA ** -0.5),
        'k_k': 0.85 + nrm((DEPTH, D_RWKV), 0.05),
        'k_a': gain((DEPTH, D_RWKV)),
        'r_k': nrm((DEPTH, N_HEADS_B, HEAD_DIM_B), 0.1),
        'ln_x_w': gain((DEPTH, D_RWKV)),
        'ln_x_b': nrm((DEPTH, D_RWKV), 0.02),
        'w_out': nrm((DEPTH, D_MIX, D_MODEL), D_MIX ** -0.5),
        'g_ffn': gain((DEPTH, D_MODEL)),
        'w_gate': nrm((DEPTH, D_MODEL, D_FF), D_MODEL ** -0.5),
        'w_up': nrm((DEPTH, D_MODEL, D_FF), D_MODEL ** -0.5),
        'conv_w': nrm((DEPTH, CONV_WIDTH, D_FF), CONV_WIDTH ** -0.5),
        'conv_b': nrm((DEPTH, D_FF), 0.02),
        'w_down': nrm((DEPTH, D_FF, D_MODEL), D_FF ** -0.5),
        'g_ple': gain((DEPTH, D_MODEL)),
        'w_ple': nrm((DEPTH, PLE_DIM, D_MODEL), PLE_DIM ** -0.5),
        'w_ple_gate': nrm((DEPTH, D_MODEL, D_MODEL), D_MODEL ** -0.5),
    }


def reference(x_prompt, x_sample, cache_k, cache_v, state_shift, state_wkv, state_conv, p_prompt, p_sample,
              rel_bias, g_mix, w_in, q_norm_g, k_norm_g, mu_shift, w0, w2, a0, a2, g2, k_k, k_a, r_k,
              ln_x_w, ln_x_b, w_out, g_ffn, w_gate, w_up, conv_w, conv_b, w_down, g_ple, w_ple, w_ple_gate):
    nbp = x_prompt.shape[0]
    hp, hs = x_prompt, x_sample
    prompt_states, sample_states = [], []
    for i in range(DEPTH):
        lp = (g_mix[i], w_in[i], q_norm_g[i], k_norm_g[i], mu_shift[i], w0[i], w2[i], a0[i], a2[i], g2[i],
              k_k[i], k_a[i], r_k[i], ln_x_w[i], ln_x_b[i], w_out[i], g_ffn[i], w_gate[i], w_up[i],
              conv_w[i], conv_b[i], w_down[i], g_ple[i], w_ple[i], w_ple_gate[i])
        zero_shift = jnp.zeros((nbp, B_COLS), x_prompt.dtype)
        zero_wkv = jnp.zeros((nbp, N_HEADS_B, HEAD_DIM_B, HEAD_DIM_B), jnp.float32)
        zero_conv = jnp.zeros((nbp, CONV_WIDTH - 1, D_FF), x_prompt.dtype)
        hp, *sp = _layer(hp, p_prompt[i], None, None, zero_shift, zero_wkv, zero_conv, rel_bias, *lp)
        hs, *ss = _layer(hs, p_sample[i], cache_k[i], cache_v[i], state_shift[i], state_wkv[i],
                         state_conv[i], rel_bias, *lp)
        prompt_states.append(sp)
        sample_states.append(ss)
    k_prompt = jnp.stack([s[0] for s in prompt_states])
    v_prompt = jnp.stack([s[1] for s in prompt_states])
    shift_prompt = jnp.stack([s[2] for s in prompt_states])
    wkv_prompt = jnp.stack([s[3] for s in prompt_states])
    conv_prompt = jnp.stack([s[4] for s in prompt_states])
    k_sample = jnp.stack([s[0] for s in sample_states])
    v_sample = jnp.stack([s[1] for s in sample_states])
    shift_sample = jnp.stack([s[2] for s in sample_states])
    wkv_sample = jnp.stack([s[3] for s in sample_states])
    conv_sample = jnp.stack([s[4] for s in sample_states])
    return (hp, hs, k_prompt, v_prompt, shift_prompt, wkv_prompt, conv_prompt,
            k_sample, v_sample, shift_sample, wkv_sample, conv_sample)
```

```python
import functools
import math

import jax
import jax.numpy as jnp
from jax import lax
from jax.experimental import pallas as pl
from jax.experimental.pallas import tpu as pltpu

F32 = jnp.float32
BF16 = jnp.bfloat16

HEAD_DIM_A = 128
HEAD_DIM_B = 64
DILATION_PAIRS = ((128, 1), (512, 4), (2048, 16))
MAX_WINDOW = max(w for w, _ in DILATION_PAIRS)
NUM_BUCKETS = 32
REL_MAX_DIST = MAX_WINDOW
DECAY_LORA = 128
AAA_LORA = 128
CONV_WIDTH = 3
RMS_EPS = 1e-6
GN_EPS = 64e-5
NEG_INF = -1e30

LANE = 128
FF_ALIGN = 512
VMEM_LIMIT = 56 * 1024 * 1024


def _cparams(sem):
    return pltpu.CompilerParams(dimension_semantics=sem, vmem_limit_bytes=VMEM_LIMIT)


def _round_up(x, m):
    return -(-x // m) * m


def _rms_kernel(x_ref, g_ref, o_ref):
    x = x_ref[...]
    ms = jnp.mean(x * x, axis=-1, keepdims=True)
    o_ref[...] = (x * lax.rsqrt(ms + RMS_EPS) * g_ref[...]).astype(o_ref.dtype)


def _rmsnorm(x, g, tm):
    m, d = x.shape
    return pl.pallas_call(
        _rms_kernel,
        grid=(m // tm,),
        in_specs=[pl.BlockSpec((tm, d), lambda i: (i, 0)), pl.BlockSpec((1, d), lambda i: (0, 0))],
        out_specs=pl.BlockSpec((tm, d), lambda i: (i, 0)),
        out_shape=jax.ShapeDtypeStruct((m, d), BF16),
        compiler_params=_cparams(("parallel",)),
        name="rmsnorm",
    )(x, g.reshape(1, d))


def _mm_kernel(*refs, nk, mode):
    a_ref, b_ref = refs[0], refs[1]
    o_ref = refs[-2] if nk > 1 else refs[-1]
    acc_ref = refs[-1] if nk > 1 else None

    def epilogue(acc):
        if mode == "plain":
            return acc
        if mode == "residual":
            return refs[2][...] + acc
        h_ref, p_ref, wp_ref = refs[2], refs[3], refs[4]
        ple = jnp.dot(p_ref[...], wp_ref[...], preferred_element_type=F32)
        return h_ref[...] + ple * jax.nn.sigmoid(acc)

    part = jnp.dot(a_ref[...], b_ref[...], preferred_element_type=F32)
    if nk == 1:
        o_ref[...] = epilogue(part).astype(o_ref.dtype)
        return
    k = pl.program_id(2)

    @pl.when(k == 0)
    def _():
        acc_ref[...] = part

    @pl.when(k > 0)
    def _():
        acc_ref[...] += part

    @pl.when(k == nk - 1)
    def _():
        o_ref[...] = epilogue(acc_ref[...]).astype(o_ref.dtype)


def _matmul(a, b, *, tm, tn, tk, mode="plain", extras=(), out_dtype=F32, name="matmul"):
    m, kdim = a.shape
    n = b.shape[1]
    nk = kdim // tk
    in_specs = [pl.BlockSpec((tm, tk), lambda i, j, k: (i, k)), pl.BlockSpec((tk, tn), lambda i, j, k: (k, j))]
    if mode == "residual":
        in_specs.append(pl.BlockSpec((tm, tn), lambda i, j, k: (i, j)))
    elif mode == "ple":
        pdim = extras[1].shape[1]
        in_specs += [pl.BlockSpec((tm, tn), lambda i, j, k: (i, j)),
                     pl.BlockSpec((tm, pdim), lambda i, j, k: (i, 0)),
                     pl.BlockSpec((pdim, tn), lambda i, j, k: (0, j))]
    scratch = [pltpu.VMEM((tm, tn), F32)] if nk > 1 else []
    return pl.pallas_call(
        functools.partial(_mm_kernel, nk=nk, mode=mode),
        grid=(m // tm, n // tn, nk),
        in_specs=in_specs,
        out_specs=pl.BlockSpec((tm, tn), lambda i, j, k: (i, j)),
        out_shape=jax.ShapeDtypeStruct((m, n), out_dtype),
        scratch_shapes=scratch,
        compiler_params=_cparams(("parallel", "parallel", "arbitrary")),
        name=name,
    )(a, b, *extras)


def _ffn_kernel(a_ref, wg_ref, wu_ref, cw_ref, cb_ref, prev_ref, act_ref, tail_ref, ext_ref, carry_ref,
                *, tm, shift, tiles_per_group):
    i = pl.program_id(0)
    j = pl.program_id(1)
    hist = (CONV_WIDTH - 1) * shift
    pad = _round_up(hist, 8)
    a = a_ref[...]
    u = jnp.dot(a, wg_ref[...], preferred_element_type=F32)
    up = jnp.dot(a, wu_ref[...], preferred_element_type=F32)

    @pl.when(i % tiles_per_group == 0)
    def _():
        ext_ref[pl.ds(pad - hist, hist), :] = prev_ref[0]

    @pl.when(i % tiles_per_group != 0)
    def _():
        ext_ref[pl.ds(pad - hist, hist), :] = carry_ref[j]

    ext_ref[pl.ds(pad, tm), :] = u
    cw = cw_ref[...]
    c = (cb_ref[...] + cw[0:1] * ext_ref[pl.ds(pad - 2 * shift, tm), :]
         + cw[1:2] * ext_ref[pl.ds(pad - shift, tm), :] + cw[2:3] * u)
    act_ref[...] = (jax.nn.silu(c) * up).astype(act_ref.dtype)
    tail = ext_ref[pl.ds(pad + tm - hist, hist), :]
    carry_ref[j] = tail
    tail_ref[0] = tail


def _ffn_act(xn, wg, wu, conv_w, conv_b, conv_prev, *, tm, tf, shift, tiles_per_group):
    m, d = xn.shape
    dffp = wg.shape[1]
    hist = (CONV_WIDTH - 1) * shift
    nj = dffp // tf
    kern = functools.partial(_ffn_kernel, tm=tm, shift=shift, tiles_per_group=tiles_per_group)
    return pl.pallas_call(
        kern,
        grid=(m // tm, nj),
        in_specs=[
            pl.BlockSpec((tm, d), lambda i, j: (i, 0)),
            pl.BlockSpec((d, tf), lambda i, j: (0, j)),
            pl.BlockSpec((d, tf), lambda i, j: (0, j)),
            pl.BlockSpec((CONV_WIDTH, tf), lambda i, j: (0, j)),
            pl.BlockSpec((1, tf), lambda i, j: (0, j)),
            pl.BlockSpec((1, hist, tf), lambda i, j: (i // tiles_per_group, 0, j)),
        ],
        out_specs=[
            pl.BlockSpec((tm, tf), lambda i, j: (i, j)),
            pl.BlockSpec((1, hist, tf), lambda i, j: (i, 0, j)),
        ],
        out_shape=[jax.ShapeDtypeStruct((m, dffp), BF16), jax.ShapeDtypeStruct((m // tm, hist, dffp), F32)],
        scratch_shapes=[pltpu.VMEM((_round_up(hist, 8) + tm, tf), F32), pltpu.VMEM((nj, hist, tf), F32)],
        compiler_params=_cparams(("arbitrary", "arbitrary")),
        name="ffn_gate_up_conv",
    )(xn, wg, wu, conv_w, conv_b.reshape(1, dffp), conv_prev)


def _rms_jnp(x, g):
    return x * lax.rsqrt(jnp.mean(x * x, axis=-1, keepdims=True) + RMS_EPS) * g


def _rel_bucket(dist):
    max_exact = NUM_BUCKETS // 2
    d_f = jnp.maximum(dist, 1).astype(F32)
    large = max_exact + (jnp.log(d_f / max_exact) / math.log(REL_MAX_DIST / max_exact)
                         * (NUM_BUCKETS - max_exact)).astype(jnp.int32)
    large = jnp.minimum(large, NUM_BUCKETS - 1)
    return jnp.where(dist < max_exact, dist, large)


def _attn_jnp(q, k, v, rel_bias, cache_k, cache_v):
    n, t, h, dh = q.shape
    num = 0.0
    outs, lses = [], []
    for window, dil in DILATION_PAIRS:
        sub = window // dil
        if cache_k is None:
            nb = -(-t // (dil * sub))
            tp = nb * sub * dil
            pad = lambda a: jnp.pad(a, ((0, 0), (0, tp - t), (0, 0), (0, 0))).reshape(n, nb, sub, dil, h, dh)
            qb, kb, vb = pad(q), pad(k), pad(v)
            wp = lambda a: jnp.concatenate(
                [jnp.pad(a[:, :-1], ((0, 0), (1, 0), (0, 0), (0, 0), (0, 0), (0, 0))), a], axis=2)
            kk, vv = wp(kb), wp(vb)
            logits = jnp.einsum('bnqrhd,bnkrhd->bnrhqk', qb, kk) / math.sqrt(dh)
            qi = jnp.arange(sub)[:, None]
            ki = jnp.arange(2 * sub)[None, :]
            dsub = qi + sub - ki
            bias = rel_bias[_rel_bucket(dil * jnp.clip(dsub, 0, sub))]
            key_idx = jnp.arange(nb)[:, None] * sub + ki - sub
            mask = ((dsub >= 0) & (dsub <= sub))[None] & (key_idx >= 0)[:, None, :]
            logits = jnp.where(mask[None, :, None, None], logits + jnp.transpose(bias, (2, 0, 1)), NEG_INF)
            m = jnp.max(logits, axis=-1, keepdims=True)
            e = jnp.exp(logits - m)
            s = jnp.sum(e, axis=-1)
            o = jnp.einsum('bnrhqk,bnkrhd->bnqrhd', e, vv)
            o = o / jnp.transpose(s, (0, 1, 4, 2, 3))[..., None]
            lse = jnp.transpose(m[..., 0] + jnp.log(s), (0, 1, 4, 2, 3))
            o, lse = o.reshape(n, tp, h, dh)[:, :t], lse.reshape(n, tp, h)[:, :t]
        else:
            wb = cache_k.shape[1]
            j = jnp.arange(sub + 1)
            idx = wb + jnp.arange(t)[:, None] - dil * j[None, :]
            valid = idx >= 0
            from_new = (idx >= wb)[None, :, :, None, None]
            ic = jnp.clip(idx, 0, wb - 1)
            inew = jnp.clip(idx - wb, 0, t - 1)
            kg = jnp.where(from_new, k[:, inew], cache_k[:, ic])
            vg = jnp.where(from_new, v[:, inew], cache_v[:, ic])
            logits = jnp.einsum('nshd,nsjhd->nhsj', q, kg) / math.sqrt(dh)
            bias = rel_bias[_rel_bucket(dil * j)].T
            logits = jnp.where(valid[None, None], logits + bias[None, :, None, :], NEG_INF)
            m = jnp.max(logits, axis=-1, keepdims=True)
            e = jnp.exp(logits - m)
            s = jnp.sum(e, axis=-1)
            o = jnp.einsum('nhsj,nsjhd->nshd', e, vg)
            o = o / jnp.transpose(s, (0, 2, 1))[..., None]
            lse = jnp.transpose(m[..., 0] + jnp.log(s), (0, 2, 1))
        outs.append(o)
        lses.append(lse)
    wts = jax.nn.softmax(jnp.stack(lses), axis=0)
    o = jnp.sum(wts[..., None] * jnp.stack(outs), axis=0)
    return o.reshape(n, t, h * dh)


def _rwkv_jnp(pb, shift_prev, wkv_prev, mu, w0, w2, a0, a2, g2, k_k, k_a, r_k, ln_w, ln_b):
    n, t, _ = pb.shape
    d_rwkv = w0.shape[0]
    nh = d_rwkv // HEAD_DIM_B
    prev = jnp.concatenate([shift_prev[:, None], pb[:, :-1]], axis=1)
    xs = pb + mu * (prev - pb)
    c1, c2, c3 = d_rwkv, 2 * d_rwkv, 3 * d_rwkv
    c4, c5 = c3 + DECAY_LORA, c3 + DECAY_LORA + AAA_LORA
    r, k, v = xs[..., :c1], xs[..., c1:c2], xs[..., c2:c3]
    wd, ad, gd = xs[..., c3:c4], xs[..., c4:c5], xs[..., c5:]
    w_log = -jax.nn.softplus(-(w0 + jnp.tanh(wd) @ w2)) - 0.5
    decay = jnp.exp(-jnp.exp(w_log))
    a = jax.nn.sigmoid(a0 + ad @ a2)
    g = jax.nn.sigmoid(gd) @ g2
    hd = lambda z: z.reshape(n, t, nh, HEAD_DIM_B)
    r_h, k_h, v_h, a_h, w_h = hd(r), hd(k), hd(v), hd(a), hd(decay)
    kk = k_h * k_k.reshape(nh, HEAD_DIM_B)
    kk = kk / jnp.maximum(jnp.linalg.norm(kk, axis=-1, keepdims=True), 1e-12)
    k_h = k_h * (1.0 + (a_h - 1.0) * k_a.reshape(nh, HEAD_DIM_B))

    def step(state, inp):
        r_t, k_t, v_t, kk_t, a_t, w_t = inp
        sa = jnp.einsum('nhvk,nhk->nhv', state, -kk_t, precision='highest')
        state = (state * w_t[:, :, None, :] + sa[..., None] * (kk_t * a_t)[:, :, None, :]
                 + v_t[..., None] * k_t[:, :, None, :])
        return state, jnp.einsum('nhvk,nhk->nhv', state, r_t, precision='highest')

    sf = lambda z: jnp.swapaxes(z, 0, 1)
    wkv_new, ys = lax.scan(step, wkv_prev, (sf(r_h), sf(k_h), sf(v_h), sf(kk), sf(a_h), sf(w_h)))
    y = sf(ys)
    mean = jnp.mean(y, axis=-1, keepdims=True)
    var = jnp.mean(jnp.square(y - mean), axis=-1, keepdims=True)
    y = ((y - mean) * lax.rsqrt(var + GN_EPS)).reshape(n, t, d_rwkv) * ln_w + ln_b
    bonus = jnp.sum(r_h * k_h * r_k, axis=-1, keepdims=True) * v_h
    y = (y + bonus.reshape(n, t, d_rwkv)) * g
    return y, wkv_new


def _prep_weights(lp):
    (g_mix, w_in, q_norm_g, k_norm_g, mu_shift, w0, w2, a0, a2, g2, k_k, k_a, r_k, ln_x_w, ln_x_b, w_out,
     g_ffn, w_gate, w_up, conv_w, conv_b, w_down, g_ple, w_ple, w_ple_gate) = lp
    d_model = w_in.shape[0]
    d_rwkv = w0.shape[0]
    d_att = w_out.shape[0] - d_rwkv
    main = 3 * d_att + 3 * d_rwkv
    lora = w_in.shape[1] - main
    lora_p = _round_up(lora, LANE)
    d_ff = w_gate.shape[1]
    dffp = _round_up(d_ff, FF_ALIGN)
    padc = lambda w, n: jnp.pad(w, ((0, 0), (0, n - w.shape[1])))
    return dict(
        d_model=d_model, d_att=d_att, d_rwkv=d_rwkv, lora=lora, lora_p=lora_p, d_ff=d_ff, dffp=dffp,
        g_mix=g_mix, g_ffn=g_ffn, g_ple=g_ple, q_norm_g=q_norm_g, k_norm_g=k_norm_g,
        w_in_main=w_in[:, :main].astype(BF16),
        w_in_lora=padc(w_in[:, main:], lora_p).astype(BF16),
        mu_shift=mu_shift, w0=w0, w2=w2, a0=a0, a2=a2, g2=g2, k_k=k_k, k_a=k_a, r_k=r_k,
        ln_x_w=ln_x_w, ln_x_b=ln_x_b,
        w_out=w_out.astype(BF16),
        w_gate=padc(w_gate, dffp).astype(BF16), w_up=padc(w_up, dffp).astype(BF16),
        conv_w=padc(conv_w, dffp), conv_b=jnp.pad(conv_b, (0, dffp - d_ff)),
        w_down=jnp.pad(w_down, ((0, dffp - d_ff), (0, 0))).astype(BF16),
        w_ple=w_ple.astype(BF16), w_ple_gate=w_ple_gate.astype(BF16),
    )


def _layer(x2d, p2d, wp, mixer, *, tm, shift, conv_prev, tiles_per_group):
    d_model, dffp = wp["d_model"], wp["dffp"]
    main = wp["w_in_main"].shape[1]
    tn = 512
    xn = _rmsnorm(x2d, wp["g_mix"], min(tm, 256))
    proj_main = _matmul(xn, wp["w_in_main"], tm=tm, tn=tn, tk=d_model, name="in_proj_main")
    proj_lora = _matmul(xn, wp["w_in_lora"], tm=tm, tn=wp["lora_p"], tk=d_model, name="in_proj_lora")
    mix, aux = mixer(proj_main, proj_lora)
    h1 = _matmul(mix, wp["w_out"], tm=tm, tn=tn, tk=mix.shape[1], mode="residual", extras=(x2d,), name="out_proj")
    hn = _rmsnorm(h1, wp["g_ffn"], min(tm, 256))
    hist = (CONV_WIDTH - 1) * shift
    cprev = jnp.pad(conv_prev, ((0, 0), (0, 0), (0, dffp - conv_prev.shape[-1])))
    act, conv_tail = _ffn_act(hn, wp["w_gate"], wp["w_up"], wp["conv_w"], wp["conv_b"], cprev,
                              tm=tm, tf=FF_ALIGN, shift=shift, tiles_per_group=tiles_per_group)
    tk_down = 1024 if dffp % 1024 == 0 else FF_ALIGN
    h2 = _matmul(act, wp["w_down"], tm=tm, tn=min(1024, d_model), tk=tk_down, mode="residual", extras=(h1,),
                 name="ffn_down")
    hn2 = _rmsnorm(h2, wp["g_ple"], min(tm, 256))
    y = _matmul(hn2, wp["w_ple_gate"], tm=tm, tn=tn, tk=d_model, mode="ple",
                extras=(h2, p2d.astype(BF16), wp["w_ple"]), name="ple_gate")
    conv_tail = conv_tail[tiles_per_group - 1::tiles_per_group, :, :wp["d_ff"]]
    return y, proj_main, proj_lora, conv_tail, aux


def kernel(x_prompt, x_sample, cache_k, cache_v, state_shift, state_wkv, state_conv, p_prompt, p_sample, rel_bias, g_mix, w_in, q_norm_g, k_norm_g, mu_shift, w0, w2, a0, a2, g2, k_k, k_a, r_k, ln_x_w, ln_x_b, w_out, g_ffn, w_gate, w_up, conv_w, conv_b, w_down, g_ple, w_ple, w_ple_gate):
    depth = g_mix.shape[0]
    nbp, seq, d_model = x_prompt.shape
    nbs, dseq, _ = x_sample.shape
    hp = x_prompt.reshape(nbp * seq, d_model)
    hs = jnp.swapaxes(x_sample, 0, 1).reshape(dseq * nbs, d_model)
    outs_p, outs_s = [], []
    for i in range(depth):
        lp = (g_mix[i], w_in[i], q_norm_g[i], k_norm_g[i], mu_shift[i], w0[i], w2[i], a0[i], a2[i], g2[i],
              k_k[i], k_a[i], r_k[i], ln_x_w[i], ln_x_b[i], w_out[i], g_ffn[i], w_gate[i], w_up[i],
              conv_w[i], conv_b[i], w_down[i], g_ple[i], w_ple[i], w_ple_gate[i])
        wp = _prep_weights(lp)
        d_att, d_rwkv, lora, d_ff = wp["d_att"], wp["d_rwkv"], wp["lora"], wp["d_ff"]
        nha = d_att // HEAD_DIM_A
        nhb = d_rwkv // HEAD_DIM_B

        def mixer(proj_main, proj_lora, n, t, time_major, ck, cv, shift_prev, wkv_prev):
            if time_major:
                to_nm = lambda z: jnp.swapaxes(z.reshape(t, n, -1), 0, 1)
            else:
                to_nm = lambda z: z.reshape(n, t, -1)
            pm, plo = to_nm(proj_main), to_nm(proj_lora)
            heads = lambda z: z.reshape(n, t, nha, HEAD_DIM_A)
            q = _rms_jnp(heads(pm[..., :d_att]), q_norm_g[i])
            k = _rms_jnp(heads(pm[..., d_att:2 * d_att]), k_norm_g[i])
            v = heads(pm[..., 2 * d_att:3 * d_att])
            o_att = _attn_jnp(q, k, v, rel_bias, ck, cv)
            pb = jnp.concatenate([pm[..., 3 * d_att:], plo[..., :lora]], axis=-1)
            o_rwkv, wkv_new = _rwkv_jnp(pb, shift_prev, wkv_prev, mu_shift[i], w0[i], w2[i], a0[i], a2[i], g2[i],
                                        k_k[i], k_a[i], r_k[i], ln_x_w[i], ln_x_b[i])
            mixv = jnp.concatenate([o_att, o_rwkv], axis=-1)
            if time_major:
                mixv = jnp.swapaxes(mixv, 0, 1)
            mixv = mixv.reshape(n * t, -1).astype(BF16)
            return mixv, (k, v, pb[:, -1], wkv_new)

        tm_p = min(1024, seq)
        zero_shift = jnp.zeros((nbp, 3 * d_rwkv + lora), F32)
        zero_wkv = jnp.zeros((nbp, nhb, HEAD_DIM_B, HEAD_DIM_B), F32)
        zero_conv = jnp.zeros((nbp, CONV_WIDTH - 1, d_ff), F32)
        mix_p = functools.partial(mixer, n=nbp, t=seq, time_major=False, ck=None, cv=None,
                                  shift_prev=zero_shift, wkv_prev=zero_wkv)
        hp, _, _, conv_p, (k_p, v_p, shift_p, wkv_p) = _layer(
            hp, p_prompt[i].reshape(nbp * seq, -1), wp, mix_p, tm=tm_p, shift=1, conv_prev=zero_conv,
            tiles_per_group=seq // tm_p)
        keep = min(MAX_WINDOW, seq)
        outs_p.append((k_p[:, seq - keep:], v_p[:, seq - keep:], shift_p, wkv_p, conv_p))

        mix_s = functools.partial(mixer, n=nbs, t=dseq, time_major=True, ck=cache_k[i], cv=cache_v[i],
                                  shift_prev=state_shift[i], wkv_prev=state_wkv[i])
        conv_prev_s = jnp.swapaxes(state_conv[i], 0, 1).reshape(1, (CONV_WIDTH - 1) * nbs, d_ff)
        p_s = jnp.swapaxes(p_sample[i], 0, 1).reshape(dseq * nbs, -1)
        hs, _, _, conv_s, (k_s, v_s, shift_s, wkv_s) = _layer(
            hs, p_s, wp, mix_s, tm=dseq * nbs, shift=nbs, conv_prev=conv_prev_s, tiles_per_group=1)
        conv_s = jnp.swapaxes(conv_s.reshape(CONV_WIDTH - 1, nbs, d_ff), 0, 1)
        outs_s.append((k_s, v_s, shift_s, wkv_s, conv_s))

    y_p = hp.reshape(nbp, seq, d_model)
    y_s = jnp.swapaxes(hs.reshape(dseq, nbs, d_model), 0, 1)
    st = lambda outs, idx: jnp.stack([o[idx] for o in outs])
    return (y_p, y_s, st(outs_p, 0), st(outs_p, 1), st(outs_p, 2), st(outs_p, 3), st(outs_p, 4),
            st(outs_s, 0), st(outs_s, 1), st(outs_s, 2), st(outs_s, 3), st(outs_s, 4))
```

```python
import functools
import math

import jax
import jax.numpy as jnp
from jax import lax
from jax.experimental import pallas as pl
from jax.experimental.pallas import tpu as pltpu

F32 = jnp.float32
BF16 = jnp.bfloat16

HEAD_DIM_A = 128
HEAD_DIM_B = 64
DILATION_PAIRS = ((128, 1), (512, 4), (2048, 16))
MAX_WINDOW = max(w for w, _ in DILATION_PAIRS)
NUM_BUCKETS = 32
REL_MAX_DIST = MAX_WINDOW
DECAY_LORA = 128
AAA_LORA = 128
CONV_WIDTH = 3
RMS_EPS = 1e-6
GN_EPS = 64e-5
NEG_INF = -1e30

LANE = 128
FF_ALIGN = 512
VMEM_LIMIT = 56 * 1024 * 1024


def _cparams(sem):
    return pltpu.CompilerParams(dimension_semantics=sem, vmem_limit_bytes=VMEM_LIMIT)


def _round_up(x, m):
    return -(-x // m) * m


def _rms_kernel(x_ref, g_ref, o_ref):
    x = x_ref[...]
    ms = jnp.mean(x * x, axis=-1, keepdims=True)
    o_ref[...] = (x * lax.rsqrt(ms + RMS_EPS) * g_ref[...]).astype(o_ref.dtype)


def _rmsnorm(x, g, tm):
    m, d = x.shape
    return pl.pallas_call(
        _rms_kernel,
        grid=(m // tm,),
        in_specs=[pl.BlockSpec((tm, d), lambda i: (i, 0)), pl.BlockSpec((1, d), lambda i: (0, 0))],
        out_specs=pl.BlockSpec((tm, d), lambda i: (i, 0)),
        out_shape=jax.ShapeDtypeStruct((m, d), BF16),
        compiler_params=_cparams(("parallel",)),
        name="rmsnorm",
    )(x, g.reshape(1, d))


def _mm_kernel(*refs, nk, mode):
    a_ref, b_ref = refs[0], refs[1]
    o_ref = refs[-2] if nk > 1 else refs[-1]
    acc_ref = refs[-1] if nk > 1 else None

    def epilogue(acc):
        if mode == "plain":
            return acc
        if mode == "residual":
            return refs[2][...] + acc
        h_ref, p_ref, wp_ref = refs[2], refs[3], refs[4]
        ple = jnp.dot(p_ref[...], wp_ref[...], preferred_element_type=F32)
        return h_ref[...] + ple * jax.nn.sigmoid(acc)

    part = jnp.dot(a_ref[...], b_ref[...], preferred_element_type=F32)
    if nk == 1:
        o_ref[...] = epilogue(part).astype(o_ref.dtype)
        return
    k = pl.program_id(2)

    @pl.when(k == 0)
    def _():
        acc_ref[...] = part

    @pl.when(k > 0)
    def _():
        acc_ref[...] += part

    @pl.when(k == nk - 1)
    def _():
        o_ref[...] = epilogue(acc_ref[...]).astype(o_ref.dtype)


def _matmul(a, b, *, tm, tn, tk, mode="plain", extras=(), out_dtype=F32, name="matmul"):
    m, kdim = a.shape
    n = b.shape[1]
    nk = kdim // tk
    in_specs = [pl.BlockSpec((tm, tk), lambda i, j, k: (i, k)), pl.BlockSpec((tk, tn), lambda i, j, k: (k, j))]
    if mode == "residual":
        in_specs.append(pl.BlockSpec((tm, tn), lambda i, j, k: (i, j)))
    elif mode == "ple":
        pdim = extras[1].shape[1]
        in_specs += [pl.BlockSpec((tm, tn), lambda i, j, k: (i, j)),
                     pl.BlockSpec((tm, pdim), lambda i, j, k: (i, 0)),
                     pl.BlockSpec((pdim, tn), lambda i, j, k: (0, j))]
    scratch = [pltpu.VMEM((tm, tn), F32)] if nk > 1 else []
    return pl.pallas_call(
        functools.partial(_mm_kernel, nk=nk, mode=mode),
        grid=(m // tm, n // tn, nk),
        in_specs=in_specs,
        out_specs=pl.BlockSpec((tm, tn), lambda i, j, k: (i, j)),
        out_shape=jax.ShapeDtypeStruct((m, n), out_dtype),
        scratch_shapes=scratch,
        compiler_params=_cparams(("parallel", "parallel", "arbitrary")),
        name=name,
    )(a, b, *extras)


def _ffn_kernel(a_ref, wg_ref, wu_ref, cw_ref, cb_ref, prev_ref, act_ref, tail_ref, ext_ref, carry_ref,
                *, tm, shift, tiles_per_group):
    i = pl.program_id(0)
    j = pl.program_id(1)
    hist = (CONV_WIDTH - 1) * shift
    pad = _round_up(hist, 8)
    a = a_ref[...]
    u = jnp.dot(a, wg_ref[...], preferred_element_type=F32)
    up = jnp.dot(a, wu_ref[...], preferred_element_type=F32)

    @pl.when(i % tiles_per_group == 0)
    def _():
        ext_ref[pl.ds(pad - hist, hist), :] = prev_ref[0]

    @pl.when(i % tiles_per_group != 0)
    def _():
        ext_ref[pl.ds(pad - hist, hist), :] = carry_ref[j]

    ext_ref[pl.ds(pad, tm), :] = u
    cw = cw_ref[...]
    c = (cb_ref[...] + cw[0:1] * ext_ref[pl.ds(pad - 2 * shift, tm), :]
         + cw[1:2] * ext_ref[pl.ds(pad - shift, tm), :] + cw[2:3] * u)
    act_ref[...] = (jax.nn.silu(c) * up).astype(act_ref.dtype)
    tail = ext_ref[pl.ds(pad + tm - hist, hist), :]
    carry_ref[j] = tail
    tail_ref[0] = tail


def _ffn_act(xn, wg, wu, conv_w, conv_b, conv_prev, *, tm, tf, shift, tiles_per_group):
    m, d = xn.shape
    dffp = wg.shape[1]
    hist = (CONV_WIDTH - 1) * shift
    nj = dffp // tf
    kern = functools.partial(_ffn_kernel, tm=tm, shift=shift, tiles_per_group=tiles_per_group)
    return pl.pallas_call(
        kern,
        grid=(m // tm, nj),
        in_specs=[
            pl.BlockSpec((tm, d), lambda i, j: (i, 0)),
            pl.BlockSpec((d, tf), lambda i, j: (0, j)),
            pl.BlockSpec((d, tf), lambda i, j: (0, j)),
            pl.BlockSpec((CONV_WIDTH, tf), lambda i, j: (0, j)),
            pl.BlockSpec((1, tf), lambda i, j: (0, j)),
            pl.BlockSpec((1, hist, tf), lambda i, j: (i // tiles_per_group, 0, j)),
        ],
        out_specs=[
            pl.BlockSpec((tm, tf), lambda i, j: (i, j)),
            pl.BlockSpec((1, hist, tf), lambda i, j: (i, 0, j)),
        ],
        out_shape=[jax.ShapeDtypeStruct((m, dffp), BF16), jax.ShapeDtypeStruct((m // tm, hist, dffp), F32)],
        scratch_shapes=[pltpu.VMEM((_round_up(hist, 8) + tm, tf), F32), pltpu.VMEM((nj, hist, tf), F32)],
        compiler_params=_cparams(("arbitrary", "arbitrary")),
        name="ffn_gate_up_conv",
    )(xn, wg, wu, conv_w, conv_b.reshape(1, dffp), conv_prev)


RWKV_CHUNK = 64
_HI = lax.Precision.HIGHEST


def _dot(a, b, prec=_HI):
    return jnp.dot(a, b, preferred_element_type=F32, precision=prec)


def _dot_nt(a, b, prec=_HI):
    return lax.dot_general(a, b, (((1,), (1,)), ((), ())), preferred_element_type=F32, precision=prec)


def _softplus(z):
    return jnp.maximum(z, 0.0) + jnp.log(1.0 + jnp.exp(-jnp.abs(z)))


def _rwkv_kernel(pr_ref, pk_ref, pv_ref, plo_ref, spr_ref, spk_ref, spv_ref, splo_ref,
                 mur_ref, muk_ref, muv_ref, mulo_ref, w0_ref, a0_ref, w2_ref, a2_ref, g2_ref,
                 kk_ref, ka_ref, rk_ref, lnw_ref, lnb_ref, z0_ref,
                 o_ref, zout_ref,
                 z_s, cr_s, ck_s, cv_s, clo_s, a_s, b_s, r_s, k_s, v_s, lw_s, y_s,
                 *, nb, tb, t_valid):
    ti = pl.program_id(2)
    nt = pl.num_programs(2)
    cs = RWKV_CHUNK
    hd = HEAD_DIM_B
    nchunk = tb // cs

    lane = lax.broadcasted_iota(jnp.int32, (1, LANE), 1)
    head_mask = [(lane // hd == h).astype(F32) for h in range(2)]
    li = lax.broadcasted_iota(jnp.int32, (LANE, LANE), 0)
    lj = lax.broadcasted_iota(jnp.int32, (LANE, LANE), 1)
    blockdiag = (li // hd == lj // hd).astype(F32)
    eye128 = (li == lj).astype(F32)
    ci = lax.broadcasted_iota(jnp.int32, (cs, cs), 0)
    cj = lax.broadcasted_iota(jnp.int32, (cs, cs), 1)
    tril_incl = (ci >= cj).astype(F32)
    tril_strict = (ci > cj).astype(F32)
    eye_c = (ci == cj).astype(F32)
    tri_mask = jnp.concatenate([tril_strict, tril_incl], axis=0)
    row = lax.broadcasted_iota(jnp.int32, (tb, 1), 0)
    valid = ((ti * tb + row) < t_valid).astype(F32)

    def shifted(x, carry_row, mu):
        prev = jnp.where(row == 0, carry_row, pltpu.roll(x, 1, 0))
        return x + mu * (prev - x)

    def per_seq(s, _):
        @pl.when(ti == 0)
        def _():
            z_s[s] = jnp.zeros((LANE, LANE), F32)
            z_s[s, 0:hd, 0:hd] = z0_ref[s, 0]
            z_s[s, hd:2 * hd, hd:2 * hd] = z0_ref[s, 1]
            cr_s[s] = spr_ref[s]
            ck_s[s] = spk_ref[s]
            cv_s[s] = spv_ref[s]
            clo_s[s] = splo_ref[s]

        pr, pk, pv, plo = pr_ref[s], pk_ref[s], pv_ref[s], plo_ref[s]
        xr = shifted(pr, cr_s[s], mur_ref[...])
        xk = shifted(pk, ck_s[s], muk_ref[...])
        xv = shifted(pv, cv_s[s], muv_ref[...])
        xl = shifted(plo, clo_s[s], mulo_ref[...])
        cr_s[s] = pr[tb - 1:tb]
        ck_s[s] = pk[tb - 1:tb]
        cv_s[s] = pv[tb - 1:tb]
        clo_s[s] = plo[tb - 1:tb]

        wd = xl[:, 0:DECAY_LORA]
        ad = xl[:, DECAY_LORA:DECAY_LORA + AAA_LORA]
        gd = xl[:, DECAY_LORA + AAA_LORA:]
        wl = w0_ref[...] + jnp.dot(jnp.tanh(wd).astype(BF16), w2_ref[...], preferred_element_type=F32)
        lw = -jnp.exp(-_softplus(-wl) - 0.5)
        a = jax.nn.sigmoid(a0_ref[...] + jnp.dot(ad.astype(BF16), a2_ref[...], preferred_element_type=F32))
        g = jnp.dot(jax.nn.sigmoid(gd).astype(BF16), g2_ref[...], preferred_element_type=F32)
        kk = xk * kk_ref[...]
        kk = kk / jnp.maximum(jnp.sqrt(_dot(kk * kk, blockdiag)), 1e-12)
        kmod = xk * (1.0 + (a - 1.0) * ka_ref[...])
        a_s[...] = -kk * valid
        b_s[...] = kk * a * valid
        r_s[...] = xr
        k_s[...] = kmod * valid
        v_s[...] = xv
        lw_s[...] = lw * valid

        def per_chunk(c, _):
            rows = pl.ds(pl.multiple_of(c * cs, cs), cs)
            lwc = lw_s[rows, :]
            cum = _dot(tril_incl, lwc)
            tot = cum[cs - 1:cs, :]
            e_in = jnp.exp(cum)
            e_inv = jnp.exp(-cum)
            e_end = jnp.exp(tot - cum)
            at = a_s[rows, :] * jnp.exp(cum - lwc)
            rt = r_s[rows, :] * e_in
            bc, kc, vc = b_s[rows, :], k_s[rows, :], v_s[rows, :]
            bt = bc * e_inv
            kt = kc * e_inv
            lhs = jnp.concatenate([at, rt], axis=0)
            w_all = jnp.zeros((cs, LANE), F32)
            u0_all = jnp.zeros((cs, LANE), F32)
            y0_all = jnp.zeros((cs, LANE), F32)
            arb = []
            for h in range(2):
                lm = lhs * head_mask[h]
                g1 = _dot_nt(lm, bt) * tri_mask
                g2m = _dot_nt(lm, kt) * tri_mask
                n_pow = g1[0:cs]
                arb.append(g1[cs:2 * cs])
                tinv = eye_c + n_pow
                for _ in range(int(math.log2(cs)) - 1):
                    n_pow = _dot(n_pow, n_pow)
                    tinv = tinv + _dot(tinv, n_pow)
                av = _dot(g2m, vc * head_mask[h])
                wu = _dot(tinv, jnp.concatenate([at * head_mask[h], av[0:cs]], axis=1))
                w_all = w_all + wu[:, 0:LANE]
                u0_all = u0_all + wu[:, LANE:2 * LANE]
                y0_all = y0_all + av[cs:2 * cs]
            z = z_s[s]
            yu = _dot(jnp.concatenate([rt, w_all], axis=0), z)
            u = yu[cs:2 * cs] + u0_all
            y = (yu[0:cs] + y0_all + _dot(arb[0], u) * head_mask[0] + _dot(arb[1], u) * head_mask[1])
            y_s[rows, :] = y
            mt = jnp.concatenate([bc * e_end, kc * e_end], axis=0).T
            dg = eye128 * jnp.exp(tot)
            z_new = _dot(jnp.concatenate([mt, dg], axis=1), jnp.concatenate([u, vc, z], axis=0))
            z_s[s] = z_new * blockdiag
            return 0

        lax.fori_loop(0, nchunk, per_chunk, 0)

        y = y_s[...]
        mean = _dot(y, blockdiag) * (1.0 / hd)
        d = y - mean
        var = _dot(d * d, blockdiag) * (1.0 / hd)
        yn = d * lax.rsqrt(var + GN_EPS) * lnw_ref[...] + lnb_ref[...]
        bonus = _dot(xr * kmod * rk_ref[...], blockdiag) * xv
        o_ref[s] = ((yn + bonus) * g).astype(o_ref.dtype)

        @pl.when(ti == nt - 1)
        def _():
            z = z_s[s]
            zout_ref[s, 0] = z[0:hd, 0:hd]
            zout_ref[s, 1] = z[hd:2 * hd, hd:2 * hd]

        return 0

    lax.fori_loop(0, nb, per_seq, 0)


def _rwkv_mixer(pm3, plo3, shift_prev, wkv_prev, wp, *, nb, tb, t_valid):
    n, t, _ = pm3.shape
    d_att, d_rwkv, lora, lora_p = wp["d_att"], wp["d_rwkv"], wp["lora"], wp["lora_p"]
    nh = d_rwkv // HEAD_DIM_B
    npair = d_rwkv // LANE
    glora = lora - DECAY_LORA - AAA_LORA
    gpad = lora_p - DECAY_LORA - AAA_LORA
    cb = 3 * d_att // LANE
    rb = d_rwkv // LANE
    sp_main = shift_prev[:, None, :3 * d_rwkv]
    sp_lora = jnp.pad(shift_prev[:, None, 3 * d_rwkv:], ((0, 0), (0, 0), (0, lora_p - lora)))
    mu = wp["mu_shift"]
    mu_main = mu[None, :3 * d_rwkv]
    mu_lora = jnp.pad(mu[None, 3 * d_rwkv:], ((0, 0), (0, lora_p - lora)))
    row2 = lambda v: v.reshape(1, d_rwkv)
    g2p = jnp.pad(wp["g2"], ((0, gpad - glora), (0, 0))).astype(BF16)
    z0 = jnp.swapaxes(wkv_prev, -1, -2)

    blk3 = lambda off: pl.BlockSpec((nb, tb, LANE), lambda i, h, j: (i, j, off + h))
    sp3 = lambda off: pl.BlockSpec((nb, 1, LANE), lambda i, h, j: (i, 0, off + h))
    vec = lambda off: pl.BlockSpec((1, LANE), lambda i, h, j: (0, off + h))
    full2 = lambda r, c: pl.BlockSpec((r, c), lambda i, h, j: (0, 0))
    kern = functools.partial(_rwkv_kernel, nb=nb, tb=tb, t_valid=t_valid)
    o, zout = pl.pallas_call(
        kern,
        grid=(n // nb, npair, t // tb),
        in_specs=[
            blk3(cb), blk3(cb + rb), blk3(cb + 2 * rb),
            pl.BlockSpec((nb, tb, lora_p), lambda i, h, j: (i, j, 0)),
            sp3(0), sp3(rb), sp3(2 * rb),
            pl.BlockSpec((nb, 1, lora_p), lambda i, h, j: (i, 0, 0)),
            vec(0), vec(rb), vec(2 * rb), full2(1, lora_p),
            vec(0), vec(0),
            pl.BlockSpec((DECAY_LORA, LANE), lambda i, h, j: (0, h)),
            pl.BlockSpec((AAA_LORA, LANE), lambda i, h, j: (0, h)),
            pl.BlockSpec((gpad, LANE), lambda i, h, j: (0, h)),
            vec(0), vec(0), vec(0), vec(0), vec(0),
            pl.BlockSpec((nb, 2, HEAD_DIM_B, HEAD_DIM_B), lambda i, h, j: (i, h, 0, 0)),
        ],
        out_specs=[
            pl.BlockSpec((nb, tb, LANE), lambda i, h, j: (i, j, h)),
            pl.BlockSpec((nb, 2, HEAD_DIM_B, HEAD_DIM_B), lambda i, h, j: (i, h, 0, 0)),
        ],
        out_shape=[jax.ShapeDtypeStruct((n, t, d_rwkv), BF16),
                   jax.ShapeDtypeStruct((n, nh, HEAD_DIM_B, HEAD_DIM_B), F32)],
        scratch_shapes=[pltpu.VMEM((nb, LANE, LANE), F32),
                        pltpu.VMEM((nb, 1, LANE), F32), pltpu.VMEM((nb, 1, LANE), F32),
                        pltpu.VMEM((nb, 1, LANE), F32), pltpu.VMEM((nb, 1, lora_p), F32)]
                       + [pltpu.VMEM((tb, LANE), F32)] * 7,
        compiler_params=_cparams(("parallel", "parallel", "arbitrary")),
        name="rwkv7_chunked",
    )(pm3, pm3, pm3, plo3, sp_main, sp_main, sp_main, sp_lora,
      mu_main, mu_main, mu_main, mu_lora, row2(wp["w0"]), row2(wp["a0"]),
      wp["w2"].astype(BF16), wp["a2"].astype(BF16), g2p,
      row2(wp["k_k"]), row2(wp["k_a"]), row2(wp["r_k"]), row2(wp["ln_x_w"]), row2(wp["ln_x_b"]), z0)
    return o, jnp.swapaxes(zout, -1, -2)


ATT_SUB = DILATION_PAIRS[0][0] // DILATION_PAIRS[0][1]
assert all(w // d == ATT_SUB for w, d in DILATION_PAIRS)


def _rel_bucket(dist):
    max_exact = NUM_BUCKETS // 2
    d_f = jnp.maximum(dist, 1).astype(F32)
    large = max_exact + (jnp.log(d_f / max_exact) / math.log(REL_MAX_DIST / max_exact)
                         * (NUM_BUCKETS - max_exact)).astype(jnp.int32)
    large = jnp.minimum(large, NUM_BUCKETS - 1)
    return jnp.where(dist < max_exact, dist, large)


def _prompt_bias(rel_bias):
    sub = ATT_SUB
    qi = jnp.arange(sub)[:, None]
    ki = jnp.arange(2 * sub)[None, :]
    dsub = qi + sub - ki
    ok = ((dsub >= 0) & (dsub <= sub))[..., None]
    tabs = []
    for _, dil in DILATION_PAIRS:
        b = rel_bias[_rel_bucket(dil * jnp.clip(dsub, 0, sub))].astype(F32)
        tabs.append(jnp.transpose(jnp.where(ok, b, NEG_INF), (2, 0, 1)))
    return jnp.stack(tabs)


def _head_rms(x, g):
    return x * lax.rsqrt(jnp.mean(x * x, axis=-1, keepdims=True) + RMS_EPS) * g


def _attn_prompt_kernel(q_ref, k_ref, v_ref, bias_ref, qg_ref, kg_ref, o_ref, kn_ref,
                        qs, ks, acc, m_s, l_s, *, t):
    sub = ATT_SUB
    qs[...] = _head_rms(q_ref[...], qg_ref[...]) * (1.0 / math.sqrt(HEAD_DIM_A))
    kn = _head_rms(k_ref[...], kg_ref[...])
    ks[...] = kn
    kn_ref[...] = kn
    m_s[...] = jnp.full(m_s.shape, NEG_INF, F32)
    l_s[...] = jnp.zeros(l_s.shape, F32)
    acc[...] = jnp.zeros(acc.shape, F32)
    for g, (_, dil) in enumerate(DILATION_PAIRS):
        span = dil * sub
        bias_prev = bias_ref[g, :, 0:sub]
        bias_cur = bias_ref[g, :, sub:2 * sub]
        for r in range(dil):
            def body(n, _, dil=dil, span=span, r=r, bias_prev=bias_prev, bias_cur=bias_cur):
                start = r + span * n
                startp = r + span * jnp.maximum(n - 1, 0)
                if dil == 1:
                    rows, rowsp = pl.ds(start, sub), pl.ds(startp, sub)
                else:
                    rows, rowsp = pl.ds(start, sub, stride=dil), pl.ds(startp, sub, stride=dil)
                qb = qs[rows, :].astype(BF16)
                sc = _dot_nt(qb, ks[rows, :].astype(BF16), None) + bias_cur
                sp = jnp.where(n > 0, _dot_nt(qb, ks[rowsp, :].astype(BF16), None) + bias_prev, NEG_INF)
                m_old = m_s[rows, :]
                mb = jnp.maximum(jnp.max(sc, axis=-1, keepdims=True), jnp.max(sp, axis=-1, keepdims=True))
                m_new = jnp.maximum(m_old, mb)
                alpha = jnp.exp(m_old - m_new)
                pc = jnp.exp(sc - m_new)
                pp = jnp.exp(sp - m_new)
                l_s[rows, :] = (alpha * l_s[rows, :] + jnp.sum(pc, axis=-1, keepdims=True)
                                + jnp.sum(pp, axis=-1, keepdims=True))
                acc[rows, :] = (alpha * acc[rows, :]
                                + _dot(pc.astype(BF16), v_ref[rows, :].astype(BF16), None)
                                + _dot(pp.astype(BF16), v_ref[rowsp, :].astype(BF16), None))
                m_s[rows, :] = m_new
                return 0

            lax.fori_loop(0, t // span, body, 0)
    o_ref[...] = (acc[...] / l_s[...]).astype(o_ref.dtype)


def _attn_prompt(pm3, rel_bias, qg, kg, d_att):
    n, t, _ = pm3.shape
    nh = d_att // HEAD_DIM_A
    assert t % MAX_WINDOW == 0
    bias = _prompt_bias(rel_bias)
    blk = lambda off: pl.BlockSpec((None, t, HEAD_DIM_A), lambda i, h: (i, 0, off + h))
    vec = pl.BlockSpec((1, HEAD_DIM_A), lambda i, h: (0, 0))
    return pl.pallas_call(
        functools.partial(_attn_prompt_kernel, t=t),
        grid=(n, nh),
        in_specs=[blk(0), blk(nh), blk(2 * nh),
                  pl.BlockSpec((len(DILATION_PAIRS), None, ATT_SUB, 2 * ATT_SUB), lambda i, h: (0, h, 0, 0)),
                  vec, vec],
        out_specs=[blk(0), blk(0)],
        out_shape=[jax.ShapeDtypeStruct((n, t, d_att), BF16), jax.ShapeDtypeStruct((n, t, d_att), F32)],
        scratch_shapes=[pltpu.VMEM((t, HEAD_DIM_A), F32)] * 5,
        compiler_params=_cparams(("parallel", "parallel")),
        name="attn_prompt",
    )(pm3, pm3, pm3, bias, qg.reshape(1, -1), kg.reshape(1, -1))


ROWS_PAD = 8


def _sample_bias(rel_bias, dseq):
    sub = ATT_SUB
    s = jnp.arange(ROWS_PAD)[:, None]
    c = jnp.arange(sub)[None, :]
    live = s < dseq
    rb = lambda dist: rel_bias[_rel_bucket(dist)].astype(F32)
    tabs = []
    for _, dil in reversed(DILATION_PAIRS[1:]):
        tabs.append(jnp.where(live[..., None], rb(dil * (sub - c) + 0 * s), 0.0))
    j1 = sub + s - c
    tabs.append(jnp.where(((c >= s) & live)[..., None], rb(jnp.clip(j1, 0, sub)), jnp.where(live, NEG_INF, 0.0)[..., None]))
    for _, dil in DILATION_PAIRS:
        jn = s - c
        ok = (jn >= 0) & (c < dseq) & live & ((dil * jn <= 0) | (dil == 1))
        tabs.append(jnp.where(ok[..., None], rb(dil * jnp.clip(jn, 0, sub)), NEG_INF))
    return jnp.transpose(jnp.stack(tabs), (3, 0, 1, 2))


def _attn_sample_kernel(x_ref, k16_ref, k4_ref, k1_ref, v16_ref, v4_ref, v1_ref, bias_ref, qg_ref, kg_ref,
                        o_ref, kn_ref, *, nheads, dseq):
    dh = HEAD_DIM_A
    d_att = nheads * dh
    row = lax.broadcasted_iota(jnp.int32, (ROWS_PAD, 1), 0)
    zpad = jnp.zeros((ATT_SUB - ROWS_PAD, dh), F32)
    for h in range(nheads):
        cols = slice(h * dh, (h + 1) * dh)
        qn = _head_rms(x_ref[:, h * dh:(h + 1) * dh], qg_ref[...]) * (1.0 / math.sqrt(dh))
        kn = _head_rms(x_ref[:, d_att + h * dh:d_att + (h + 1) * dh], kg_ref[...])
        vn = x_ref[:, 2 * d_att + h * dh:2 * d_att + (h + 1) * dh]
        kn_ref[:, cols] = kn
        qb = qn.astype(BF16)
        knp = jnp.concatenate([kn, zpad], axis=0).astype(BF16)
        vnp = jnp.concatenate([vn, zpad], axis=0).astype(BF16)

        def grouped_logits(kref):
            out = jnp.zeros((ROWS_PAD, ATT_SUB), F32)
            for s in range(dseq):
                kt = kref[:, s * d_att + h * dh:s * d_att + (h + 1) * dh].astype(BF16)
                out = jnp.where(row == s, _dot_nt(qb, kt, None), out)
            return out

        def grouped_pv(p, vref):
            pb = p.astype(BF16)
            out = jnp.zeros((ROWS_PAD, dh), F32)
            for s in range(dseq):
                vt = vref[:, s * d_att + h * dh:s * d_att + (h + 1) * dh].astype(BF16)
                out = jnp.where(row == s, _dot(pb, vt, None), out)
            return out

        l_new = _dot_nt(qb, knp, None)
        logits = [grouped_logits(k16_ref) + bias_ref[h, 0], grouped_logits(k4_ref) + bias_ref[h, 1],
                  _dot_nt(qb, k1_ref[:, cols].astype(BF16), None) + bias_ref[h, 2],
                  l_new + bias_ref[h, 3], l_new + bias_ref[h, 4], l_new + bias_ref[h, 5]]
        m = functools.reduce(jnp.maximum, [jnp.max(x, axis=-1, keepdims=True) for x in logits])
        p = [jnp.exp(x - m) for x in logits]
        denom = functools.reduce(jnp.add, [jnp.sum(x, axis=-1, keepdims=True) for x in p])
        o = (grouped_pv(p[0], v16_ref) + grouped_pv(p[1], v4_ref)
             + _dot(p[2].astype(BF16), v1_ref[:, cols].astype(BF16), None)
             + _dot((p[3] + p[4] + p[5]).astype(BF16), vnp, None))
        o_ref[:, cols] = o / denom


def _attn_sample(x3, cache_k, cache_v, rel_bias, qg, kg, d_att, dseq):
    n, wb = cache_k.shape[0], cache_k.shape[1]
    nh = d_att // HEAD_DIM_A
    d4, d16 = DILATION_PAIRS[1][1], DILATION_PAIRS[2][1]
    assert wb == MAX_WINDOW == d16 * ATT_SUB and dseq <= d4 and DILATION_PAIRS[0][1] == 1
    bias = _sample_bias(rel_bias, dseq)
    view = lambda c, d: c.reshape(n, wb // d, d * d_att)
    spec = lambda d, blk: pl.BlockSpec((None, ATT_SUB, min(d, dseq) * d_att), lambda i: (i, blk, 0))
    s16, s4, s1 = spec(d16, 0), spec(d4, wb // d4 // ATT_SUB - 1), spec(1, wb // ATT_SUB - 1)
    row_spec = lambda w: pl.BlockSpec((None, ROWS_PAD, w), lambda i: (i, 0, 0))
    vec = pl.BlockSpec((1, HEAD_DIM_A), lambda i: (0, 0))
    return pl.pallas_call(
        functools.partial(_attn_sample_kernel, nheads=nh, dseq=dseq),
        grid=(n,),
        in_specs=[row_spec(3 * d_att), s16, s4, s1, s16, s4, s1,
                  pl.BlockSpec(bias.shape, lambda i: (0, 0, 0, 0)), vec, vec],
        out_specs=[row_spec(d_att), row_spec(d_att)],
        out_shape=[jax.ShapeDtypeStruct((n, ROWS_PAD, d_att), F32)] * 2,
        compiler_params=_cparams(("parallel",)),
        name="attn_sample",
    )(x3, view(cache_k, d16), view(cache_k, d4), view(cache_k, 1), view(cache_v, d16), view(cache_v, d4),
      view(cache_v, 1), bias, qg.reshape(1, -1), kg.reshape(1, -1))


def _rms_jnp(x, g):
    return x * lax.rsqrt(jnp.mean(x * x, axis=-1, keepdims=True) + RMS_EPS) * g


def _rel_bucket(dist):
    max_exact = NUM_BUCKETS // 2
    d_f = jnp.maximum(dist, 1).astype(F32)
    large = max_exact + (jnp.log(d_f / max_exact) / math.log(REL_MAX_DIST / max_exact)
                         * (NUM_BUCKETS - max_exact)).astype(jnp.int32)
    large = jnp.minimum(large, NUM_BUCKETS - 1)
    return jnp.where(dist < max_exact, dist, large)


def _attn_jnp(q, k, v, rel_bias, cache_k, cache_v):
    n, t, h, dh = q.shape
    num = 0.0
    outs, lses = [], []
    for window, dil in DILATION_PAIRS:
        sub = window // dil
        if cache_k is None:
            nb = -(-t // (dil * sub))
            tp = nb * sub * dil
            pad = lambda a: jnp.pad(a, ((0, 0), (0, tp - t), (0, 0), (0, 0))).reshape(n, nb, sub, dil, h, dh)
            qb, kb, vb = pad(q), pad(k), pad(v)
            wp = lambda a: jnp.concatenate(
                [jnp.pad(a[:, :-1], ((0, 0), (1, 0), (0, 0), (0, 0), (0, 0), (0, 0))), a], axis=2)
            kk, vv = wp(kb), wp(vb)
            logits = jnp.einsum('bnqrhd,bnkrhd->bnrhqk', qb, kk) / math.sqrt(dh)
            qi = jnp.arange(sub)[:, None]
            ki = jnp.arange(2 * sub)[None, :]
            dsub = qi + sub - ki
            bias = rel_bias[_rel_bucket(dil * jnp.clip(dsub, 0, sub))]
            key_idx = jnp.arange(nb)[:, None] * sub + ki - sub
            mask = ((dsub >= 0) & (dsub <= sub))[None] & (key_idx >= 0)[:, None, :]
            logits = jnp.where(mask[None, :, None, None], logits + jnp.transpose(bias, (2, 0, 1)), NEG_INF)
            m = jnp.max(logits, axis=-1, keepdims=True)
            e = jnp.exp(logits - m)
            s = jnp.sum(e, axis=-1)
            o = jnp.einsum('bnrhqk,bnkrhd->bnqrhd', e, vv)
            o = o / jnp.transpose(s, (0, 1, 4, 2, 3))[..., None]
            lse = jnp.transpose(m[..., 0] + jnp.log(s), (0, 1, 4, 2, 3))
            o, lse = o.reshape(n, tp, h, dh)[:, :t], lse.reshape(n, tp, h)[:, :t]
        else:
            wb = cache_k.shape[1]
            j = jnp.arange(sub + 1)
            idx = wb + jnp.arange(t)[:, None] - dil * j[None, :]
            valid = idx >= 0
            from_new = (idx >= wb)[None, :, :, None, None]
            ic = jnp.clip(idx, 0, wb - 1)
            inew = jnp.clip(idx - wb, 0, t - 1)
            kg = jnp.where(from_new, k[:, inew], cache_k[:, ic])
            vg = jnp.where(from_new, v[:, inew], cache_v[:, ic])
            logits = jnp.einsum('nshd,nsjhd->nhsj', q, kg) / math.sqrt(dh)
            bias = rel_bias[_rel_bucket(dil * j)].T
            logits = jnp.where(valid[None, None], logits + bias[None, :, None, :], NEG_INF)
            m = jnp.max(logits, axis=-1, keepdims=True)
            e = jnp.exp(logits - m)
            s = jnp.sum(e, axis=-1)
            o = jnp.einsum('nhsj,nsjhd->nshd', e, vg)
            o = o / jnp.transpose(s, (0, 2, 1))[..., None]
            lse = jnp.transpose(m[..., 0] + jnp.log(s), (0, 2, 1))
        outs.append(o)
        lses.append(lse)
    wts = jax.nn.softmax(jnp.stack(lses), axis=0)
    o = jnp.sum(wts[..., None] * jnp.stack(outs), axis=0)
    return o.reshape(n, t, h * dh)


def _rwkv_jnp(pb, shift_prev, wkv_prev, mu, w0, w2, a0, a2, g2, k_k, k_a, r_k, ln_w, ln_b):
    n, t, _ = pb.shape
    d_rwkv = w0.shape[0]
    nh = d_rwkv // HEAD_DIM_B
    prev = jnp.concatenate([shift_prev[:, None], pb[:, :-1]], axis=1)
    xs = pb + mu * (prev - pb)
    c1, c2, c3 = d_rwkv, 2 * d_rwkv, 3 * d_rwkv
    c4, c5 = c3 + DECAY_LORA, c3 + DECAY_LORA + AAA_LORA
    r, k, v = xs[..., :c1], xs[..., c1:c2], xs[..., c2:c3]
    wd, ad, gd = xs[..., c3:c4], xs[..., c4:c5], xs[..., c5:]
    w_log = -jax.nn.softplus(-(w0 + jnp.tanh(wd) @ w2)) - 0.5
    decay = jnp.exp(-jnp.exp(w_log))
    a = jax.nn.sigmoid(a0 + ad @ a2)
    g = jax.nn.sigmoid(gd) @ g2
    hd = lambda z: z.reshape(n, t, nh, HEAD_DIM_B)
    r_h, k_h, v_h, a_h, w_h = hd(r), hd(k), hd(v), hd(a), hd(decay)
    kk = k_h * k_k.reshape(nh, HEAD_DIM_B)
    kk = kk / jnp.maximum(jnp.linalg.norm(kk, axis=-1, keepdims=True), 1e-12)
    k_h = k_h * (1.0 + (a_h - 1.0) * k_a.reshape(nh, HEAD_DIM_B))

    def step(state, inp):
        r_t, k_t, v_t, kk_t, a_t, w_t = inp
        sa = jnp.einsum('nhvk,nhk->nhv', state, -kk_t, precision='highest')
        state = (state * w_t[:, :, None, :] + sa[..., None] * (kk_t * a_t)[:, :, None, :]
                 + v_t[..., None] * k_t[:, :, None, :])
        return state, jnp.einsum('nhvk,nhk->nhv', state, r_t, precision='highest')

    sf = lambda z: jnp.swapaxes(z, 0, 1)
    wkv_new, ys = lax.scan(step, wkv_prev, (sf(r_h), sf(k_h), sf(v_h), sf(kk), sf(a_h), sf(w_h)))
    y = sf(ys)
    mean = jnp.mean(y, axis=-1, keepdims=True)
    var = jnp.mean(jnp.square(y - mean), axis=-1, keepdims=True)
    y = ((y - mean) * lax.rsqrt(var + GN_EPS)).reshape(n, t, d_rwkv) * ln_w + ln_b
    bonus = jnp.sum(r_h * k_h * r_k, axis=-1, keepdims=True) * v_h
    y = (y + bonus.reshape(n, t, d_rwkv)) * g
    return y, wkv_new


def _prep_weights(lp):
    (g_mix, w_in, q_norm_g, k_norm_g, mu_shift, w0, w2, a0, a2, g2, k_k, k_a, r_k, ln_x_w, ln_x_b, w_out,
     g_ffn, w_gate, w_up, conv_w, conv_b, w_down, g_ple, w_ple, w_ple_gate) = lp
    d_model = w_in.shape[0]
    d_rwkv = w0.shape[0]
    d_att = w_out.shape[0] - d_rwkv
    main = 3 * d_att + 3 * d_rwkv
    lora = w_in.shape[1] - main
    lora_p = _round_up(lora, LANE)
    d_ff = w_gate.shape[1]
    dffp = _round_up(d_ff, FF_ALIGN)
    padc = lambda w, n: jnp.pad(w, ((0, 0), (0, n - w.shape[1])))
    return dict(
        d_model=d_model, d_att=d_att, d_rwkv=d_rwkv, lora=lora, lora_p=lora_p, d_ff=d_ff, dffp=dffp,
        g_mix=g_mix, g_ffn=g_ffn, g_ple=g_ple, q_norm_g=q_norm_g, k_norm_g=k_norm_g,
        w_in_main=w_in[:, :main].astype(BF16),
        w_in_lora=padc(w_in[:, main:], lora_p).astype(BF16),
        mu_shift=mu_shift, w0=w0, w2=w2, a0=a0, a2=a2, g2=g2, k_k=k_k, k_a=k_a, r_k=r_k,
        ln_x_w=ln_x_w, ln_x_b=ln_x_b,
        w_out=w_out.astype(BF16),
        w_gate=padc(w_gate, dffp).astype(BF16), w_up=padc(w_up, dffp).astype(BF16),
        conv_w=padc(conv_w, dffp), conv_b=jnp.pad(conv_b, (0, dffp - d_ff)),
        w_down=jnp.pad(w_down, ((0, dffp - d_ff), (0, 0))).astype(BF16),
        w_ple=w_ple.astype(BF16), w_ple_gate=w_ple_gate.astype(BF16),
    )


def _layer(x2d, p2d, wp, mixer, *, tm, shift, conv_prev, tiles_per_group):
    d_model, dffp = wp["d_model"], wp["dffp"]
    main = wp["w_in_main"].shape[1]
    tn = 512
    xn = _rmsnorm(x2d, wp["g_mix"], min(tm, 256))
    proj_main = _matmul(xn, wp["w_in_main"], tm=tm, tn=tn, tk=d_model, name="in_proj_main")
    proj_lora = _matmul(xn, wp["w_in_lora"], tm=tm, tn=wp["lora_p"], tk=d_model, name="in_proj_lora")
    mix, aux = mixer(proj_main, proj_lora)
    h1 = _matmul(mix, wp["w_out"], tm=tm, tn=tn, tk=mix.shape[1], mode="residual", extras=(x2d,), name="out_proj")
    hn = _rmsnorm(h1, wp["g_ffn"], min(tm, 256))
    hist = (CONV_WIDTH - 1) * shift
    cprev = jnp.pad(conv_prev, ((0, 0), (0, 0), (0, dffp - conv_prev.shape[-1])))
    act, conv_tail = _ffn_act(hn, wp["w_gate"], wp["w_up"], wp["conv_w"], wp["conv_b"], cprev,
                              tm=tm, tf=FF_ALIGN, shift=shift, tiles_per_group=tiles_per_group)
    tk_down = 1024 if dffp % 1024 == 0 else FF_ALIGN
    h2 = _matmul(act, wp["w_down"], tm=tm, tn=min(1024, d_model), tk=tk_down, mode="residual", extras=(h1,),
                 name="ffn_down")
    hn2 = _rmsnorm(h2, wp["g_ple"], min(tm, 256))
    y = _matmul(hn2, wp["w_ple_gate"], tm=tm, tn=tn, tk=d_model, mode="ple",
                extras=(h2, p2d.astype(BF16), wp["w_ple"]), name="ple_gate")
    conv_tail = conv_tail[tiles_per_group - 1::tiles_per_group, :, :wp["d_ff"]]
    return y, proj_main, proj_lora, conv_tail, aux


def kernel(x_prompt, x_sample, cache_k, cache_v, state_shift, state_wkv, state_conv, p_prompt, p_sample, rel_bias, g_mix, w_in, q_norm_g, k_norm_g, mu_shift, w0, w2, a0, a2, g2, k_k, k_a, r_k, ln_x_w, ln_x_b, w_out, g_ffn, w_gate, w_up, conv_w, conv_b, w_down, g_ple, w_ple, w_ple_gate):
    depth = g_mix.shape[0]
    nbp, seq, d_model = x_prompt.shape
    nbs, dseq, _ = x_sample.shape
    hp = x_prompt.reshape(nbp * seq, d_model)
    hs = jnp.swapaxes(x_sample, 0, 1).reshape(dseq * nbs, d_model)
    outs_p, outs_s = [], []
    for i in range(depth):
        lp = (g_mix[i], w_in[i], q_norm_g[i], k_norm_g[i], mu_shift[i], w0[i], w2[i], a0[i], a2[i], g2[i],
              k_k[i], k_a[i], r_k[i], ln_x_w[i], ln_x_b[i], w_out[i], g_ffn[i], w_gate[i], w_up[i],
              conv_w[i], conv_b[i], w_down[i], g_ple[i], w_ple[i], w_ple_gate[i])
        wp = _prep_weights(lp)
        d_att, d_rwkv, lora, d_ff = wp["d_att"], wp["d_rwkv"], wp["lora"], wp["d_ff"]
        nha = d_att // HEAD_DIM_A
        nhb = d_rwkv // HEAD_DIM_B

        def mixer(proj_main, proj_lora, n, t, time_major, ck, cv, shift_prev, wkv_prev):
            if time_major:
                to_nm = lambda z: jnp.swapaxes(z.reshape(t, n, -1), 0, 1)
            else:
                to_nm = lambda z: z.reshape(n, t, -1)
            pm, plo = to_nm(proj_main), to_nm(proj_lora)
            heads = lambda z: z.reshape(n, t, nha, HEAD_DIM_A)
            v = heads(pm[..., 2 * d_att:3 * d_att])
            if ck is None:
                o_att, kn = _attn_prompt(pm, rel_bias, q_norm_g[i], k_norm_g[i], d_att)
            else:
                x3 = jnp.pad(pm[..., :3 * d_att], ((0, 0), (0, ROWS_PAD - t), (0, 0)))
                o_att, kn = _attn_sample(x3, ck, cv, rel_bias, q_norm_g[i], k_norm_g[i], d_att, t)
                o_att, kn = o_att[:, :t], kn[:, :t]
            k = heads(kn)
            shift_new = jnp.concatenate([pm[:, -1, 3 * d_att:], plo[:, -1, :lora]], axis=-1)
            tpad = _round_up(t, RWKV_CHUNK)
            padt = lambda z: jnp.pad(z, ((0, 0), (0, tpad - t), (0, 0)))
            o_rwkv, wkv_new = _rwkv_mixer(padt(pm), padt(plo), shift_prev, wkv_prev, wp,
                                          nb=min(n, 16), tb=min(tpad, 256), t_valid=t)
            mixv = jnp.concatenate([o_att.astype(BF16), o_rwkv[:, :t]], axis=-1)
            if time_major:
                mixv = jnp.swapaxes(mixv, 0, 1)
            mixv = mixv.reshape(n * t, -1)
            return mixv, (k, v, shift_new, wkv_new)

        tm_p = min(1024, seq)
        zero_shift = jnp.zeros((nbp, 3 * d_rwkv + lora), F32)
        zero_wkv = jnp.zeros((nbp, nhb, HEAD_DIM_B, HEAD_DIM_B), F32)
        zero_conv = jnp.zeros((nbp, CONV_WIDTH - 1, d_ff), F32)
        mix_p = functools.partial(mixer, n=nbp, t=seq, time_major=False, ck=None, cv=None,
                                  shift_prev=zero_shift, wkv_prev=zero_wkv)
        hp, _, _, conv_p, (k_p, v_p, shift_p, wkv_p) = _layer(
            hp, p_prompt[i].reshape(nbp * seq, -1), wp, mix_p, tm=tm_p, shift=1, conv_prev=zero_conv,
            tiles_per_group=seq // tm_p)
        keep = min(MAX_WINDOW, seq)
        outs_p.append((k_p[:, seq - keep:], v_p[:, seq - keep:], shift_p, wkv_p, conv_p))

        mix_s = functools.partial(mixer, n=nbs, t=dseq, time_major=True, ck=cache_k[i], cv=cache_v[i],
                                  shift_prev=state_shift[i], wkv_prev=state_wkv[i])
        conv_prev_s = jnp.swapaxes(state_conv[i], 0, 1).reshape(1, (CONV_WIDTH - 1) * nbs, d_ff)
        p_s = jnp.swapaxes(p_sample[i], 0, 1).reshape(dseq * nbs, -1)
        hs, _, _, conv_s, (k_s, v_s, shift_s, wkv_s) = _layer(
            hs, p_s, wp, mix_s, tm=dseq * nbs, shift=nbs, conv_prev=conv_prev_s, tiles_per_group=1)
        conv_s = jnp.swapaxes(conv_s.reshape(CONV_WIDTH - 1, nbs, d_ff), 0, 1)
        outs_s.append((k_s, v_s, shift_s, wkv_s, conv_s))

    y_p = hp.reshape(nbp, seq, d_model)
    y_s = jnp.swapaxes(hs.reshape(dseq, nbs, d_model), 0, 1)
    st = lambda outs, idx: jnp.stack([o[idx] for o in outs])
    return (y_p, y_s, st(outs_p, 0), st(outs_p, 1), st(outs_p, 2), st(outs_p, 3), st(outs_p, 4),
            st(outs_s, 0), st(outs_s, 1), st(outs_s, 2), st(outs_s, 3), st(outs_s, 4))
```

```python
import functools
import math

import jax
import jax.numpy as jnp
from jax import lax
from jax.experimental import pallas as pl
from jax.experimental.pallas import tpu as pltpu

F32 = jnp.float32
BF16 = jnp.bfloat16

HEAD_DIM_A = 128
HEAD_DIM_B = 64
DILATION_PAIRS = ((128, 1), (512, 4), (2048, 16))
MAX_WINDOW = max(w for w, _ in DILATION_PAIRS)
NUM_BUCKETS = 32
REL_MAX_DIST = MAX_WINDOW
DECAY_LORA = 128
AAA_LORA = 128
CONV_WIDTH = 3
RMS_EPS = 1e-6
GN_EPS = 64e-5
NEG_INF = -1e30

LANE = 128
FF_ALIGN = 512
VMEM_LIMIT = 56 * 1024 * 1024
_HI = lax.Precision.HIGHEST


def _cparams(sem):
    return pltpu.CompilerParams(dimension_semantics=sem, vmem_limit_bytes=VMEM_LIMIT)


def _round_up(x, m):
    return -(-x // m) * m


def _dot(a, b, prec=None):
    return jnp.dot(a, b, preferred_element_type=F32, precision=prec)


def _dot_nt(a, b, prec=None):
    return lax.dot_general(a, b, (((1,), (1,)), ((), ())), preferred_element_type=F32, precision=prec)


def _bdot(a, b):
    return _dot(a.astype(BF16), b.astype(BF16))


def _rms_kernel(x_ref, g_ref, o_ref):
    x = x_ref[...]
    ms = jnp.mean(x * x, axis=-1, keepdims=True)
    o_ref[...] = (x * lax.rsqrt(ms + RMS_EPS) * g_ref[...]).astype(o_ref.dtype)


def _rmsnorm(x, g, tm):
    m, d = x.shape
    return pl.pallas_call(
        _rms_kernel,
        grid=(m // tm,),
        in_specs=[pl.BlockSpec((tm, d), lambda i: (i, 0)), pl.BlockSpec((1, d), lambda i: (0, 0))],
        out_specs=pl.BlockSpec((tm, d), lambda i: (i, 0)),
        out_shape=jax.ShapeDtypeStruct((m, d), BF16),
        compiler_params=_cparams(("parallel",)),
        name="rmsnorm",
    )(x, g.reshape(1, d))


def _mm_kernel(*refs, nk, mode):
    a_ref, b_ref = refs[0], refs[1]
    o_ref = refs[-2] if nk > 1 else refs[-1]
    acc_ref = refs[-1] if nk > 1 else None

    def epilogue(acc):
        if mode == "plain":
            return acc
        if mode == "residual":
            return refs[2][...] + acc
        h_ref, p_ref, wp_ref = refs[2], refs[3], refs[4]
        ple = _dot(p_ref[...], wp_ref[...])
        return h_ref[...] + ple * jax.nn.sigmoid(acc)

    part = _dot(a_ref[...], b_ref[...])
    if nk == 1:
        o_ref[...] = epilogue(part).astype(o_ref.dtype)
        return
    k = pl.program_id(2)

    @pl.when(k == 0)
    def _():
        acc_ref[...] = part

    @pl.when(k > 0)
    def _():
        acc_ref[...] += part

    @pl.when(k == nk - 1)
    def _():
        o_ref[...] = epilogue(acc_ref[...]).astype(o_ref.dtype)


def _matmul(a, b, *, tm, tn, tk, mode="plain", extras=(), out_dtype=F32, name="matmul"):
    m, kdim = a.shape
    n = b.shape[1]
    nk = kdim // tk
    in_specs = [pl.BlockSpec((tm, tk), lambda i, j, k: (i, k)), pl.BlockSpec((tk, tn), lambda i, j, k: (k, j))]
    if mode == "residual":
        in_specs.append(pl.BlockSpec((tm, tn), lambda i, j, k: (i, j)))
    elif mode == "ple":
        pdim = extras[1].shape[1]
        in_specs += [pl.BlockSpec((tm, tn), lambda i, j, k: (i, j)),
                     pl.BlockSpec((tm, pdim), lambda i, j, k: (i, 0)),
                     pl.BlockSpec((pdim, tn), lambda i, j, k: (0, j))]
    scratch = [pltpu.VMEM((tm, tn), F32)] if nk > 1 else []
    return pl.pallas_call(
        functools.partial(_mm_kernel, nk=nk, mode=mode),
        grid=(m // tm, n // tn, nk),
        in_specs=in_specs,
        out_specs=pl.BlockSpec((tm, tn), lambda i, j, k: (i, j)),
        out_shape=jax.ShapeDtypeStruct((m, n), out_dtype),
        scratch_shapes=scratch,
        compiler_params=_cparams(("parallel", "parallel", "arbitrary")),
        name=name,
    )(a, b, *extras)


def _ffn_kernel(a_ref, wg_ref, wu_ref, cw_ref, cb_ref, prev_ref, act_ref, tail_ref, ext_ref, carry_ref,
                *, tm, shift, tiles_per_group):
    i = pl.program_id(0)
    j = pl.program_id(1)
    hist = (CONV_WIDTH - 1) * shift
    pad = _round_up(hist, 8)
    a = a_ref[...]
    u = _dot(a, wg_ref[...])
    up = _dot(a, wu_ref[...])

    @pl.when(i % tiles_per_group == 0)
    def _():
        ext_ref[pl.ds(pad - hist, hist), :] = prev_ref[0]

    @pl.when(i % tiles_per_group != 0)
    def _():
        ext_ref[pl.ds(pad - hist, hist), :] = carry_ref[j]

    ext_ref[pl.ds(pad, tm), :] = u
    cw = cw_ref[...]
    c = (cb_ref[...] + cw[0:1] * ext_ref[pl.ds(pad - 2 * shift, tm), :]
         + cw[1:2] * ext_ref[pl.ds(pad - shift, tm), :] + cw[2:3] * u)
    act_ref[...] = (jax.nn.silu(c) * up).astype(act_ref.dtype)
    tail = ext_ref[pl.ds(pad + tm - hist, hist), :]
    carry_ref[j] = tail
    tail_ref[0] = tail


def _ffn_act(xn, wg, wu, conv_w, conv_b, conv_prev, *, tm, tf, shift, tiles_per_group):
    m, d = xn.shape
    dffp = wg.shape[1]
    hist = (CONV_WIDTH - 1) * shift
    nj = dffp // tf
    kern = functools.partial(_ffn_kernel, tm=tm, shift=shift, tiles_per_group=tiles_per_group)
    return pl.pallas_call(
        kern,
        grid=(m // tm, nj),
        in_specs=[
            pl.BlockSpec((tm, d), lambda i, j: (i, 0)),
            pl.BlockSpec((d, tf), lambda i, j: (0, j)),
            pl.BlockSpec((d, tf), lambda i, j: (0, j)),
            pl.BlockSpec((CONV_WIDTH, tf), lambda i, j: (0, j)),
            pl.BlockSpec((1, tf), lambda i, j: (0, j)),
            pl.BlockSpec((1, hist, tf), lambda i, j: (i // tiles_per_group, 0, j)),
        ],
        out_specs=[
            pl.BlockSpec((tm, tf), lambda i, j: (i, j)),
            pl.BlockSpec((1, hist, tf), lambda i, j: (i, 0, j)),
        ],
        out_shape=[jax.ShapeDtypeStruct((m, dffp), BF16), jax.ShapeDtypeStruct((m // tm, hist, dffp), F32)],
        scratch_shapes=[pltpu.VMEM((_round_up(hist, 8) + tm, tf), F32), pltpu.VMEM((nj, hist, tf), F32)],
        compiler_params=_cparams(("arbitrary", "arbitrary")),
        name="ffn_gate_up_conv",
    )(xn, wg, wu, conv_w, conv_b.reshape(1, dffp), conv_prev)


RWKV_CHUNK = 64


def _softplus(z):
    return jnp.maximum(z, 0.0) + jnp.log(1.0 + jnp.exp(-jnp.abs(z)))


def _pair_blockdiag():
    li = lax.broadcasted_iota(jnp.int32, (LANE, LANE), 0)
    lj = lax.broadcasted_iota(jnp.int32, (LANE, LANE), 1)
    return (li // HEAD_DIM_B == lj // HEAD_DIM_B).astype(F32)


def _rwkv_pre(xk, xl, prm, blockdiag):
    w0_ref, a0_ref, w2_ref, a2_ref, g2_ref, kk_ref, ka_ref = prm[:7]
    wd = xl[:, 0:DECAY_LORA]
    ad = xl[:, DECAY_LORA:DECAY_LORA + AAA_LORA]
    gd = xl[:, DECAY_LORA + AAA_LORA:]
    wl = w0_ref[...] + _dot(jnp.tanh(wd).astype(BF16), w2_ref[...])
    lw = -jnp.exp(-_softplus(-wl) - 0.5)
    a = jax.nn.sigmoid(a0_ref[...] + _dot(ad.astype(BF16), a2_ref[...]))
    g = _dot(jax.nn.sigmoid(gd).astype(BF16), g2_ref[...])
    kk = xk * kk_ref[...]
    kk = kk / jnp.maximum(jnp.sqrt(_dot(kk * kk, blockdiag, _HI)), 1e-12)
    kmod = xk * (1.0 + (a - 1.0) * ka_ref[...])
    return lw, -kk, kk * a, kmod, g


def _rwkv_post(y, xr, kmod, xv, g, prm, blockdiag):
    rk_ref, lnw_ref, lnb_ref = prm[7:10]
    inv = 1.0 / HEAD_DIM_B
    mean = _dot(y, blockdiag, _HI) * inv
    d = y - mean
    var = _dot(d * d, blockdiag, _HI) * inv
    yn = d * lax.rsqrt(var + GN_EPS) * lnw_ref[...] + lnb_ref[...]
    bonus = _dot(xr * kmod * rk_ref[...], blockdiag, _HI) * xv
    return (yn + bonus) * g


def _rwkv_chunk_kernel(pr_ref, pk_ref, pv_ref, plo_ref, spr_ref, spk_ref, spv_ref, splo_ref,
                       mur_ref, muk_ref, muv_ref, mulo_ref, *rest, tb, nb):
    prm = rest[:10]
    z0_ref, o_ref, zout_ref, z_s, cr_s, ck_s, cv_s, clo_s = rest[10:]
    ti = pl.program_id(2)
    nt = pl.num_programs(2)
    cs = RWKV_CHUNK
    hd = HEAD_DIM_B

    lane = lax.broadcasted_iota(jnp.int32, (1, LANE), 1)
    m0 = (lane < hd).astype(F32)
    m1 = 1.0 - m0
    blockdiag = _pair_blockdiag()
    li = lax.broadcasted_iota(jnp.int32, (LANE, LANE), 0)
    lj = lax.broadcasted_iota(jnp.int32, (LANE, LANE), 1)
    eye = (li == lj).astype(F32)
    same = li // cs == lj // cs
    mask_incl = (same & (li >= lj)).astype(F32)
    mask_strict = (same & (li > lj)).astype(F32)
    ci = lax.broadcasted_iota(jnp.int32, (cs, cs), 0)
    cj = lax.broadcasted_iota(jnp.int32, (cs, cs), 1)
    tril_incl = (ci >= cj).astype(F32)
    row = lax.broadcasted_iota(jnp.int32, (tb, 1), 0)

    @pl.when(ti == 0)
    def _():
        z_s[...] = jnp.zeros(z_s.shape, F32)
        for q in range(nb):
            z_s[q, 0:hd, 0:hd] = z0_ref[q, 0]
            z_s[q, hd:2 * hd, hd:2 * hd] = z0_ref[q, 1]
        cr_s[...] = spr_ref[...]
        ck_s[...] = spk_ref[...]
        cv_s[...] = spv_ref[...]
        clo_s[...] = splo_ref[...]

    def shifted(x, carry_row, mu):
        prev = jnp.where(row == 0, carry_row, pltpu.roll(x, 1, 0))
        return x + mu * (prev - x)

    two = lambda x: jnp.concatenate([x * m0, x * m1], axis=0)
    fold = lambda x: x[0:cs] + x[cs:2 * cs]

    seqs = []
    for q in range(nb):
        pr, pk, pv, plo = pr_ref[q], pk_ref[q], pv_ref[q], plo_ref[q]
        xr = shifted(pr, cr_s[q], mur_ref[...])
        xk = shifted(pk, ck_s[q], muk_ref[...])
        xv = shifted(pv, cv_s[q], muv_ref[...])
        xl = shifted(plo, clo_s[q], mulo_ref[...])
        cr_s[q] = pr[tb - 1:tb]
        ck_s[q] = pk[tb - 1:tb]
        cv_s[q] = pv[tb - 1:tb]
        clo_s[q] = plo[tb - 1:tb]
        lw, aneg, bb, kmod, g = _rwkv_pre(xk, xl, prm, blockdiag)
        seqs.append(dict(xr=xr, xv=xv, lw=lw, aneg=aneg, bb=bb, kmod=kmod, g=g))

    nchunk = tb // cs
    units = []
    for q in range(nb):
        for c in range(nchunk):
            rows = slice(c * cs, (c + 1) * cs)
            units.append({k: v[rows] for k, v in seqs[q].items() if k != "g"})
    for u in units:
        u["cum"] = _dot(tril_incl, u["lw"], _HI)
    for u in units:
        cum = u["cum"]
        tot = cum[cs - 1:cs]
        e_inv = jnp.exp(-cum)
        e_end = jnp.exp(tot - cum)
        u["rt"] = u["xr"] * jnp.exp(cum)
        u["la"] = two(u["aneg"] * jnp.exp(cum - u["lw"]))
        u["mt"] = jnp.concatenate([u["bb"] * e_end, u["kmod"] * e_end], axis=0).T
        u["pc_col"] = jnp.broadcast_to(jnp.exp(tot), (LANE, LANE)).T
        u["gm"] = _dot_nt(jnp.concatenate([u["la"], two(u["rt"])], axis=0).astype(BF16),
                          jnp.concatenate([two(u["bb"] * e_inv), two(u["kmod"] * e_inv)], axis=0).astype(BF16))
    for u in units:
        gm = u.pop("gm")
        u["n_pow"] = gm[0:LANE, 0:LANE] * mask_strict
        u["tinv"] = eye + u["n_pow"]
        u["arb"] = gm[LANE:2 * LANE, 0:LANE] * mask_incl
        u["av"] = _bdot(jnp.concatenate([gm[0:LANE, LANE:2 * LANE] * mask_strict,
                                         gm[LANE:2 * LANE, LANE:2 * LANE] * mask_incl], axis=0), two(u["xv"]))
    for _ in range(int(math.log2(cs)) - 1):
        for u in units:
            u["n_pow"] = _bdot(u["n_pow"], u["n_pow"])
        for u in units:
            u["tinv"] = u["tinv"] + _bdot(u["tinv"], u["n_pow"])
    for u in units:
        wu = _bdot(u["tinv"], jnp.concatenate([u["la"], u["av"][0:LANE]], axis=1))
        u["w"] = fold(wu[:, 0:LANE])
        u["u0"] = fold(wu[:, LANE:2 * LANE])
        u["y0"] = fold(u["av"][LANE:2 * LANE])
    for u in units:
        vc = u["xv"]
        zz = _bdot(u["mt"], jnp.concatenate([jnp.concatenate([u["w"], u["u0"]], axis=1),
                                             jnp.concatenate([jnp.zeros_like(vc), vc], axis=1)], axis=0))
        u["zm"] = zz[:, 0:LANE] * blockdiag
        u["zc"] = zz[:, LANE:2 * LANE] * blockdiag

    zs = [z_s[q] for q in range(nb)]
    ys = [[] for _ in range(nb)]
    for c in range(nchunk):
        yus = [_bdot(jnp.concatenate([units[q * nchunk + c]["rt"], units[q * nchunk + c]["w"]], axis=0), zs[q])
               for q in range(nb)]
        zms = [_bdot(units[q * nchunk + c]["zm"], zs[q]) for q in range(nb)]
        for q in range(nb):
            u = units[q * nchunk + c]
            uu = yus[q][cs:2 * cs] + u["u0"]
            ys[q].append(yus[q][0:cs] + u["y0"] + fold(_bdot(u["arb"], two(uu))))
            zs[q] = u["pc_col"] * zs[q] + zms[q] + u["zc"]
    for q in range(nb):
        z_s[q] = zs[q]
        s = seqs[q]
        y = jnp.concatenate(ys[q], axis=0)
        o_ref[q] = _rwkv_post(y, s["xr"], s["kmod"], s["xv"], s["g"], prm, blockdiag).astype(o_ref.dtype)

    @pl.when(ti == nt - 1)
    def _():
        for q in range(nb):
            zout_ref[q, 0] = z_s[q, 0:hd, 0:hd]
            zout_ref[q, 1] = z_s[q, hd:2 * hd, hd:2 * hd]


def _rwkv_param_specs(wp, imap):
    d_rwkv, lora, lora_p = wp["d_rwkv"], wp["lora"], wp["lora_p"]
    gpad = lora_p - DECAY_LORA - AAA_LORA
    glora = lora - DECAY_LORA - AAA_LORA
    row2 = lambda v: v.reshape(1, d_rwkv)
    g2p = jnp.pad(wp["g2"], ((0, gpad - glora), (0, 0))).astype(BF16)
    vec = pl.BlockSpec((1, LANE), imap)
    ops = [row2(wp["w0"]), row2(wp["a0"]), wp["w2"].astype(BF16), wp["a2"].astype(BF16), g2p,
           row2(wp["k_k"]), row2(wp["k_a"]), row2(wp["r_k"]), row2(wp["ln_x_w"]), row2(wp["ln_x_b"])]
    specs = [vec, vec, pl.BlockSpec((DECAY_LORA, LANE), imap), pl.BlockSpec((AAA_LORA, LANE), imap),
             pl.BlockSpec((gpad, LANE), imap), vec, vec, vec, vec, vec]
    return ops, specs


def _mu_split(wp):
    d_rwkv, lora, lora_p = wp["d_rwkv"], wp["lora"], wp["lora_p"]
    mu = wp["mu_shift"]
    return mu[None, :3 * d_rwkv], jnp.pad(mu[None, 3 * d_rwkv:], ((0, 0), (0, lora_p - lora)))


def _rwkv_chunked(pm3, plo3, shift_prev, wkv_prev, wp, *, tb, nb):
    n, t, _ = pm3.shape
    d_att, d_rwkv, lora, lora_p = wp["d_att"], wp["d_rwkv"], wp["lora"], wp["lora_p"]
    nh = d_rwkv // HEAD_DIM_B
    cb = 3 * d_att // LANE
    rb = d_rwkv // LANE
    sp_main = shift_prev[:, None, :3 * d_rwkv]
    sp_lora = jnp.pad(shift_prev[:, None, 3 * d_rwkv:], ((0, 0), (0, 0), (0, lora_p - lora)))
    mu_main, mu_lora = _mu_split(wp)
    z0 = jnp.swapaxes(wkv_prev, -1, -2)
    p_ops, p_specs = _rwkv_param_specs(wp, lambda i, h, j: (0, h))

    blk3 = lambda off: pl.BlockSpec((nb, tb, LANE), lambda i, h, j: (i, j, off + h))
    sp3 = lambda off: pl.BlockSpec((nb, 1, LANE), lambda i, h, j: (i, 0, off + h))
    vec = lambda off: pl.BlockSpec((1, LANE), lambda i, h, j: (0, off + h))
    o, zout = pl.pallas_call(
        functools.partial(_rwkv_chunk_kernel, tb=tb, nb=nb),
        grid=(n // nb, rb, t // tb),
        in_specs=[
            blk3(cb), blk3(cb + rb), blk3(cb + 2 * rb),
            pl.BlockSpec((nb, tb, lora_p), lambda i, h, j: (i, j, 0)),
            sp3(0), sp3(rb), sp3(2 * rb),
            pl.BlockSpec((nb, 1, lora_p), lambda i, h, j: (i, 0, 0)),
            vec(0), vec(rb), vec(2 * rb), pl.BlockSpec((1, lora_p), lambda i, h, j: (0, 0)),
            *p_specs,
            pl.BlockSpec((nb, 2, HEAD_DIM_B, HEAD_DIM_B), lambda i, h, j: (i, h, 0, 0)),
        ],
        out_specs=[
            pl.BlockSpec((nb, tb, LANE), lambda i, h, j: (i, j, h)),
            pl.BlockSpec((nb, 2, HEAD_DIM_B, HEAD_DIM_B), lambda i, h, j: (i, h, 0, 0)),
        ],
        out_shape=[jax.ShapeDtypeStruct((n, t, d_rwkv), BF16),
                   jax.ShapeDtypeStruct((n, nh, HEAD_DIM_B, HEAD_DIM_B), F32)],
        scratch_shapes=[pltpu.VMEM((nb, LANE, LANE), F32), pltpu.VMEM((nb, 1, LANE), F32),
                        pltpu.VMEM((nb, 1, LANE), F32), pltpu.VMEM((nb, 1, LANE), F32),
                        pltpu.VMEM((nb, 1, lora_p), F32)],
        compiler_params=_cparams(("parallel", "parallel", "arbitrary")),
        name="rwkv7_chunked",
    )(pm3, pm3, pm3, plo3, sp_main, sp_main, sp_main, sp_lora, mu_main, mu_main, mu_main, mu_lora, *p_ops, z0)
    return o, jnp.swapaxes(zout, -1, -2)


def _rwkv_step_kernel(pr_ref, pk_ref, pv_ref, plo_ref, spr_ref, spk_ref, spv_ref, splo_ref,
                      mur_ref, muk_ref, muv_ref, mulo_ref, *rest, nseq, dseq):
    prm = rest[:10]
    z0_ref, o_ref, z_ref, tr_s, y_s = rest[10:]
    hd = HEAD_DIM_B
    m = nseq * dseq
    blockdiag = _pair_blockdiag()

    def shifted(x, first, mu):
        prev = jnp.concatenate([first, x[0:m - nseq]], axis=0)
        return x + mu * (prev - x)

    xr = shifted(pr_ref[...], spr_ref[...], mur_ref[...])
    xk = shifted(pk_ref[...], spk_ref[...], muk_ref[...])
    xv = shifted(pv_ref[...], spv_ref[...], muv_ref[...])
    xl = shifted(plo_ref[...], splo_ref[...], mulo_ref[...])
    lw, aneg, bb, kmod, g = _rwkv_pre(xk, xl, prm, blockdiag)
    decay = jnp.exp(lw)
    z_ref[...] = z0_ref[...]
    ys = []
    for t in range(dseq):
        rows = slice(t * nseq, (t + 1) * nseq)
        for q, val in enumerate((aneg, bb, decay, kmod, xr, xv)):
            tr_s[q] = val[rows].T
        for hh in range(2):
            base = hh * hd
            vcol = tr_s[5, base:base + hd, :]

            def sa_body(k, acc, hh=hh, base=base):
                return acc + z_ref[hh, k] * tr_s[0, pl.ds(base + k, 1), :]

            sa = lax.fori_loop(0, hd, sa_body, jnp.zeros((hd, nseq), F32))

            def upd_body(k, y, hh=hh, base=base, sa=sa, vcol=vcol):
                zk = (z_ref[hh, k] * tr_s[2, pl.ds(base + k, 1), :] + sa * tr_s[1, pl.ds(base + k, 1), :]
                      + vcol * tr_s[3, pl.ds(base + k, 1), :])
                z_ref[hh, k] = zk
                return y + zk * tr_s[4, pl.ds(base + k, 1), :]

            y_s[base:base + hd, :] = lax.fori_loop(0, hd, upd_body, jnp.zeros((hd, nseq), F32))
        ys.append(y_s[...].T)
    y = jnp.concatenate(ys, axis=0)
    o_ref[...] = _rwkv_post(y, xr, kmod, xv, g, prm, blockdiag).astype(o_ref.dtype)


def _rwkv_steps(pm2, plo2, shift_prev, wkv_prev, wp, *, nseq, dseq):
    m = pm2.shape[0]
    d_att, d_rwkv, lora, lora_p = wp["d_att"], wp["d_rwkv"], wp["lora"], wp["lora_p"]
    nh = d_rwkv // HEAD_DIM_B
    cb = 3 * d_att // LANE
    rb = d_rwkv // LANE
    sp_main = shift_prev[:, :3 * d_rwkv]
    sp_lora = jnp.pad(shift_prev[:, 3 * d_rwkv:], ((0, 0), (0, lora_p - lora)))
    mu_main, mu_lora = _mu_split(wp)
    z0 = jnp.transpose(wkv_prev, (1, 3, 2, 0))
    p_ops, p_specs = _rwkv_param_specs(wp, lambda h: (0, h))
    blk = lambda rows, off: pl.BlockSpec((rows, LANE), lambda h: (0, off + h))
    zspec = pl.BlockSpec((2, HEAD_DIM_B, HEAD_DIM_B, nseq), lambda h: (h, 0, 0, 0))
    o, zout = pl.pallas_call(
        functools.partial(_rwkv_step_kernel, nseq=nseq, dseq=dseq),
        grid=(rb,),
        in_specs=[
            blk(m, cb), blk(m, cb + rb), blk(m, cb + 2 * rb), pl.BlockSpec((m, lora_p), lambda h: (0, 0)),
            blk(nseq, 0), blk(nseq, rb), blk(nseq, 2 * rb), pl.BlockSpec((nseq, lora_p), lambda h: (0, 0)),
            blk(1, 0), blk(1, rb), blk(1, 2 * rb), pl.BlockSpec((1, lora_p), lambda h: (0, 0)),
            *p_specs, zspec,
        ],
        out_specs=[blk(m, 0), zspec],
        out_shape=[jax.ShapeDtypeStruct((m, d_rwkv), BF16),
                   jax.ShapeDtypeStruct((nh, HEAD_DIM_B, HEAD_DIM_B, nseq), F32)],
        scratch_shapes=[pltpu.VMEM((6, LANE, nseq), F32), pltpu.VMEM((LANE, nseq), F32)],
        compiler_params=_cparams(("parallel",)),
        name="rwkv7_steps",
    )(pm2, pm2, pm2, plo2, sp_main, sp_main, sp_main, sp_lora, mu_main, mu_main, mu_main, mu_lora, *p_ops, z0)
    return o, jnp.transpose(zout, (3, 0, 2, 1))


ATT_SUB = DILATION_PAIRS[0][0] // DILATION_PAIRS[0][1]
assert all(w // d == ATT_SUB for w, d in DILATION_PAIRS)
ATT_UNITS = 4


def _rel_bucket(dist):
    max_exact = NUM_BUCKETS // 2
    d_f = jnp.maximum(dist, 1).astype(F32)
    large = max_exact + (jnp.log(d_f / max_exact) / math.log(REL_MAX_DIST / max_exact)
                         * (NUM_BUCKETS - max_exact)).astype(jnp.int32)
    large = jnp.minimum(large, NUM_BUCKETS - 1)
    return jnp.where(dist < max_exact, dist, large)


def _prompt_bias(rel_bias):
    sub = ATT_SUB
    qi = jnp.arange(sub)[:, None]
    ki = jnp.arange(2 * sub)[None, :]
    dsub = qi + sub - ki
    ok = ((dsub >= 0) & (dsub <= sub))[..., None]
    tabs = []
    for _, dil in DILATION_PAIRS:
        b = rel_bias[_rel_bucket(dil * jnp.clip(dsub, 0, sub))].astype(F32)
        tabs.append(jnp.transpose(jnp.where(ok, b, NEG_INF), (2, 0, 1)))
    return jnp.stack(tabs)


def _head_rms(x, g):
    return x * lax.rsqrt(jnp.mean(x * x, axis=-1, keepdims=True) + RMS_EPS) * g


def _attn_prompt_kernel(q_ref, k_ref, v_ref, bias_ref, qg_ref, kg_ref, o_ref, kn_ref,
                        qs, ks, acc, m_s, l_s, *, t):
    sub = ATT_SUB
    qs[...] = _head_rms(q_ref[...], qg_ref[...]) * (1.0 / math.sqrt(HEAD_DIM_A))
    kn = _head_rms(k_ref[...], kg_ref[...])
    ks[...] = kn
    kn_ref[...] = kn
    m_s[...] = jnp.full(m_s.shape, NEG_INF, F32)
    l_s[...] = jnp.zeros(l_s.shape, F32)
    acc[...] = jnp.zeros(acc.shape, F32)
    for g, (_, dil) in enumerate(DILATION_PAIRS):
        span = dil * sub
        nblk = t // span
        bias_prev = bias_ref[g, :, 0:sub]
        bias_cur = bias_ref[g, :, sub:2 * sub]
        per_r = min(dil, ATT_UNITS)
        per_n = ATT_UNITS // per_r
        for r0 in range(0, dil, per_r):
            def body(it, _, dil=dil, span=span, r0=r0, per_r=per_r, per_n=per_n,
                     bias_prev=bias_prev, bias_cur=bias_cur):
                units = []
                for dr in range(per_r):
                    for dn in range(per_n):
                        n = it * per_n + dn
                        start = r0 + dr + span * n
                        startp = r0 + dr + span * jnp.maximum(n - 1, 0)
                        if dil == 1:
                            units.append((n, pl.ds(start, sub), pl.ds(startp, sub)))
                        else:
                            units.append((n, pl.ds(start, sub, stride=dil), pl.ds(startp, sub, stride=dil)))
                qb = [qs[rows, :].astype(BF16) for _, rows, _ in units]
                sc = [_dot_nt(q, ks[rows, :].astype(BF16)) + bias_cur for q, (_, rows, _) in zip(qb, units)]
                sp = [jnp.where(n > 0, _dot_nt(q, ks[rowsp, :].astype(BF16)) + bias_prev, NEG_INF)
                      for q, (n, _, rowsp) in zip(qb, units)]
                m_old = [m_s[rows, :] for _, rows, _ in units]
                m_new = [jnp.maximum(mo, jnp.maximum(jnp.max(a, axis=-1, keepdims=True),
                                                     jnp.max(b, axis=-1, keepdims=True)))
                         for mo, a, b in zip(m_old, sc, sp)]
                pc = [jnp.exp(a - mn) for a, mn in zip(sc, m_new)]
                pp = [jnp.exp(b - mn) for b, mn in zip(sp, m_new)]
                pv = [_dot(a.astype(BF16), v_ref[rows, :].astype(BF16))
                      + _dot(b.astype(BF16), v_ref[rowsp, :].astype(BF16))
                      for a, b, (_, rows, rowsp) in zip(pc, pp, units)]
                for i, (_, rows, _) in enumerate(units):
                    alpha = jnp.exp(m_old[i] - m_new[i])
                    l_s[rows, :] = (alpha * l_s[rows, :] + jnp.sum(pc[i], axis=-1, keepdims=True)
                                    + jnp.sum(pp[i], axis=-1, keepdims=True))
                    acc[rows, :] = alpha * acc[rows, :] + pv[i]
                    m_s[rows, :] = m_new[i]
                return 0

            lax.fori_loop(0, nblk // per_n, body, 0)
    o_ref[...] = (acc[...] / l_s[...]).astype(o_ref.dtype)


def _attn_prompt(pm3, rel_bias, qg, kg, d_att):
    n, t, _ = pm3.shape
    nh = d_att // HEAD_DIM_A
    assert t % MAX_WINDOW == 0
    bias = _prompt_bias(rel_bias)
    blk = lambda off: pl.BlockSpec((None, t, HEAD_DIM_A), lambda i, h: (i, 0, off + h))
    vec = pl.BlockSpec((1, HEAD_DIM_A), lambda i, h: (0, 0))
    return pl.pallas_call(
        functools.partial(_attn_prompt_kernel, t=t),
        grid=(n, nh),
        in_specs=[blk(0), blk(nh), blk(2 * nh),
                  pl.BlockSpec((len(DILATION_PAIRS), None, ATT_SUB, 2 * ATT_SUB), lambda i, h: (0, h, 0, 0)),
                  vec, vec],
        out_specs=[blk(0), blk(0)],
        out_shape=[jax.ShapeDtypeStruct((n, t, d_att), BF16), jax.ShapeDtypeStruct((n, t, d_att), F32)],
        scratch_shapes=[pltpu.VMEM((t, HEAD_DIM_A), F32)] * 5,
        compiler_params=_cparams(("parallel", "parallel")),
        name="attn_prompt",
    )(pm3, pm3, pm3, bias, qg.reshape(1, -1), kg.reshape(1, -1))


ROWS_PAD = 8


def _sample_bias(rel_bias, dseq):
    sub = ATT_SUB
    nh = rel_bias.shape[1]
    rb = lambda dist: rel_bias[_rel_bucket(dist)].astype(F32)
    c = jnp.arange(sub)
    tabs = [rb(dil * (sub - c)) for _, dil in reversed(DILATION_PAIRS[1:])]
    for s in range(dseq):
        tabs.append(jnp.where((c >= s)[:, None], rb(jnp.clip(sub + s - c, 0, sub)), NEG_INF))
    cache = jnp.broadcast_to(jnp.stack(tabs)[..., None], (len(tabs), sub, nh, LANE))
    s = jnp.arange(dseq)[:, None]
    sp = jnp.arange(dseq)[None, :]
    new = []
    for _, dil in DILATION_PAIRS:
        ok = (sp <= s) if dil == 1 else (sp == s)
        new.append(jnp.where(ok[..., None], rb(dil * jnp.clip(s - sp, 0, sub)), NEG_INF))
    new = jnp.broadcast_to(jnp.stack(new)[..., None], (len(new), dseq, dseq, nh, LANE))
    return cache, new


def _attn_sample_kernel(x_ref, k16_ref, k4_ref, v16_ref, v4_ref, bc_ref, bn_ref, qg_ref, kg_ref,
                        o_ref, kn_ref, *, dseq):
    nh, dh = x_ref.shape[-2], x_ref.shape[-1]
    sub = ATT_SUB
    d4 = DILATION_PAIRS[1][1]
    qn = _head_rms(x_ref[0], qg_ref[...]) * (1.0 / math.sqrt(dh))
    kn = _head_rms(x_ref[1], kg_ref[...])
    vn = x_ref[2]
    kn_ref[...] = kn
    ones = jnp.ones((dh, LANE), BF16)
    tail = sub // d4

    def lane_sum(x):
        keys = x.shape[0]
        return _dot(x.reshape(keys * nh, dh).astype(BF16), ones).reshape(keys, nh, LANE)

    outs = []
    for s in range(dseq):
        q = qn[s]
        k1 = k4_ref[sub - tail:sub].reshape(sub, nh, dh)
        v1 = v4_ref[sub - tail:sub].reshape(sub, nh, dh)
        cache = [(k16_ref[:, s], v16_ref[:, s], bc_ref[0]), (k4_ref[:, s], v4_ref[:, s], bc_ref[1]),
                 (k1, v1, bc_ref[2 + s])]
        logits = [lane_sum(kk * q[None]) + bias for kk, _, bias in cache]
        new = []
        for s2 in range(s + 1):
            ln = jnp.broadcast_to(jnp.sum(q * kn[s2], axis=-1, keepdims=True), (nh, LANE))
            new += [ln + bn_ref[g, s, s2] for g in range(len(DILATION_PAIRS))]
        m = functools.reduce(jnp.maximum, [jnp.max(x, axis=0) for x in logits] + new)
        p = [jnp.exp(x - m[None]) for x in logits]
        pn = [jnp.exp(x - m) for x in new]
        denom = functools.reduce(jnp.add, [jnp.sum(x, axis=0) for x in p] + pn)
        o = functools.reduce(jnp.add, [jnp.sum(pi * vv, axis=0) for pi, (_, vv, _) in zip(p, cache)])
        for s2 in range(s + 1):
            w = functools.reduce(jnp.add, pn[s2 * len(DILATION_PAIRS):(s2 + 1) * len(DILATION_PAIRS)])
            o = o + w * vn[s2]
        outs.append(o / denom)
    outs += [jnp.zeros((nh, dh), F32)] * (ROWS_PAD - dseq)
    o_ref[...] = jnp.stack(outs)


def _attn_sample(x5, cache_k, cache_v, rel_bias, qg, kg, dseq):
    n, wb, nh, dh = cache_k.shape
    d4, d16 = DILATION_PAIRS[1][1], DILATION_PAIRS[2][1]
    assert wb == MAX_WINDOW == d16 * ATT_SUB and dseq <= d4 and DILATION_PAIRS[0][1] == 1 and dh == LANE
    bias_c, bias_n = _sample_bias(rel_bias, dseq)
    view = lambda c, d: c.reshape(n, wb // d, d, nh, dh)
    s16 = pl.BlockSpec((None, ATT_SUB, dseq, nh, dh), lambda i: (i, 0, 0, 0, 0))
    s4 = pl.BlockSpec((None, ATT_SUB, d4, nh, dh), lambda i: (i, wb // d4 // ATT_SUB - 1, 0, 0, 0))
    row_spec = pl.BlockSpec((None, ROWS_PAD, nh, dh), lambda i: (i, 0, 0, 0))
    vec = pl.BlockSpec((1, dh), lambda i: (0, 0))
    const = lambda a: pl.BlockSpec(a.shape, lambda i: (0,) * a.ndim, pipeline_mode=pl.Buffered(1))
    return pl.pallas_call(
        functools.partial(_attn_sample_kernel, dseq=dseq),
        grid=(n,),
        in_specs=[pl.BlockSpec((None, 3, ROWS_PAD, nh, dh), lambda i: (i, 0, 0, 0, 0)), s16, s4, s16, s4,
                  const(bias_c), const(bias_n), vec, vec],
        out_specs=[row_spec, row_spec],
        out_shape=[jax.ShapeDtypeStruct((n, ROWS_PAD, nh, dh), F32)] * 2,
        compiler_params=_cparams(("parallel",)),
        name="attn_sample",
    )(x5, view(cache_k, d16), view(cache_k, d4), view(cache_v, d16), view(cache_v, d4), bias_c, bias_n,
      qg.reshape(1, -1), kg.reshape(1, -1))


def _prep_weights(lp):
    (g_mix, w_in, q_norm_g, k_norm_g, mu_shift, w0, w2, a0, a2, g2, k_k, k_a, r_k, ln_x_w, ln_x_b, w_out,
     g_ffn, w_gate, w_up, conv_w, conv_b, w_down, g_ple, w_ple, w_ple_gate) = lp
    d_model = w_in.shape[0]
    d_rwkv = w0.shape[0]
    d_att = w_out.shape[0] - d_rwkv
    main = 3 * d_att + 3 * d_rwkv
    lora = w_in.shape[1] - main
    lora_p = _round_up(lora, LANE)
    d_ff = w_gate.shape[1]
    dffp = _round_up(d_ff, FF_ALIGN)
    padc = lambda w, n: jnp.pad(w, ((0, 0), (0, n - w.shape[1])))
    return dict(
        d_model=d_model, d_att=d_att, d_rwkv=d_rwkv, lora=lora, lora_p=lora_p, d_ff=d_ff, dffp=dffp,
        g_mix=g_mix, g_ffn=g_ffn, g_ple=g_ple, q_norm_g=q_norm_g, k_norm_g=k_norm_g,
        w_in_main=w_in[:, :main].astype(BF16),
        w_in_lora=padc(w_in[:, main:], lora_p).astype(BF16),
        mu_shift=mu_shift, w0=w0, w2=w2, a0=a0, a2=a2, g2=g2, k_k=k_k, k_a=k_a, r_k=r_k,
        ln_x_w=ln_x_w, ln_x_b=ln_x_b,
        w_out=w_out.astype(BF16),
        w_gate=padc(w_gate, dffp).astype(BF16), w_up=padc(w_up, dffp).astype(BF16),
        conv_w=padc(conv_w, dffp), conv_b=jnp.pad(conv_b, (0, dffp - d_ff)),
        w_down=jnp.pad(w_down, ((0, dffp - d_ff), (0, 0))).astype(BF16),
        w_ple=w_ple.astype(BF16), w_ple_gate=w_ple_gate.astype(BF16),
    )


def _layer(x2d, p2d, wp, mixer, *, tm, shift, conv_prev, tiles_per_group):
    d_model, dffp = wp["d_model"], wp["dffp"]
    tn = 512
    xn = _rmsnorm(x2d, wp["g_mix"], min(tm, 256))
    proj_main = _matmul(xn, wp["w_in_main"], tm=tm, tn=tn, tk=d_model, name="in_proj_main")
    proj_lora = _matmul(xn, wp["w_in_lora"], tm=tm, tn=wp["lora_p"], tk=d_model, name="in_proj_lora")
    mix, aux = mixer(proj_main, proj_lora)
    h1 = _matmul(mix, wp["w_out"], tm=tm, tn=tn, tk=mix.shape[1], mode="residual", extras=(x2d,), name="out_proj")
    hn = _rmsnorm(h1, wp["g_ffn"], min(tm, 256))
    cprev = jnp.pad(conv_prev, ((0, 0), (0, 0), (0, dffp - conv_prev.shape[-1])))
    act, conv_tail = _ffn_act(hn, wp["w_gate"], wp["w_up"], wp["conv_w"], wp["conv_b"], cprev,
                              tm=tm, tf=FF_ALIGN, shift=shift, tiles_per_group=tiles_per_group)
    tk_down = 1024 if dffp % 1024 == 0 else FF_ALIGN
    h2 = _matmul(act, wp["w_down"], tm=tm, tn=min(1024, d_model), tk=tk_down, mode="residual", extras=(h1,),
                 name="ffn_down")
    hn2 = _rmsnorm(h2, wp["g_ple"], min(tm, 256))
    y = _matmul(hn2, wp["w_ple_gate"], tm=tm, tn=tn, tk=d_model, mode="ple",
                extras=(h2, p2d.astype(BF16), wp["w_ple"]), name="ple_gate")
    conv_tail = conv_tail[tiles_per_group - 1::tiles_per_group, :, :wp["d_ff"]]
    return y, conv_tail, aux


def kernel(x_prompt, x_sample, cache_k, cache_v, state_shift, state_wkv, state_conv, p_prompt, p_sample, rel_bias, g_mix, w_in, q_norm_g, k_norm_g, mu_shift, w0, w2, a0, a2, g2, k_k, k_a, r_k, ln_x_w, ln_x_b, w_out, g_ffn, w_gate, w_up, conv_w, conv_b, w_down, g_ple, w_ple, w_ple_gate):
    depth = g_mix.shape[0]
    nbp, seq, d_model = x_prompt.shape
    nbs, dseq, _ = x_sample.shape
    hp = x_prompt.reshape(nbp * seq, d_model)
    hs = jnp.swapaxes(x_sample, 0, 1).reshape(dseq * nbs, d_model)
    outs_p, outs_s = [], []
    for i in range(depth):
        lp = (g_mix[i], w_in[i], q_norm_g[i], k_norm_g[i], mu_shift[i], w0[i], w2[i], a0[i], a2[i], g2[i],
              k_k[i], k_a[i], r_k[i], ln_x_w[i], ln_x_b[i], w_out[i], g_ffn[i], w_gate[i], w_up[i],
              conv_w[i], conv_b[i], w_down[i], g_ple[i], w_ple[i], w_ple_gate[i])
        wp = _prep_weights(lp)
        d_att, d_rwkv, lora, d_ff = wp["d_att"], wp["d_rwkv"], wp["lora"], wp["d_ff"]
        nha = d_att // HEAD_DIM_A
        nhb = d_rwkv // HEAD_DIM_B
        qg, kg = q_norm_g[i], k_norm_g[i]

        def mixer_prompt(proj_main, proj_lora, shift_prev, wkv_prev):
            pm = proj_main.reshape(nbp, seq, -1)
            plo = proj_lora.reshape(nbp, seq, -1)
            o_att, kn = _attn_prompt(pm, rel_bias, qg, kg, d_att)
            o_rwkv, wkv_new = _rwkv_chunked(pm, plo, shift_prev, wkv_prev, wp, tb=min(seq, 256),
                                            nb=2 if nbp % 2 == 0 else 1)
            keep = min(MAX_WINDOW, seq)
            heads = lambda z: z[:, seq - keep:].reshape(nbp, keep, nha, HEAD_DIM_A)
            shift_new = jnp.concatenate([pm[:, -1, 3 * d_att:], plo[:, -1, :lora]], axis=-1)
            mix = jnp.concatenate([o_att, o_rwkv], axis=-1).reshape(nbp * seq, -1)
            return mix, (heads(kn), heads(pm[..., 2 * d_att:3 * d_att]), shift_new, wkv_new)

        def mixer_sample(proj_main, proj_lora, ck, cv, shift_prev, wkv_prev):
            pm = proj_main.reshape(dseq, nbs, -1)
            x5 = jnp.transpose(pm[..., :3 * d_att].reshape(dseq, nbs, 3, nha, HEAD_DIM_A), (1, 2, 0, 3, 4))
            x5 = jnp.pad(x5, ((0, 0), (0, 0), (0, ROWS_PAD - dseq), (0, 0), (0, 0)))
            o_att, kn = _attn_sample(x5, ck, cv, rel_bias, qg, kg, dseq)
            o_att = jnp.swapaxes(o_att[:, :dseq], 0, 1).reshape(dseq * nbs, d_att).astype(BF16)
            o_rwkv, wkv_new = _rwkv_steps(proj_main, proj_lora, shift_prev, wkv_prev, wp, nseq=nbs, dseq=dseq)
            shift_new = jnp.concatenate([pm[-1, :, 3 * d_att:], proj_lora[(dseq - 1) * nbs:, :lora]], axis=-1)
            mix = jnp.concatenate([o_att, o_rwkv], axis=-1)
            v_new = jnp.swapaxes(pm[..., 2 * d_att:3 * d_att], 0, 1).reshape(nbs, dseq, nha, HEAD_DIM_A)
            return mix, (kn[:, :dseq], v_new, shift_new, wkv_new)

        tm_p = min(1024, seq)
        zero_shift = jnp.zeros((nbp, 3 * d_rwkv + lora), F32)
        zero_wkv = jnp.zeros((nbp, nhb, HEAD_DIM_B, HEAD_DIM_B), F32)
        zero_conv = jnp.zeros((nbp, CONV_WIDTH - 1, d_ff), F32)
        mix_p = functools.partial(mixer_prompt, shift_prev=zero_shift, wkv_prev=zero_wkv)
        hp, conv_p, aux_p = _layer(hp, p_prompt[i].reshape(nbp * seq, -1), wp, mix_p, tm=tm_p, shift=1,
                                   conv_prev=zero_conv, tiles_per_group=seq // tm_p)
        outs_p.append((*aux_p, conv_p))

        mix_s = functools.partial(mixer_sample, ck=cache_k[i], cv=cache_v[i],
                                  shift_prev=state_shift[i], wkv_prev=state_wkv[i])
        conv_prev_s = jnp.swapaxes(state_conv[i], 0, 1).reshape(1, (CONV_WIDTH - 1) * nbs, d_ff)
        p_s = jnp.swapaxes(p_sample[i], 0, 1).reshape(dseq * nbs, -1)
        hs, conv_s, aux_s = _layer(hs, p_s, wp, mix_s, tm=dseq * nbs, shift=nbs, conv_prev=conv_prev_s,
                                   tiles_per_group=1)
        conv_s = jnp.swapaxes(conv_s.reshape(CONV_WIDTH - 1, nbs, d_ff), 0, 1)
        outs_s.append((*aux_s, conv_s))

    y_p = hp.reshape(nbp, seq, d_model)
    y_s = jnp.swapaxes(hs.reshape(dseq, nbs, d_model), 0, 1)
    st = lambda outs, idx: jnp.stack([o[idx] for o in outs])
    return (y_p, y_s, st(outs_p, 0), st(outs_p, 1), st(outs_p, 2), st(outs_p, 3), st(outs_p, 4),
            st(outs_s, 0), st(outs_s, 1), st(outs_s, 2), st(outs_s, 3), st(outs_s, 4))
```

```python
import functools
import math

import jax
import jax.numpy as jnp
from jax import lax
from jax.experimental import pallas as pl
from jax.experimental.pallas import tpu as pltpu

F32 = jnp.float32
BF16 = jnp.bfloat16

HEAD_DIM_A = 128
HEAD_DIM_B = 64
DILATION_PAIRS = ((128, 1), (512, 4), (2048, 16))
MAX_WINDOW = max(w for w, _ in DILATION_PAIRS)
NUM_BUCKETS = 32
REL_MAX_DIST = MAX_WINDOW
DECAY_LORA = 128
AAA_LORA = 128
CONV_WIDTH = 3
RMS_EPS = 1e-6
GN_EPS = 64e-5
NEG_INF = -1e30

LANE = 128
FF_ALIGN = 512
VMEM_LIMIT = 56 * 1024 * 1024
_HI = lax.Precision.HIGHEST


def _cparams(sem):
    return pltpu.CompilerParams(dimension_semantics=sem, vmem_limit_bytes=VMEM_LIMIT)


def _round_up(x, m):
    return -(-x // m) * m


def _dot(a, b, prec=None):
    return jnp.dot(a, b, preferred_element_type=F32, precision=prec)


def _dot_nt(a, b, prec=None):
    return lax.dot_general(a, b, (((1,), (1,)), ((), ())), preferred_element_type=F32, precision=prec)


def _bdot(a, b):
    return _dot(a.astype(BF16), b.astype(BF16))


def _rms_kernel(x_ref, g_ref, o_ref):
    x = x_ref[...]
    ms = jnp.mean(x * x, axis=-1, keepdims=True)
    o_ref[...] = (x * lax.rsqrt(ms + RMS_EPS) * g_ref[...]).astype(o_ref.dtype)


def _rmsnorm(x, g, tm):
    m, d = x.shape
    return pl.pallas_call(
        _rms_kernel,
        grid=(m // tm,),
        in_specs=[pl.BlockSpec((tm, d), lambda i: (i, 0)), pl.BlockSpec((1, d), lambda i: (0, 0))],
        out_specs=pl.BlockSpec((tm, d), lambda i: (i, 0)),
        out_shape=jax.ShapeDtypeStruct((m, d), BF16),
        compiler_params=_cparams(("parallel",)),
        name="rmsnorm",
    )(x, g.reshape(1, d))


_N_EXTRA = {"plain": 0, "residual": 1, "ple": 3}


def _cast_tile(b, row0, col0, k_valid, n_valid):
    b16 = b.astype(BF16)
    if k_valid is not None:
        r = row0 + lax.broadcasted_iota(jnp.int32, (b.shape[0], 1), 0)
        b16 = jnp.where(r < k_valid, b16, jnp.zeros_like(b16))
    if n_valid is not None:
        c = col0 + lax.broadcasted_iota(jnp.int32, (1, b.shape[1]), 1)
        b16 = jnp.where(c < n_valid, b16, jnp.zeros_like(b16))
    return b16


def _mm_kernel(*refs, nk, mode, emit, k_valid, n_valid):
    a_ref, b_ref = refs[0], refs[1]
    n_extra = _N_EXTRA[mode]
    o_ref = refs[2 + n_extra]
    acc_ref = refs[-1] if nk > 1 else None

    def epilogue(acc):
        if mode == "plain":
            return acc
        if mode == "residual":
            return refs[2][...] + acc
        h_ref, p_ref, wp_ref = refs[2], refs[3], refs[4]
        ple = _dot(p_ref[...], wp_ref[...])
        return h_ref[...] + ple * jax.nn.sigmoid(acc)

    b = b_ref[...]
    if emit:
        tk, tn = b.shape
        b = _cast_tile(b, pl.program_id(2) * tk, pl.program_id(1) * tn, k_valid, n_valid)
        refs[3 + n_extra][...] = b
    part = _dot(a_ref[...], b)
    if nk == 1:
        o_ref[...] = epilogue(part).astype(o_ref.dtype)
        return
    k = pl.program_id(2)

    @pl.when(k == 0)
    def _():
        acc_ref[...] = part

    @pl.when(k > 0)
    def _():
        acc_ref[...] += part

    @pl.when(k == nk - 1)
    def _():
        o_ref[...] = epilogue(acc_ref[...]).astype(o_ref.dtype)


def _matmul(a, b, *, tm, tn, tk, mode="plain", extras=(), name="matmul"):
    m, kp = a.shape
    emit = b.dtype != BF16
    np_ = _round_up(b.shape[1], tn)
    assert kp % tk == 0 and m % tm == 0 and (emit or b.shape == (kp, np_)) and not (emit and m != tm)
    nk = kp // tk
    k_valid = b.shape[0] if emit and b.shape[0] != kp else None
    n_valid = b.shape[1] if emit and b.shape[1] != np_ else None
    last_k, last_j = pl.cdiv(b.shape[0], tk) - 1, pl.cdiv(b.shape[1], tn) - 1
    in_specs = [pl.BlockSpec((tm, tk), lambda i, j, k: (i, k)),
                pl.BlockSpec((tk, tn), lambda i, j, k: (jnp.minimum(k, last_k), jnp.minimum(j, last_j)))]
    if mode == "residual":
        in_specs.append(pl.BlockSpec((tm, tn), lambda i, j, k: (i, j)))
    elif mode == "ple":
        pdim = extras[1].shape[1]
        in_specs += [pl.BlockSpec((tm, tn), lambda i, j, k: (i, j)),
                     pl.BlockSpec((tm, pdim), lambda i, j, k: (i, 0)),
                     pl.BlockSpec((pdim, tn), lambda i, j, k: (0, j))]
    out_specs = [pl.BlockSpec((tm, tn), lambda i, j, k: (i, j))]
    out_shape = [jax.ShapeDtypeStruct((m, np_), F32)]
    if emit:
        out_specs.append(pl.BlockSpec((tk, tn), lambda i, j, k: (k, j)))
        out_shape.append(jax.ShapeDtypeStruct((kp, np_), BF16))
    scratch = [pltpu.VMEM((tm, tn), F32)] if nk > 1 else []
    res = pl.pallas_call(
        functools.partial(_mm_kernel, nk=nk, mode=mode, emit=emit, k_valid=k_valid, n_valid=n_valid),
        grid=(m // tm, np_ // tn, nk),
        in_specs=in_specs,
        out_specs=out_specs,
        out_shape=out_shape,
        scratch_shapes=scratch,
        compiler_params=_cparams(("parallel", "parallel", "arbitrary")),
        name=name,
    )(a, b, *extras)
    return tuple(res) if emit else res[0]


FFN_SUB = 256


def _ffn_kernel(a_ref, wg_ref, wu_ref, cw_ref, cb_ref, prev_ref, act_ref, tail_ref, *rest,
                tm, tf, shift, tiles_per_group, emit, n_valid):
    i = pl.program_id(0)
    j = pl.program_id(1)
    hist = (CONV_WIDTH - 1) * shift
    pad = _round_up(hist, 8)
    ext_ref, carry_ref = rest[-2], rest[-1]
    a = a_ref[...]
    sw = min(tf, FFN_SUB)
    subs = [slice(s, s + sw) for s in range(0, tf, sw)]

    def weight(w_ref, w16_ref, cols):
        w = w_ref[:, cols]
        if emit:
            w = _cast_tile(w, 0, j * tf + cols.start, None, n_valid)
            w16_ref[:, cols] = w
        return w

    us = [_dot(a, weight(wg_ref, rest[0], cols)) for cols in subs]
    ups = [_dot(a, weight(wu_ref, rest[1], cols)) for cols in subs]

    @pl.when(i % tiles_per_group == 0)
    def _():
        ext_ref[pl.ds(pad - hist, hist), :] = prev_ref[0]

    @pl.when(i % tiles_per_group != 0)
    def _():
        ext_ref[pl.ds(pad - hist, hist), :] = carry_ref[j]

    for cols, u, up in zip(subs, us, ups):
        ext_ref[pl.ds(pad, tm), cols] = u
        c = (cb_ref[:, cols] + cw_ref[0:1, cols] * ext_ref[pl.ds(pad - 2 * shift, tm), cols]
             + cw_ref[1:2, cols] * ext_ref[pl.ds(pad - shift, tm), cols] + cw_ref[2:3, cols] * u)
        act_ref[:, cols] = (jax.nn.silu(c) * up).astype(act_ref.dtype)
    tail = ext_ref[pl.ds(pad + tm - hist, hist), :]
    carry_ref[j] = tail
    tail_ref[0] = tail


def _ffn_act(xn, wg, wu, conv_w, conv_b, conv_prev, *, tm, tf, shift, tiles_per_group):
    m, d = xn.shape
    dffp = conv_w.shape[1]
    emit = wg.dtype != BF16
    assert dffp % tf == 0 and (emit or wg.shape[1] == dffp) and not (emit and m != tm)
    hist = (CONV_WIDTH - 1) * shift
    nj = dffp // tf
    n_valid = wg.shape[1] if emit and wg.shape[1] != dffp else None
    kern = functools.partial(_ffn_kernel, tm=tm, tf=tf, shift=shift, tiles_per_group=tiles_per_group,
                             emit=emit, n_valid=n_valid)
    last = pl.cdiv(wg.shape[1], tf) - 1
    wspec = pl.BlockSpec((d, tf), lambda i, j: (0, jnp.minimum(j, last)))
    out_specs = [pl.BlockSpec((tm, tf), lambda i, j: (i, j)), pl.BlockSpec((1, hist, tf), lambda i, j: (i, 0, j))]
    out_shape = [jax.ShapeDtypeStruct((m, dffp), BF16), jax.ShapeDtypeStruct((m // tm, hist, dffp), F32)]
    if emit:
        out_specs += [pl.BlockSpec((d, tf), lambda i, j: (0, j))] * 2
        out_shape += [jax.ShapeDtypeStruct((d, dffp), BF16)] * 2
    return pl.pallas_call(
        kern,
        grid=(m // tm, nj),
        in_specs=[
            pl.BlockSpec((tm, d), lambda i, j: (i, 0)), wspec, wspec,
            pl.BlockSpec((CONV_WIDTH, tf), lambda i, j: (0, j)),
            pl.BlockSpec((1, tf), lambda i, j: (0, j)),
            pl.BlockSpec((1, hist, tf), lambda i, j: (i // tiles_per_group, 0, j)),
        ],
        out_specs=out_specs,
        out_shape=out_shape,
        scratch_shapes=[pltpu.VMEM((_round_up(hist, 8) + tm, tf), F32), pltpu.VMEM((nj, hist, tf), F32)],
        compiler_params=_cparams(("arbitrary", "arbitrary")),
        name="ffn_gate_up_conv",
    )(xn, wg, wu, conv_w, conv_b.reshape(1, dffp), conv_prev)


RWKV_CHUNK = 64


def _softplus(z):
    return jnp.maximum(z, 0.0) + jnp.log(1.0 + jnp.exp(-jnp.abs(z)))


def _pair_blockdiag():
    li = lax.broadcasted_iota(jnp.int32, (LANE, LANE), 0)
    lj = lax.broadcasted_iota(jnp.int32, (LANE, LANE), 1)
    return (li // HEAD_DIM_B == lj // HEAD_DIM_B).astype(F32)


def _rwkv_pre(xk, xl, prm, blockdiag):
    w0_ref, a0_ref, w2_ref, a2_ref, g2_ref, kk_ref, ka_ref = prm[:7]
    wd = xl[:, 0:DECAY_LORA]
    ad = xl[:, DECAY_LORA:DECAY_LORA + AAA_LORA]
    gd = xl[:, DECAY_LORA + AAA_LORA:]
    wl = w0_ref[...] + _dot(jnp.tanh(wd).astype(BF16), w2_ref[...])
    lw = -jnp.exp(-_softplus(-wl) - 0.5)
    a = jax.nn.sigmoid(a0_ref[...] + _dot(ad.astype(BF16), a2_ref[...]))
    g = _dot(jax.nn.sigmoid(gd).astype(BF16), g2_ref[...])
    kk = xk * kk_ref[...]
    kk = kk / jnp.maximum(jnp.sqrt(_dot(kk * kk, blockdiag, _HI)), 1e-12)
    kmod = xk * (1.0 + (a - 1.0) * ka_ref[...])
    return lw, -kk, kk * a, kmod, g


def _rwkv_post(y, xr, kmod, xv, g, prm, blockdiag):
    rk_ref, lnw_ref, lnb_ref = prm[7:10]
    inv = 1.0 / HEAD_DIM_B
    mean = _dot(y, blockdiag, _HI) * inv
    d = y - mean
    var = _dot(d * d, blockdiag, _HI) * inv
    yn = d * lax.rsqrt(var + GN_EPS) * lnw_ref[...] + lnb_ref[...]
    bonus = _dot(xr * kmod * rk_ref[...], blockdiag, _HI) * xv
    return (yn + bonus) * g


def _rwkv_chunk_kernel(pr_ref, pk_ref, pv_ref, plo_ref, spr_ref, spk_ref, spv_ref, splo_ref,
                       mur_ref, muk_ref, muv_ref, mulo_ref, *rest, tb, nb):
    prm = rest[:10]
    z0_ref, o_ref, zout_ref, z_s, cr_s, ck_s, cv_s, clo_s = rest[10:]
    ti = pl.program_id(2)
    nt = pl.num_programs(2)
    cs = RWKV_CHUNK
    hd = HEAD_DIM_B

    lane = lax.broadcasted_iota(jnp.int32, (1, LANE), 1)
    m0 = (lane < hd).astype(F32)
    m1 = 1.0 - m0
    blockdiag = _pair_blockdiag()
    li = lax.broadcasted_iota(jnp.int32, (LANE, LANE), 0)
    lj = lax.broadcasted_iota(jnp.int32, (LANE, LANE), 1)
    eye = (li == lj).astype(F32)
    same = li // cs == lj // cs
    mask_incl = (same & (li >= lj)).astype(F32)
    mask_strict = (same & (li > lj)).astype(F32)
    ci = lax.broadcasted_iota(jnp.int32, (cs, cs), 0)
    cj = lax.broadcasted_iota(jnp.int32, (cs, cs), 1)
    tril_incl = (ci >= cj).astype(F32)
    row = lax.broadcasted_iota(jnp.int32, (tb, 1), 0)

    @pl.when(ti == 0)
    def _():
        z_s[...] = jnp.zeros(z_s.shape, F32)
        for q in range(nb):
            z_s[q, 0:hd, 0:hd] = z0_ref[q, 0]
            z_s[q, hd:2 * hd, hd:2 * hd] = z0_ref[q, 1]
        cr_s[...] = spr_ref[...]
        ck_s[...] = spk_ref[...]
        cv_s[...] = spv_ref[...]
        clo_s[...] = splo_ref[...]

    def shifted(x, carry_row, mu):
        prev = jnp.where(row == 0, carry_row, pltpu.roll(x, 1, 0))
        return x + mu * (prev - x)

    two = lambda x: jnp.concatenate([x * m0, x * m1], axis=0)
    fold = lambda x: x[0:cs] + x[cs:2 * cs]

    seqs = []
    for q in range(nb):
        pr, pk, pv, plo = pr_ref[q], pk_ref[q], pv_ref[q], plo_ref[q]
        xr = shifted(pr, cr_s[q], mur_ref[...])
        xk = shifted(pk, ck_s[q], muk_ref[...])
        xv = shifted(pv, cv_s[q], muv_ref[...])
        xl = shifted(plo, clo_s[q], mulo_ref[...])
        cr_s[q] = pr[tb - 1:tb]
        ck_s[q] = pk[tb - 1:tb]
        cv_s[q] = pv[tb - 1:tb]
        clo_s[q] = plo[tb - 1:tb]
        lw, aneg, bb, kmod, g = _rwkv_pre(xk, xl, prm, blockdiag)
        seqs.append(dict(xr=xr, xv=xv, lw=lw, aneg=aneg, bb=bb, kmod=kmod, g=g))

    nchunk = tb // cs
    units = []
    for q in range(nb):
        for c in range(nchunk):
            rows = slice(c * cs, (c + 1) * cs)
            units.append({k: v[rows] for k, v in seqs[q].items() if k != "g"})
    tril16 = tril_incl.astype(BF16)
    for u in units:
        lw1 = u["lw"].astype(BF16)
        r1 = u["lw"] - lw1.astype(F32)
        lw2 = r1.astype(BF16)
        lw3 = (r1 - lw2.astype(F32)).astype(BF16)
        u["cum"] = _dot(tril16, lw1) + (_dot(tril16, lw2) + _dot(tril16, lw3))
    for u in units:
        cum = u["cum"]
        tot = cum[cs - 1:cs]
        e_inv = jnp.exp(-cum)
        e_end = jnp.exp(tot - cum)
        u["rt"] = u["xr"] * jnp.exp(cum)
        u["la"] = two(u["aneg"] * jnp.exp(cum - u["lw"]))
        u["mt"] = jnp.concatenate([u["bb"] * e_end, u["kmod"] * e_end], axis=0).T
        u["pc_col"] = jnp.broadcast_to(jnp.exp(tot), (LANE, LANE)).T
        u["gm"] = _dot_nt(jnp.concatenate([u["la"], two(u["rt"])], axis=0).astype(BF16),
                          jnp.concatenate([two(u["bb"] * e_inv), two(u["kmod"] * e_inv)], axis=0).astype(BF16))
    for u in units:
        gm = u.pop("gm")
        u["n_pow"] = gm[0:LANE, 0:LANE] * mask_strict
        u["tinv"] = eye + u["n_pow"]
        u["arb"] = gm[LANE:2 * LANE, 0:LANE] * mask_incl
        u["av"] = _bdot(jnp.concatenate([gm[0:LANE, LANE:2 * LANE] * mask_strict,
                                         gm[LANE:2 * LANE, LANE:2 * LANE] * mask_incl], axis=0), two(u["xv"]))
    for _ in range(int(math.log2(cs)) - 1):
        for u in units:
            u["n_pow"] = _bdot(u["n_pow"], u["n_pow"])
        for u in units:
            u["tinv"] = u["tinv"] + _bdot(u["tinv"], u["n_pow"])
    for u in units:
        wu = _bdot(u["tinv"], jnp.concatenate([u["la"], u["av"][0:LANE]], axis=1))
        u["w"] = fold(wu[:, 0:LANE])
        u["u0"] = fold(wu[:, LANE:2 * LANE])
        u["y0"] = fold(u["av"][LANE:2 * LANE])
    for u in units:
        vc = u["xv"]
        zz = _bdot(u["mt"], jnp.concatenate([jnp.concatenate([u["w"], u["u0"]], axis=1),
                                             jnp.concatenate([jnp.zeros_like(vc), vc], axis=1)], axis=0))
        u["zm"] = zz[:, 0:LANE] * blockdiag
        u["zc"] = zz[:, LANE:2 * LANE] * blockdiag

    zs = [z_s[q] for q in range(nb)]
    ys = [[] for _ in range(nb)]
    for c in range(nchunk):
        yus = [_bdot(jnp.concatenate([units[q * nchunk + c]["rt"], units[q * nchunk + c]["w"]], axis=0), zs[q])
               for q in range(nb)]
        zms = [_bdot(units[q * nchunk + c]["zm"], zs[q]) for q in range(nb)]
        for q in range(nb):
            u = units[q * nchunk + c]
            uu = yus[q][cs:2 * cs] + u["u0"]
            ys[q].append(yus[q][0:cs] + u["y0"] + fold(_bdot(u["arb"], two(uu))))
            zs[q] = u["pc_col"] * zs[q] + zms[q] + u["zc"]
    for q in range(nb):
        z_s[q] = zs[q]
        s = seqs[q]
        y = jnp.concatenate(ys[q], axis=0)
        o_ref[q] = _rwkv_post(y, s["xr"], s["kmod"], s["xv"], s["g"], prm, blockdiag).astype(o_ref.dtype)

    @pl.when(ti == nt - 1)
    def _():
        for q in range(nb):
            zout_ref[q, 0] = z_s[q, 0:hd, 0:hd]
            zout_ref[q, 1] = z_s[q, hd:2 * hd, hd:2 * hd]


def _rwkv_param_specs(wp, imap):
    d_rwkv, lora, lora_p = wp["d_rwkv"], wp["lora"], wp["lora_p"]
    gpad = lora_p - DECAY_LORA - AAA_LORA
    glora = lora - DECAY_LORA - AAA_LORA
    row2 = lambda v: v.reshape(1, d_rwkv)
    g2p = jnp.pad(wp["g2"], ((0, gpad - glora), (0, 0))).astype(BF16)
    vec = pl.BlockSpec((1, LANE), imap)
    ops = [row2(wp["w0"]), row2(wp["a0"]), wp["w2"].astype(BF16), wp["a2"].astype(BF16), g2p,
           row2(wp["k_k"]), row2(wp["k_a"]), row2(wp["r_k"]), row2(wp["ln_x_w"]), row2(wp["ln_x_b"])]
    specs = [vec, vec, pl.BlockSpec((DECAY_LORA, LANE), imap), pl.BlockSpec((AAA_LORA, LANE), imap),
             pl.BlockSpec((gpad, LANE), imap), vec, vec, vec, vec, vec]
    return ops, specs


def _mu_split(wp):
    d_rwkv, lora, lora_p = wp["d_rwkv"], wp["lora"], wp["lora_p"]
    mu = wp["mu_shift"]
    return mu[None, :3 * d_rwkv], jnp.pad(mu[None, 3 * d_rwkv:], ((0, 0), (0, lora_p - lora)))


def _rwkv_chunked(pm3, shift_prev, wkv_prev, wp, *, tb, nb):
    plo3, lblk = pm3, wp["lora_blk"]
    n, t, _ = pm3.shape
    d_att, d_rwkv, lora, lora_p = wp["d_att"], wp["d_rwkv"], wp["lora"], wp["lora_p"]
    nh = d_rwkv // HEAD_DIM_B
    cb = 3 * d_att // LANE
    rb = d_rwkv // LANE
    sp_main = shift_prev[:, None, :3 * d_rwkv]
    sp_lora = jnp.pad(shift_prev[:, None, 3 * d_rwkv:], ((0, 0), (0, 0), (0, lora_p - lora)))
    mu_main, mu_lora = _mu_split(wp)
    z0 = jnp.swapaxes(wkv_prev, -1, -2)
    p_ops, p_specs = _rwkv_param_specs(wp, lambda i, h, j: (0, h))

    blk3 = lambda off: pl.BlockSpec((nb, tb, LANE), lambda i, h, j: (i, j, off + h))
    sp3 = lambda off: pl.BlockSpec((nb, 1, LANE), lambda i, h, j: (i, 0, off + h))
    vec = lambda off: pl.BlockSpec((1, LANE), lambda i, h, j: (0, off + h))
    o, zout = pl.pallas_call(
        functools.partial(_rwkv_chunk_kernel, tb=tb, nb=nb),
        grid=(n // nb, rb, t // tb),
        in_specs=[
            blk3(cb), blk3(cb + rb), blk3(cb + 2 * rb),
            pl.BlockSpec((nb, tb, lora_p), lambda i, h, j: (i, j, lblk)),
            sp3(0), sp3(rb), sp3(2 * rb),
            pl.BlockSpec((nb, 1, lora_p), lambda i, h, j: (i, 0, 0)),
            vec(0), vec(rb), vec(2 * rb), pl.BlockSpec((1, lora_p), lambda i, h, j: (0, 0)),
            *p_specs,
            pl.BlockSpec((nb, 2, HEAD_DIM_B, HEAD_DIM_B), lambda i, h, j: (i, h, 0, 0)),
        ],
        out_specs=[
            pl.BlockSpec((nb, tb, LANE), lambda i, h, j: (i, j, h)),
            pl.BlockSpec((nb, 2, HEAD_DIM_B, HEAD_DIM_B), lambda i, h, j: (i, h, 0, 0)),
        ],
        out_shape=[jax.ShapeDtypeStruct((n, t, d_rwkv), BF16),
                   jax.ShapeDtypeStruct((n, nh, HEAD_DIM_B, HEAD_DIM_B), F32)],
        scratch_shapes=[pltpu.VMEM((nb, LANE, LANE), F32), pltpu.VMEM((nb, 1, LANE), F32),
                        pltpu.VMEM((nb, 1, LANE), F32), pltpu.VMEM((nb, 1, LANE), F32),
                        pltpu.VMEM((nb, 1, lora_p), F32)],
        compiler_params=_cparams(("parallel", "parallel", "arbitrary")),
        name="rwkv7_chunked",
    )(pm3, pm3, pm3, plo3, sp_main, sp_main, sp_main, sp_lora, mu_main, mu_main, mu_main, mu_lora, *p_ops, z0)
    return o, jnp.swapaxes(zout, -1, -2)


def _rwkv_step_kernel(pr_ref, pk_ref, pv_ref, plo_ref, spr_ref, spk_ref, spv_ref, splo_ref,
                      mur_ref, muk_ref, muv_ref, mulo_ref, *rest, nseq, dseq):
    prm = rest[:10]
    z0_ref, o_ref, z_ref, tr_s, y_s = rest[10:]
    hd = HEAD_DIM_B
    m = nseq * dseq
    blockdiag = _pair_blockdiag()

    def shifted(x, first, mu):
        prev = jnp.concatenate([first, x[0:m - nseq]], axis=0)
        return x + mu * (prev - x)

    xr = shifted(pr_ref[...], spr_ref[...], mur_ref[...])
    xk = shifted(pk_ref[...], spk_ref[...], muk_ref[...])
    xv = shifted(pv_ref[...], spv_ref[...], muv_ref[...])
    xl = shifted(plo_ref[...], splo_ref[...], mulo_ref[...])
    lw, aneg, bb, kmod, g = _rwkv_pre(xk, xl, prm, blockdiag)
    decay = jnp.exp(lw)
    z_ref[...] = z0_ref[...]
    ys = []
    for t in range(dseq):
        rows = slice(t * nseq, (t + 1) * nseq)
        for q, val in enumerate((aneg, bb, decay, kmod, xr, xv)):
            tr_s[q] = val[rows].T
        for hh in range(2):
            base = hh * hd
            vcol = tr_s[5, base:base + hd, :]

            def sa_body(k, acc, hh=hh, base=base):
                return acc + z_ref[hh, k] * tr_s[0, pl.ds(base + k, 1), :]

            sa = lax.fori_loop(0, hd, sa_body, jnp.zeros((hd, nseq), F32))

            def upd_body(k, y, hh=hh, base=base, sa=sa, vcol=vcol):
                zk = (z_ref[hh, k] * tr_s[2, pl.ds(base + k, 1), :] + sa * tr_s[1, pl.ds(base + k, 1), :]
                      + vcol * tr_s[3, pl.ds(base + k, 1), :])
                z_ref[hh, k] = zk
                return y + zk * tr_s[4, pl.ds(base + k, 1), :]

            y_s[base:base + hd, :] = lax.fori_loop(0, hd, upd_body, jnp.zeros((hd, nseq), F32))
        ys.append(y_s[...].T)
    y = jnp.concatenate(ys, axis=0)
    o_ref[...] = _rwkv_post(y, xr, kmod, xv, g, prm, blockdiag).astype(o_ref.dtype)


def _rwkv_steps(pm2, shift_prev, wkv_prev, wp, *, nseq, dseq):
    plo2, lblk = pm2, wp["lora_blk"]
    m = pm2.shape[0]
    d_att, d_rwkv, lora, lora_p = wp["d_att"], wp["d_rwkv"], wp["lora"], wp["lora_p"]
    nh = d_rwkv // HEAD_DIM_B
    cb = 3 * d_att // LANE
    rb = d_rwkv // LANE
    sp_main = shift_prev[:, :3 * d_rwkv]
    sp_lora = jnp.pad(shift_prev[:, 3 * d_rwkv:], ((0, 0), (0, lora_p - lora)))
    mu_main, mu_lora = _mu_split(wp)
    z0 = jnp.transpose(wkv_prev, (1, 3, 2, 0))
    p_ops, p_specs = _rwkv_param_specs(wp, lambda h: (0, h))
    blk = lambda rows, off: pl.BlockSpec((rows, LANE), lambda h: (0, off + h))
    zspec = pl.BlockSpec((2, HEAD_DIM_B, HEAD_DIM_B, nseq), lambda h: (h, 0, 0, 0))
    o, zout = pl.pallas_call(
        functools.partial(_rwkv_step_kernel, nseq=nseq, dseq=dseq),
        grid=(rb,),
        in_specs=[
            blk(m, cb), blk(m, cb + rb), blk(m, cb + 2 * rb), pl.BlockSpec((m, lora_p), lambda h: (0, lblk)),
            blk(nseq, 0), blk(nseq, rb), blk(nseq, 2 * rb), pl.BlockSpec((nseq, lora_p), lambda h: (0, 0)),
            blk(1, 0), blk(1, rb), blk(1, 2 * rb), pl.BlockSpec((1, lora_p), lambda h: (0, 0)),
            *p_specs, zspec,
        ],
        out_specs=[blk(m, 0), zspec],
        out_shape=[jax.ShapeDtypeStruct((m, d_rwkv), BF16),
                   jax.ShapeDtypeStruct((nh, HEAD_DIM_B, HEAD_DIM_B, nseq), F32)],
        scratch_shapes=[pltpu.VMEM((6, LANE, nseq), F32), pltpu.VMEM((LANE, nseq), F32)],
        compiler_params=_cparams(("parallel",)),
        name="rwkv7_steps",
    )(pm2, pm2, pm2, plo2, sp_main, sp_main, sp_main, sp_lora, mu_main, mu_main, mu_main, mu_lora, *p_ops, z0)
    return o, jnp.transpose(zout, (3, 0, 2, 1))


ATT_SUB = DILATION_PAIRS[0][0] // DILATION_PAIRS[0][1]
assert all(w // d == ATT_SUB for w, d in DILATION_PAIRS)
ATT_UNITS = 4


def _rel_bucket(dist):
    max_exact = NUM_BUCKETS // 2
    d_f = jnp.maximum(dist, 1).astype(F32)
    large = max_exact + (jnp.log(d_f / max_exact) / math.log(REL_MAX_DIST / max_exact)
                         * (NUM_BUCKETS - max_exact)).astype(jnp.int32)
    large = jnp.minimum(large, NUM_BUCKETS - 1)
    return jnp.where(dist < max_exact, dist, large)


def _bias_rows(rel_bias, dist):
    bucket = _rel_bucket(dist)[..., None]
    out = jnp.zeros(dist.shape + (rel_bias.shape[1],), F32)
    for b in range(NUM_BUCKETS):
        out = jnp.where(bucket == b, rel_bias[b].astype(F32), out)
    return out


def _prompt_bias(rel_bias):
    sub = ATT_SUB
    qi = jnp.arange(sub)[:, None]
    ki = jnp.arange(2 * sub)[None, :]
    dsub = qi + sub - ki
    ok = ((dsub >= 0) & (dsub <= sub))[..., None]
    tabs = []
    for _, dil in DILATION_PAIRS:
        b = _bias_rows(rel_bias, dil * jnp.clip(dsub, 0, sub))
        tabs.append(jnp.transpose(jnp.where(ok, b, NEG_INF), (2, 0, 1)))
    return jnp.stack(tabs)


def _head_rms(x, g):
    return x * lax.rsqrt(jnp.mean(x * x, axis=-1, keepdims=True) + RMS_EPS) * g


def _attn_prompt_kernel(q_ref, k_ref, v_ref, bias_ref, qg_ref, kg_ref, o_ref, kn_ref, vkeep_ref,
                        qs, ks, acc, m_s, l_s, *, t):
    sub = ATT_SUB
    keep = kn_ref.shape[0]
    qs[...] = _head_rms(q_ref[...], qg_ref[...]) * (1.0 / math.sqrt(HEAD_DIM_A))
    ks[...] = _head_rms(k_ref[...], kg_ref[...])
    kn_ref[...] = ks[pl.ds(t - keep, keep), :]
    vkeep_ref[...] = v_ref[pl.ds(t - keep, keep), :]
    m_s[...] = jnp.full(m_s.shape, NEG_INF, F32)
    l_s[...] = jnp.zeros(l_s.shape, F32)
    acc[...] = jnp.zeros(acc.shape, F32)
    for g, (_, dil) in enumerate(DILATION_PAIRS):
        span = dil * sub
        nblk = t // span
        bias_prev = bias_ref[g, :, 0:sub]
        bias_cur = bias_ref[g, :, sub:2 * sub]
        per_r = min(dil, ATT_UNITS)
        per_n = ATT_UNITS // per_r
        for r0 in range(0, dil, per_r):
            def body(it, _, dil=dil, span=span, r0=r0, per_r=per_r, per_n=per_n,
                     bias_prev=bias_prev, bias_cur=bias_cur):
                units = []
                for dr in range(per_r):
                    for dn in range(per_n):
                        n = it * per_n + dn
                        start = r0 + dr + span * n
                        startp = r0 + dr + span * jnp.maximum(n - 1, 0)
                        if dil == 1:
                            units.append((n, pl.ds(start, sub), pl.ds(startp, sub)))
                        else:
                            units.append((n, pl.ds(start, sub, stride=dil), pl.ds(startp, sub, stride=dil)))
                qb = [qs[rows, :].astype(BF16) for _, rows, _ in units]
                sc = [_dot_nt(q, ks[rows, :].astype(BF16)) + bias_cur for q, (_, rows, _) in zip(qb, units)]
                sp = [jnp.where(n > 0, _dot_nt(q, ks[rowsp, :].astype(BF16)) + bias_prev, NEG_INF)
                      for q, (n, _, rowsp) in zip(qb, units)]
                m_old = [m_s[rows, :] for _, rows, _ in units]
                m_new = [jnp.maximum(mo, jnp.maximum(jnp.max(a, axis=-1, keepdims=True),
                                                     jnp.max(b, axis=-1, keepdims=True)))
                         for mo, a, b in zip(m_old, sc, sp)]
                pc = [jnp.exp(a - mn) for a, mn in zip(sc, m_new)]
                pp = [jnp.exp(b - mn) for b, mn in zip(sp, m_new)]
                pv = [_dot(a.astype(BF16), v_ref[rows, :].astype(BF16))
                      + _dot(b.astype(BF16), v_ref[rowsp, :].astype(BF16))
                      for a, b, (_, rows, rowsp) in zip(pc, pp, units)]
                for i, (_, rows, _) in enumerate(units):
                    alpha = jnp.exp(m_old[i] - m_new[i])
                    l_s[rows, :] = (alpha * l_s[rows, :] + jnp.sum(pc[i], axis=-1, keepdims=True)
                                    + jnp.sum(pp[i], axis=-1, keepdims=True))
                    acc[rows, :] = alpha * acc[rows, :] + pv[i]
                    m_s[rows, :] = m_new[i]
                return 0

            lax.fori_loop(0, nblk // per_n, body, 0)
    o_ref[...] = (acc[...] / l_s[...]).astype(o_ref.dtype)


def _attn_prompt(pm3, rel_bias, qg, kg, d_att):
    n, t, _ = pm3.shape
    nh = d_att // HEAD_DIM_A
    assert t % MAX_WINDOW == 0
    keep = min(MAX_WINDOW, t)
    bias = _prompt_bias(rel_bias)
    blk = lambda off: pl.BlockSpec((None, t, HEAD_DIM_A), lambda i, h: (i, 0, off + h))
    kblk = pl.BlockSpec((None, keep, HEAD_DIM_A), lambda i, h: (i, 0, h))
    vec = pl.BlockSpec((1, HEAD_DIM_A), lambda i, h: (0, 0))
    return pl.pallas_call(
        functools.partial(_attn_prompt_kernel, t=t),
        grid=(n, nh),
        in_specs=[blk(0), blk(nh), blk(2 * nh),
                  pl.BlockSpec((len(DILATION_PAIRS), None, ATT_SUB, 2 * ATT_SUB), lambda i, h: (0, h, 0, 0)),
                  vec, vec],
        out_specs=[blk(0), kblk, kblk],
        out_shape=[jax.ShapeDtypeStruct((n, t, d_att), BF16)] + [jax.ShapeDtypeStruct((n, keep, d_att), F32)] * 2,
        scratch_shapes=[pltpu.VMEM((t, HEAD_DIM_A), F32)] * 5,
        compiler_params=_cparams(("parallel", "parallel")),
        name="attn_prompt",
    )(pm3, pm3, pm3, bias, qg.reshape(1, -1), kg.reshape(1, -1))


ROWS_PAD = 8


def _sample_bias(rel_bias, dseq):
    sub = ATT_SUB
    nh = rel_bias.shape[1]
    rb = lambda dist: _bias_rows(rel_bias, dist)
    c = jnp.arange(sub)
    tabs = [rb(dil * (sub - c)) for _, dil in reversed(DILATION_PAIRS[1:])]
    for s in range(dseq):
        tabs.append(jnp.where((c >= s)[:, None], rb(jnp.clip(sub + s - c, 0, sub)), NEG_INF))
    cache = jnp.broadcast_to(jnp.stack(tabs)[..., None], (len(tabs), sub, nh, LANE))
    s = jnp.arange(dseq)[:, None]
    sp = jnp.arange(dseq)[None, :]
    new = []
    for _, dil in DILATION_PAIRS:
        ok = (sp <= s) if dil == 1 else (sp == s)
        new.append(jnp.where(ok[..., None], rb(dil * jnp.clip(s - sp, 0, sub)), NEG_INF))
    new = jnp.broadcast_to(jnp.stack(new)[..., None], (len(new), dseq, dseq, nh, LANE))
    return cache, new


def _attn_sample_kernel(x_ref, k16_ref, k4_ref, v16_ref, v4_ref, bc_ref, bn_ref, qg_ref, kg_ref,
                        o_ref, kn_ref, *, dseq):
    nh, dh = x_ref.shape[-2], x_ref.shape[-1]
    sub = ATT_SUB
    d4 = DILATION_PAIRS[1][1]
    qn = _head_rms(x_ref[0], qg_ref[...]) * (1.0 / math.sqrt(dh))
    kn = _head_rms(x_ref[1], kg_ref[...])
    vn = x_ref[2]
    kn_ref[...] = kn
    ones = jnp.ones((dh, LANE), BF16)
    tail = sub // d4

    def lane_sum(x):
        keys = x.shape[0]
        return _dot(x.reshape(keys * nh, dh).astype(BF16), ones).reshape(keys, nh, LANE)

    outs = []
    for s in range(dseq):
        q = qn[s]
        k1 = k4_ref[sub - tail:sub].reshape(sub, nh, dh)
        v1 = v4_ref[sub - tail:sub].reshape(sub, nh, dh)
        cache = [(k16_ref[:, s], v16_ref[:, s], bc_ref[0]), (k4_ref[:, s], v4_ref[:, s], bc_ref[1]),
                 (k1, v1, bc_ref[2 + s])]
        logits = [lane_sum(kk * q[None]) + bias for kk, _, bias in cache]
        new = []
        for s2 in range(s + 1):
            ln = jnp.broadcast_to(jnp.sum(q * kn[s2], axis=-1, keepdims=True), (nh, LANE))
            new += [ln + bn_ref[g, s, s2] for g in range(len(DILATION_PAIRS))]
        m = functools.reduce(jnp.maximum, [jnp.max(x, axis=0) for x in logits] + new)
        p = [jnp.exp(x - m[None]) for x in logits]
        pn = [jnp.exp(x - m) for x in new]
        denom = functools.reduce(jnp.add, [jnp.sum(x, axis=0) for x in p] + pn)
        o = functools.reduce(jnp.add, [jnp.sum(pi * vv, axis=0) for pi, (_, vv, _) in zip(p, cache)])
        for s2 in range(s + 1):
            w = functools.reduce(jnp.add, pn[s2 * len(DILATION_PAIRS):(s2 + 1) * len(DILATION_PAIRS)])
            o = o + w * vn[s2]
        outs.append(o / denom)
    outs += [jnp.zeros((nh, dh), F32)] * (ROWS_PAD - dseq)
    o_ref[...] = jnp.stack(outs)


def _attn_sample(x5, cache_k, cache_v, rel_bias, qg, kg, dseq):
    n, wb, nh, dh = cache_k.shape
    d4, d16 = DILATION_PAIRS[1][1], DILATION_PAIRS[2][1]
    assert wb == MAX_WINDOW == d16 * ATT_SUB and dseq <= d4 and DILATION_PAIRS[0][1] == 1 and dh == LANE
    bias_c, bias_n = _sample_bias(rel_bias, dseq)
    view = lambda c, d: c.reshape(n, wb // d, d, nh, dh)
    s16 = pl.BlockSpec((None, ATT_SUB, dseq, nh, dh), lambda i: (i, 0, 0, 0, 0))
    s4 = pl.BlockSpec((None, ATT_SUB, d4, nh, dh), lambda i: (i, wb // d4 // ATT_SUB - 1, 0, 0, 0))
    row_spec = pl.BlockSpec((None, ROWS_PAD, nh, dh), lambda i: (i, 0, 0, 0))
    vec = pl.BlockSpec((1, dh), lambda i: (0, 0))
    const = lambda a: pl.BlockSpec(a.shape, lambda i: (0,) * a.ndim, pipeline_mode=pl.Buffered(1))
    return pl.pallas_call(
        functools.partial(_attn_sample_kernel, dseq=dseq),
        grid=(n,),
        in_specs=[pl.BlockSpec((None, 3, ROWS_PAD, nh, dh), lambda i: (i, 0, 0, 0, 0)), s16, s4, s16, s4,
                  const(bias_c), const(bias_n), vec, vec],
        out_specs=[row_spec, row_spec],
        out_shape=[jax.ShapeDtypeStruct((n, ROWS_PAD, nh, dh), F32)] * 2,
        compiler_params=_cparams(("parallel",)),
        name="attn_sample",
    )(x5, view(cache_k, d16), view(cache_k, d4), view(cache_v, d16), view(cache_v, d4), bias_c, bias_n,
      qg.reshape(1, -1), kg.reshape(1, -1))


def _prep_weights(lp):
    (g_mix, w_in, q_norm_g, k_norm_g, mu_shift, w0, w2, a0, a2, g2, k_k, k_a, r_k, ln_x_w, ln_x_b, w_out,
     g_ffn, w_gate, w_up, conv_w, conv_b, w_down, g_ple, w_ple, w_ple_gate) = lp
    d_model = w_in.shape[0]
    d_rwkv = w0.shape[0]
    d_att = w_out.shape[0] - d_rwkv
    main = 3 * d_att + 3 * d_rwkv
    lora = w_in.shape[1] - main
    lora_p = _round_up(lora, LANE)
    d_ff = w_gate.shape[1]
    dffp = _round_up(d_ff, FF_ALIGN)
    assert main % lora_p == 0 and main % PROJ_TN == 0
    padc = lambda w, n: jnp.pad(w, ((0, 0), (0, n - w.shape[1])))
    wp = dict(
        d_model=d_model, d_att=d_att, d_rwkv=d_rwkv, lora=lora, lora_p=lora_p, lora_blk=main // lora_p,
        d_ff=d_ff, dffp=dffp,
        g_mix=g_mix, g_ffn=g_ffn, g_ple=g_ple, q_norm_g=q_norm_g, k_norm_g=k_norm_g,
        mu_shift=mu_shift, w0=w0, w2=w2, a0=a0, a2=a2, g2=g2, k_k=k_k, k_a=k_a, r_k=r_k,
        ln_x_w=ln_x_w, ln_x_b=ln_x_b,
        conv_w=padc(conv_w, dffp), conv_b=jnp.pad(conv_b, (0, dffp - d_ff)), w_ple=w_ple.astype(BF16),
    )
    big = dict(w_in=w_in, w_out=w_out, w_gate=w_gate, w_up=w_up, w_down=w_down, w_ple_gate=w_ple_gate)
    return wp, big


PROJ_TN = 512


def _layer(x2d, p2d, wp, big, mixer, *, tm, tf, tk_down, shift, conv_prev, tiles_per_group):
    d_model, dffp = wp["d_model"], wp["dffp"]
    emit = big["w_in"].dtype != BF16
    w16 = dict(big)

    def mm(name, a, key, **kw):
        res = _matmul(a, big[key], tm=tm, name=name, **kw)
        if emit:
            res, w16[key] = res
        return res

    xn = _rmsnorm(x2d, wp["g_mix"], min(tm, 256))
    proj = mm("in_proj", xn, "w_in", tn=PROJ_TN, tk=d_model)
    mix, aux = mixer(proj)
    h1 = mm("out_proj", mix, "w_out", tn=PROJ_TN, tk=mix.shape[1], mode="residual", extras=(x2d,))
    hn = _rmsnorm(h1, wp["g_ffn"], min(tm, 256))
    cprev = jnp.pad(conv_prev, ((0, 0), (0, 0), (0, dffp - conv_prev.shape[-1])))
    res = _ffn_act(hn, big["w_gate"], big["w_up"], wp["conv_w"], wp["conv_b"], cprev,
                   tm=tm, tf=tf, shift=shift, tiles_per_group=tiles_per_group)
    act, conv_tail = res[0], res[1]
    if emit:
        w16["w_gate"], w16["w_up"] = res[2], res[3]
    h2 = mm("ffn_down", act, "w_down", tn=min(1024, d_model), tk=tk_down, mode="residual", extras=(h1,))
    hn2 = _rmsnorm(h2, wp["g_ple"], min(tm, 256))
    y = mm("ple_gate", hn2, "w_ple_gate", tn=PROJ_TN, tk=d_model, mode="ple",
           extras=(h2, p2d.astype(BF16), wp["w_ple"]))
    conv_tail = conv_tail[tiles_per_group - 1::tiles_per_group, :, :wp["d_ff"]]
    return y, conv_tail, aux, w16


def kernel(x_prompt, x_sample, cache_k, cache_v, state_shift, state_wkv, state_conv, p_prompt, p_sample, rel_bias, g_mix, w_in, q_norm_g, k_norm_g, mu_shift, w0, w2, a0, a2, g2, k_k, k_a, r_k, ln_x_w, ln_x_b, w_out, g_ffn, w_gate, w_up, conv_w, conv_b, w_down, g_ple, w_ple, w_ple_gate):
    depth = g_mix.shape[0]
    nbp, seq, d_model = x_prompt.shape
    nbs, dseq, _ = x_sample.shape
    hp = x_prompt.reshape(nbp * seq, d_model)
    hs = jnp.swapaxes(x_sample, 0, 1).reshape(dseq * nbs, d_model)
    outs_p, outs_s = [], []
    for i in range(depth):
        lp = (g_mix[i], w_in[i], q_norm_g[i], k_norm_g[i], mu_shift[i], w0[i], w2[i], a0[i], a2[i], g2[i],
              k_k[i], k_a[i], r_k[i], ln_x_w[i], ln_x_b[i], w_out[i], g_ffn[i], w_gate[i], w_up[i],
              conv_w[i], conv_b[i], w_down[i], g_ple[i], w_ple[i], w_ple_gate[i])
        wp, big = _prep_weights(lp)
        d_att, d_rwkv, lora, d_ff, dffp = wp["d_att"], wp["d_rwkv"], wp["lora"], wp["d_ff"], wp["dffp"]
        nha = d_att // HEAD_DIM_A
        nhb = d_rwkv // HEAD_DIM_B
        qg, kg = q_norm_g[i], k_norm_g[i]
        state_cols = slice(3 * d_att, 3 * d_att + 3 * d_rwkv + lora)

        def mixer_prompt(proj, shift_prev, wkv_prev):
            pm = proj.reshape(nbp, seq, -1)
            o_att, k_keep, v_keep = _attn_prompt(pm, rel_bias, qg, kg, d_att)
            o_rwkv, wkv_new = _rwkv_chunked(pm, shift_prev, wkv_prev, wp, tb=min(seq, 256),
                                            nb=2 if nbp % 2 == 0 else 1)
            heads = lambda z: z.reshape(nbp, -1, nha, HEAD_DIM_A)
            mix = jnp.concatenate([o_att, o_rwkv], axis=-1).reshape(nbp * seq, -1)
            return mix, (heads(k_keep), heads(v_keep), pm[:, -1, state_cols], wkv_new)

        def mixer_sample(proj, ck, cv, shift_prev, wkv_prev):
            pm = proj.reshape(dseq, nbs, -1)
            x5 = jnp.transpose(pm[..., :3 * d_att].reshape(dseq, nbs, 3, nha, HEAD_DIM_A), (1, 2, 0, 3, 4))
            x5 = jnp.pad(x5, ((0, 0), (0, 0), (0, ROWS_PAD - dseq), (0, 0), (0, 0)))
            o_att, kn = _attn_sample(x5, ck, cv, rel_bias, qg, kg, dseq)
            o_att = jnp.swapaxes(o_att[:, :dseq], 0, 1).reshape(dseq * nbs, d_att).astype(BF16)
            o_rwkv, wkv_new = _rwkv_steps(proj, shift_prev, wkv_prev, wp, nseq=nbs, dseq=dseq)
            mix = jnp.concatenate([o_att, o_rwkv], axis=-1)
            v_new = jnp.swapaxes(pm[..., 2 * d_att:3 * d_att], 0, 1).reshape(nbs, dseq, nha, HEAD_DIM_A)
            return mix, (kn[:, :dseq], v_new, pm[-1, :, state_cols], wkv_new)

        mix_s = functools.partial(mixer_sample, ck=cache_k[i], cv=cache_v[i],
                                  shift_prev=state_shift[i], wkv_prev=state_wkv[i])
        conv_prev_s = jnp.swapaxes(state_conv[i], 0, 1).reshape(1, (CONV_WIDTH - 1) * nbs, d_ff)
        p_s = jnp.swapaxes(p_sample[i], 0, 1).reshape(dseq * nbs, -1)
        hs, conv_s, aux_s, w16 = _layer(hs, p_s, wp, big, mix_s, tm=dseq * nbs, tf=FF_ALIGN // 2,
                                        tk_down=FF_ALIGN, shift=nbs, conv_prev=conv_prev_s, tiles_per_group=1)
        conv_s = jnp.swapaxes(conv_s.reshape(CONV_WIDTH - 1, nbs, d_ff), 0, 1)
        outs_s.append((*aux_s, conv_s))

        tm_p = min(1024, seq)
        zero_shift = jnp.zeros((nbp, 3 * d_rwkv + lora), F32)
        zero_wkv = jnp.zeros((nbp, nhb, HEAD_DIM_B, HEAD_DIM_B), F32)
        zero_conv = jnp.zeros((nbp, CONV_WIDTH - 1, d_ff), F32)
        mix_p = functools.partial(mixer_prompt, shift_prev=zero_shift, wkv_prev=zero_wkv)
        tk_down_p = dffp // 4 if dffp % (4 * LANE) == 0 else FF_ALIGN
        hp, conv_p, aux_p, _ = _layer(hp, p_prompt[i].reshape(nbp * seq, -1), wp, w16, mix_p, tm=tm_p, tf=FF_ALIGN,
                                      tk_down=tk_down_p, shift=1, conv_prev=zero_conv,
                                      tiles_per_group=seq // tm_p)
        outs_p.append((*aux_p, conv_p))

    y_p = hp.reshape(nbp, seq, d_model)
    y_s = jnp.swapaxes(hs.reshape(dseq, nbs, d_model), 0, 1)
    st = lambda outs, idx: jnp.stack([o[idx] for o in outs])
    return (y_p, y_s, st(outs_p, 0), st(outs_p, 1), st(outs_p, 2), st(outs_p, 3), st(outs_p, 4),
            st(outs_s, 0), st(outs_s, 1), st(outs_s, 2), st(outs_s, 3), st(outs_s, 4))
```

```python
import functools
import math

import jax
import jax.numpy as jnp
from jax import lax
from jax.experimental import pallas as pl
from jax.experimental.pallas import tpu as pltpu

F32 = jnp.float32
BF16 = jnp.bfloat16

HEAD_DIM_A = 128
HEAD_DIM_B = 64
DILATION_PAIRS = ((128, 1), (512, 4), (2048, 16))
MAX_WINDOW = max(w for w, _ in DILATION_PAIRS)
NUM_BUCKETS = 32
REL_MAX_DIST = MAX_WINDOW
DECAY_LORA = 128
AAA_LORA = 128
CONV_WIDTH = 3
RMS_EPS = 1e-6
GN_EPS = 64e-5
NEG_INF = -1e30

LANE = 128
FF_ALIGN = 512
VMEM_LIMIT = 56 * 1024 * 1024
_HI = lax.Precision.HIGHEST


def _cparams(sem):
    return pltpu.CompilerParams(dimension_semantics=sem, vmem_limit_bytes=VMEM_LIMIT)


def _round_up(x, m):
    return -(-x // m) * m


def _dot(a, b, prec=None):
    return jnp.dot(a, b, preferred_element_type=F32, precision=prec)


def _dot_nt(a, b, prec=None):
    return lax.dot_general(a, b, (((1,), (1,)), ((), ())), preferred_element_type=F32, precision=prec)


def _bdot(a, b):
    return _dot(a.astype(BF16), b.astype(BF16))


def _rms_kernel(x_ref, g_ref, o_ref):
    x = x_ref[...]
    ms = jnp.mean(x * x, axis=-1, keepdims=True)
    o_ref[...] = (x * lax.rsqrt(ms + RMS_EPS) * g_ref[...]).astype(o_ref.dtype)


def _rmsnorm(x, g, tm):
    m, d = x.shape
    return pl.pallas_call(
        _rms_kernel,
        grid=(m // tm,),
        in_specs=[pl.BlockSpec((tm, d), lambda i: (i, 0)), pl.BlockSpec((1, d), lambda i: (0, 0))],
        out_specs=pl.BlockSpec((tm, d), lambda i: (i, 0)),
        out_shape=jax.ShapeDtypeStruct((m, d), BF16),
        compiler_params=_cparams(("parallel",)),
        name="rmsnorm",
    )(x, g.reshape(1, d))


_N_EXTRA = {"plain": 0, "residual": 1, "ple": 3}


def _cast_tile(b, row0, col0, k_valid, n_valid):
    b16 = b.astype(BF16)
    if k_valid is not None:
        r = row0 + lax.broadcasted_iota(jnp.int32, (b.shape[0], 1), 0)
        b16 = jnp.where(r < k_valid, b16, jnp.zeros_like(b16))
    if n_valid is not None:
        c = col0 + lax.broadcasted_iota(jnp.int32, (1, b.shape[1]), 1)
        b16 = jnp.where(c < n_valid, b16, jnp.zeros_like(b16))
    return b16


def _mm_kernel(*refs, nk, mode, emit, k_valid, n_valid, b_t):
    a_ref, b_ref = refs[0], refs[1]
    n_extra = _N_EXTRA[mode]
    o_ref = refs[2 + n_extra]
    acc_ref = refs[-1] if nk > 1 else None

    def epilogue(acc):
        if mode == "plain":
            return acc
        if mode == "residual":
            return refs[2][...] + acc
        h_ref, p_ref, wp_ref = refs[2], refs[3], refs[4]
        ple = _dot(p_ref[...], wp_ref[...])
        return h_ref[...] + ple * jax.nn.sigmoid(acc)

    b = b_ref[...]
    if emit and b_t:
        tn, tk = b.shape
        b = _cast_tile(b, pl.program_id(1) * tn, pl.program_id(2) * tk, n_valid, k_valid)
    elif emit:
        tk, tn = b.shape
        b = _cast_tile(b, pl.program_id(2) * tk, pl.program_id(1) * tn, k_valid, n_valid)
    if emit:
        refs[3 + n_extra][...] = b
    part = _dot_nt(a_ref[...], b) if b_t else _dot(a_ref[...], b)
    if nk == 1:
        o_ref[...] = epilogue(part).astype(o_ref.dtype)
        return
    k = pl.program_id(2)

    @pl.when(k == 0)
    def _():
        acc_ref[...] = part

    @pl.when(k > 0)
    def _():
        acc_ref[...] += part

    @pl.when(k == nk - 1)
    def _():
        o_ref[...] = epilogue(acc_ref[...]).astype(o_ref.dtype)


def _matmul(a, b, *, tm, tn, tk, mode="plain", extras=(), b_t=False, name="matmul"):
    m, kp = a.shape
    emit = b.dtype != BF16
    kb, nb_ = (b.shape[1], b.shape[0]) if b_t else b.shape
    np_ = _round_up(nb_, tn)
    assert kp % tk == 0 and m % tm == 0 and (emit or (kb, nb_) == (kp, np_)) and not (emit and m != tm)
    nk = kp // tk
    k_valid = kb if emit and kb != kp else None
    n_valid = nb_ if emit and nb_ != np_ else None
    last_k, last_j = pl.cdiv(kb, tk) - 1, pl.cdiv(nb_, tn) - 1
    if b_t:
        b_spec = pl.BlockSpec((tn, tk), lambda i, j, k: (jnp.minimum(j, last_j), jnp.minimum(k, last_k)))
        w16_spec, w16_shape = pl.BlockSpec((tn, tk), lambda i, j, k: (j, k)), (np_, kp)
    else:
        b_spec = pl.BlockSpec((tk, tn), lambda i, j, k: (jnp.minimum(k, last_k), jnp.minimum(j, last_j)))
        w16_spec, w16_shape = pl.BlockSpec((tk, tn), lambda i, j, k: (k, j)), (kp, np_)
    in_specs = [pl.BlockSpec((tm, tk), lambda i, j, k: (i, k)), b_spec]
    if mode == "residual":
        in_specs.append(pl.BlockSpec((tm, tn), lambda i, j, k: (i, j)))
    elif mode == "ple":
        pdim = extras[1].shape[1]
        in_specs += [pl.BlockSpec((tm, tn), lambda i, j, k: (i, j)),
                     pl.BlockSpec((tm, pdim), lambda i, j, k: (i, 0)),
                     pl.BlockSpec((pdim, tn), lambda i, j, k: (0, j))]
    out_specs = [pl.BlockSpec((tm, tn), lambda i, j, k: (i, j))]
    out_shape = [jax.ShapeDtypeStruct((m, np_), F32)]
    if emit:
        out_specs.append(w16_spec)
        out_shape.append(jax.ShapeDtypeStruct(w16_shape, BF16))
    scratch = [pltpu.VMEM((tm, tn), F32)] if nk > 1 else []
    res = pl.pallas_call(
        functools.partial(_mm_kernel, nk=nk, mode=mode, emit=emit, k_valid=k_valid, n_valid=n_valid, b_t=b_t),
        grid=(m // tm, np_ // tn, nk),
        in_specs=in_specs,
        out_specs=out_specs,
        out_shape=out_shape,
        scratch_shapes=scratch,
        compiler_params=_cparams(("parallel", "parallel", "arbitrary")),
        name=name,
    )(a, b, *extras)
    return tuple(res) if emit else res[0]


FFN_SUB = 256


def _ffn_kernel(a_ref, wg_ref, wu_ref, cw_ref, cb_ref, prev_ref, act_ref, tail_ref, *rest,
                tm, tf, shift, tiles_per_group, emit, n_valid):
    i = pl.program_id(0)
    j = pl.program_id(1)
    hist = (CONV_WIDTH - 1) * shift
    pad = _round_up(hist, 8)
    ext_ref, carry_ref = rest[-2], rest[-1]
    a = a_ref[...]
    sw = min(tf, FFN_SUB)
    subs = [slice(s, s + sw) for s in range(0, tf, sw)]

    def weight(w_ref, w16_ref, cols):
        w = w_ref[:, cols]
        if emit:
            w = _cast_tile(w, 0, j * tf + cols.start, None, n_valid)
            w16_ref[:, cols] = w
        return w

    @pl.when(i % tiles_per_group == 0)
    def _():
        ext_ref[pl.ds(pad - hist, hist), :] = prev_ref[0]

    @pl.when(i % tiles_per_group != 0)
    def _():
        ext_ref[pl.ds(pad - hist, hist), :] = carry_ref[j]

    for cols in subs:
        u = _dot(a, weight(wg_ref, rest[0], cols))
        up = _dot(a, weight(wu_ref, rest[1], cols))
        ext_ref[pl.ds(pad, tm), cols] = u
        c = (cb_ref[:, cols] + cw_ref[0:1, cols] * ext_ref[pl.ds(pad - 2 * shift, tm), cols]
             + cw_ref[1:2, cols] * ext_ref[pl.ds(pad - shift, tm), cols] + cw_ref[2:3, cols] * u)
        act_ref[:, cols] = (jax.nn.silu(c) * up).astype(act_ref.dtype)
    tail = ext_ref[pl.ds(pad + tm - hist, hist), :]
    carry_ref[j] = tail
    tail_ref[0] = tail


def _ffn_act(xn, wg, wu, conv_w, conv_b, conv_prev, *, tm, tf, shift, tiles_per_group):
    m, d = xn.shape
    dffp = conv_w.shape[1]
    emit = wg.dtype != BF16
    assert dffp % tf == 0 and (emit or wg.shape[1] == dffp) and not (emit and m != tm)
    hist = (CONV_WIDTH - 1) * shift
    nj = dffp // tf
    n_valid = wg.shape[1] if emit and wg.shape[1] != dffp else None
    kern = functools.partial(_ffn_kernel, tm=tm, tf=tf, shift=shift, tiles_per_group=tiles_per_group,
                             emit=emit, n_valid=n_valid)
    last = pl.cdiv(wg.shape[1], tf) - 1
    wspec = pl.BlockSpec((d, tf), lambda i, j: (0, jnp.minimum(j, last)))
    out_specs = [pl.BlockSpec((tm, tf), lambda i, j: (i, j)), pl.BlockSpec((1, hist, tf), lambda i, j: (i, 0, j))]
    out_shape = [jax.ShapeDtypeStruct((m, dffp), BF16), jax.ShapeDtypeStruct((m // tm, hist, dffp), F32)]
    if emit:
        out_specs += [pl.BlockSpec((d, tf), lambda i, j: (0, j))] * 2
        out_shape += [jax.ShapeDtypeStruct((d, dffp), BF16)] * 2
    return pl.pallas_call(
        kern,
        grid=(m // tm, nj),
        in_specs=[
            pl.BlockSpec((tm, d), lambda i, j: (i, 0)), wspec, wspec,
            pl.BlockSpec((CONV_WIDTH, tf), lambda i, j: (0, j)),
            pl.BlockSpec((1, tf), lambda i, j: (0, j)),
            pl.BlockSpec((1, hist, tf), lambda i, j: (i // tiles_per_group, 0, j)),
        ],
        out_specs=out_specs,
        out_shape=out_shape,
        scratch_shapes=[pltpu.VMEM((_round_up(hist, 8) + tm, tf), F32), pltpu.VMEM((nj, hist, tf), F32)],
        compiler_params=_cparams(("arbitrary", "arbitrary")),
        name="ffn_gate_up_conv",
    )(xn, wg, wu, conv_w, conv_b.reshape(1, dffp), conv_prev)


RWKV_CHUNK = 64


def _softplus(z):
    return jnp.maximum(z, 0.0) + jnp.log(1.0 + jnp.exp(-jnp.abs(z)))


def _pair_blockdiag():
    li = lax.broadcasted_iota(jnp.int32, (LANE, LANE), 0)
    lj = lax.broadcasted_iota(jnp.int32, (LANE, LANE), 1)
    return (li // HEAD_DIM_B == lj // HEAD_DIM_B).astype(F32)


def _head_sum(x, blockdiag):
    bd = blockdiag.astype(BF16)
    hi = x.astype(BF16)
    lo = (x - hi.astype(F32)).astype(BF16)
    return _dot(hi, bd) + _dot(lo, bd)


def _rwkv_pre(xk, xl, prm, blockdiag):
    w0_ref, a0_ref, w2_ref, a2_ref, g2_ref, kk_ref, ka_ref = prm[:7]
    wd = xl[:, 0:DECAY_LORA]
    ad = xl[:, DECAY_LORA:DECAY_LORA + AAA_LORA]
    gd = xl[:, DECAY_LORA + AAA_LORA:]
    wl = w0_ref[...] + _dot(jnp.tanh(wd).astype(BF16), w2_ref[...])
    lw = -jnp.exp(-_softplus(-wl) - 0.5)
    a = jax.nn.sigmoid(a0_ref[...] + _dot(ad.astype(BF16), a2_ref[...]))
    g = _dot(jax.nn.sigmoid(gd).astype(BF16), g2_ref[...])
    kk = xk * kk_ref[...]
    kk = kk / jnp.maximum(jnp.sqrt(_head_sum(kk * kk, blockdiag)), 1e-12)
    kmod = xk * (1.0 + (a - 1.0) * ka_ref[...])
    return lw, -kk, kk * a, kmod, g


def _rwkv_post(y, xr, kmod, xv, g, prm, blockdiag):
    rk_ref, lnw_ref, lnb_ref = prm[7:10]
    inv = 1.0 / HEAD_DIM_B
    mean = _head_sum(y, blockdiag) * inv
    d = y - mean
    var = _head_sum(d * d, blockdiag) * inv
    yn = d * lax.rsqrt(var + GN_EPS) * lnw_ref[...] + lnb_ref[...]
    bonus = _head_sum(xr * kmod * rk_ref[...], blockdiag) * xv
    return (yn + bonus) * g


def _rwkv_chunk_kernel(pr_ref, pk_ref, pv_ref, plo_ref, spr_ref, spk_ref, spv_ref, splo_ref,
                       mur_ref, muk_ref, muv_ref, mulo_ref, *rest, tb, nb):
    prm = rest[:10]
    z0_ref, o_ref, zout_ref, z_s, cr_s, ck_s, cv_s, clo_s = rest[10:]
    ti = pl.program_id(2)
    nt = pl.num_programs(2)
    cs = RWKV_CHUNK
    hd = HEAD_DIM_B

    lane = lax.broadcasted_iota(jnp.int32, (1, LANE), 1)
    m0 = (lane < hd).astype(F32)
    m1 = 1.0 - m0
    blockdiag = _pair_blockdiag()
    li = lax.broadcasted_iota(jnp.int32, (LANE, LANE), 0)
    lj = lax.broadcasted_iota(jnp.int32, (LANE, LANE), 1)
    eye = (li == lj).astype(F32)
    same = li // cs == lj // cs
    mask_incl = (same & (li >= lj)).astype(F32)
    mask_strict = (same & (li > lj)).astype(F32)
    ci = lax.broadcasted_iota(jnp.int32, (cs, cs), 0)
    cj = lax.broadcasted_iota(jnp.int32, (cs, cs), 1)
    tril_incl = (ci >= cj).astype(F32)
    row = lax.broadcasted_iota(jnp.int32, (tb, 1), 0)

    @pl.when(ti == 0)
    def _():
        z_s[...] = jnp.zeros(z_s.shape, F32)
        for q in range(nb):
            z_s[q, 0:hd, 0:hd] = z0_ref[q, 0]
            z_s[q, hd:2 * hd, hd:2 * hd] = z0_ref[q, 1]
        cr_s[...] = spr_ref[...]
        ck_s[...] = spk_ref[...]
        cv_s[...] = spv_ref[...]
        clo_s[...] = splo_ref[...]

    def shifted(x, carry_row, mu):
        prev = jnp.where(row == 0, carry_row, pltpu.roll(x, 1, 0))
        return x + mu * (prev - x)

    two = lambda x: jnp.concatenate([x * m0, x * m1], axis=0)
    fold = lambda x: x[0:cs] + x[cs:2 * cs]

    seqs = []
    for q in range(nb):
        pr, pk, pv, plo = pr_ref[q], pk_ref[q], pv_ref[q], plo_ref[q]
        xr = shifted(pr, cr_s[q], mur_ref[...])
        xk = shifted(pk, ck_s[q], muk_ref[...])
        xv = shifted(pv, cv_s[q], muv_ref[...])
        xl = shifted(plo, clo_s[q], mulo_ref[...])
        cr_s[q] = pr[tb - 1:tb]
        ck_s[q] = pk[tb - 1:tb]
        cv_s[q] = pv[tb - 1:tb]
        clo_s[q] = plo[tb - 1:tb]
        lw, aneg, bb, kmod, g = _rwkv_pre(xk, xl, prm, blockdiag)
        seqs.append(dict(xr=xr, xv=xv, lw=lw, aneg=aneg, bb=bb, kmod=kmod, g=g))

    nchunk = tb // cs
    units = []
    for q in range(nb):
        for c in range(nchunk):
            rows = slice(c * cs, (c + 1) * cs)
            units.append({k: v[rows] for k, v in seqs[q].items() if k != "g"})
    tril16 = tril_incl.astype(BF16)
    for u in units:
        lw1 = u["lw"].astype(BF16)
        r1 = u["lw"] - lw1.astype(F32)
        lw2 = r1.astype(BF16)
        lw3 = (r1 - lw2.astype(F32)).astype(BF16)
        u["cum"] = _dot(tril16, lw1) + (_dot(tril16, lw2) + _dot(tril16, lw3))
    for u in units:
        cum = u["cum"]
        tot = cum[cs - 1:cs]
        e_inv = jnp.exp(-cum)
        e_end = jnp.exp(tot - cum)
        u["rt"] = u["xr"] * jnp.exp(cum)
        u["la"] = two(u["aneg"] * jnp.exp(cum - u["lw"]))
        u["mt"] = jnp.concatenate([u["bb"] * e_end, u["kmod"] * e_end], axis=0).T
        u["pc_col"] = jnp.broadcast_to(jnp.exp(tot), (LANE, LANE)).T
        u["gm"] = _dot_nt(jnp.concatenate([u["la"], two(u["rt"])], axis=0).astype(BF16),
                          jnp.concatenate([two(u["bb"] * e_inv), two(u["kmod"] * e_inv)], axis=0).astype(BF16))
    for u in units:
        gm = u.pop("gm")
        u["n_pow"] = gm[0:LANE, 0:LANE] * mask_strict
        u["tinv"] = eye + u["n_pow"]
        u["arb"] = gm[LANE:2 * LANE, 0:LANE] * mask_incl
        u["av"] = _bdot(jnp.concatenate([gm[0:LANE, LANE:2 * LANE] * mask_strict,
                                         gm[LANE:2 * LANE, LANE:2 * LANE] * mask_incl], axis=0), two(u["xv"]))
    for _ in range(int(math.log2(cs)) - 1):
        for u in units:
            u["n_pow"] = _bdot(u["n_pow"], u["n_pow"])
        for u in units:
            u["tinv"] = u["tinv"] + _bdot(u["tinv"], u["n_pow"])
    for u in units:
        wu = _bdot(u["tinv"], jnp.concatenate([u["la"], u["av"][0:LANE]], axis=1))
        u["w"] = fold(wu[:, 0:LANE])
        u["u0"] = fold(wu[:, LANE:2 * LANE])
        u["y0"] = fold(u["av"][LANE:2 * LANE])
    for u in units:
        vc = u["xv"]
        zz = _bdot(u["mt"], jnp.concatenate([jnp.concatenate([u["w"], u["u0"]], axis=1),
                                             jnp.concatenate([jnp.zeros_like(vc), vc], axis=1)], axis=0))
        u["zm"] = zz[:, 0:LANE] * blockdiag
        u["zc"] = zz[:, LANE:2 * LANE] * blockdiag

    zs = [z_s[q] for q in range(nb)]
    ys = [[] for _ in range(nb)]
    for c in range(nchunk):
        yus = [_bdot(jnp.concatenate([units[q * nchunk + c]["rt"], units[q * nchunk + c]["w"]], axis=0), zs[q])
               for q in range(nb)]
        zms = [_bdot(units[q * nchunk + c]["zm"], zs[q]) for q in range(nb)]
        for q in range(nb):
            u = units[q * nchunk + c]
            uu = yus[q][cs:2 * cs] + u["u0"]
            ys[q].append(yus[q][0:cs] + u["y0"] + fold(_bdot(u["arb"], two(uu))))
            zs[q] = u["pc_col"] * zs[q] + zms[q] + u["zc"]
    for q in range(nb):
        z_s[q] = zs[q]
        s = seqs[q]
        y = jnp.concatenate(ys[q], axis=0)
        o_ref[q] = _rwkv_post(y, s["xr"], s["kmod"], s["xv"], s["g"], prm, blockdiag).astype(o_ref.dtype)

    @pl.when(ti == nt - 1)
    def _():
        for q in range(nb):
            zout_ref[q, 0] = z_s[q, 0:hd, 0:hd]
            zout_ref[q, 1] = z_s[q, hd:2 * hd, hd:2 * hd]


def _rwkv_param_specs(wp, imap):
    d_rwkv, lora, lora_p = wp["d_rwkv"], wp["lora"], wp["lora_p"]
    gpad = lora_p - DECAY_LORA - AAA_LORA
    glora = lora - DECAY_LORA - AAA_LORA
    row2 = lambda v: v.reshape(1, d_rwkv)
    g2p = jnp.pad(wp["g2"], ((0, gpad - glora), (0, 0))).astype(BF16)
    vec = pl.BlockSpec((1, LANE), imap)
    ops = [row2(wp["w0"]), row2(wp["a0"]), wp["w2"].astype(BF16), wp["a2"].astype(BF16), g2p,
           row2(wp["k_k"]), row2(wp["k_a"]), row2(wp["r_k"]), row2(wp["ln_x_w"]), row2(wp["ln_x_b"])]
    specs = [vec, vec, pl.BlockSpec((DECAY_LORA, LANE), imap), pl.BlockSpec((AAA_LORA, LANE), imap),
             pl.BlockSpec((gpad, LANE), imap), vec, vec, vec, vec, vec]
    return ops, specs


def _mu_split(wp):
    d_rwkv, lora, lora_p = wp["d_rwkv"], wp["lora"], wp["lora_p"]
    mu = wp["mu_shift"]
    return mu[None, :3 * d_rwkv], jnp.pad(mu[None, 3 * d_rwkv:], ((0, 0), (0, lora_p - lora)))


def _rwkv_chunked(pm3, shift_prev, wkv_prev, wp, *, tb, nb):
    plo3, lblk = pm3, wp["lora_blk"]
    n, t, _ = pm3.shape
    d_att, d_rwkv, lora, lora_p = wp["d_att"], wp["d_rwkv"], wp["lora"], wp["lora_p"]
    nh = d_rwkv // HEAD_DIM_B
    cb = 3 * d_att // LANE
    rb = d_rwkv // LANE
    sp_main = shift_prev[:, None, :3 * d_rwkv]
    sp_lora = jnp.pad(shift_prev[:, None, 3 * d_rwkv:], ((0, 0), (0, 0), (0, lora_p - lora)))
    mu_main, mu_lora = _mu_split(wp)
    z0 = jnp.swapaxes(wkv_prev, -1, -2)
    p_ops, p_specs = _rwkv_param_specs(wp, lambda i, h, j: (0, h))

    blk3 = lambda off: pl.BlockSpec((nb, tb, LANE), lambda i, h, j: (i, j, off + h))
    sp3 = lambda off: pl.BlockSpec((nb, 1, LANE), lambda i, h, j: (i, 0, off + h))
    vec = lambda off: pl.BlockSpec((1, LANE), lambda i, h, j: (0, off + h))
    o, zout = pl.pallas_call(
        functools.partial(_rwkv_chunk_kernel, tb=tb, nb=nb),
        grid=(n // nb, rb, t // tb),
        in_specs=[
            blk3(cb), blk3(cb + rb), blk3(cb + 2 * rb),
            pl.BlockSpec((nb, tb, lora_p), lambda i, h, j: (i, j, lblk)),
            sp3(0), sp3(rb), sp3(2 * rb),
            pl.BlockSpec((nb, 1, lora_p), lambda i, h, j: (i, 0, 0)),
            vec(0), vec(rb), vec(2 * rb), pl.BlockSpec((1, lora_p), lambda i, h, j: (0, 0)),
            *p_specs,
            pl.BlockSpec((nb, 2, HEAD_DIM_B, HEAD_DIM_B), lambda i, h, j: (i, h, 0, 0)),
        ],
        out_specs=[
            pl.BlockSpec((nb, tb, LANE), lambda i, h, j: (i, j, h)),
            pl.BlockSpec((nb, 2, HEAD_DIM_B, HEAD_DIM_B), lambda i, h, j: (i, h, 0, 0)),
        ],
        out_shape=[jax.ShapeDtypeStruct((n, t, d_rwkv), BF16),
                   jax.ShapeDtypeStruct((n, nh, HEAD_DIM_B, HEAD_DIM_B), F32)],
        scratch_shapes=[pltpu.VMEM((nb, LANE, LANE), F32), pltpu.VMEM((nb, 1, LANE), F32),
                        pltpu.VMEM((nb, 1, LANE), F32), pltpu.VMEM((nb, 1, LANE), F32),
                        pltpu.VMEM((nb, 1, lora_p), F32)],
        compiler_params=_cparams(("parallel", "parallel", "arbitrary")),
        name="rwkv7_chunked",
    )(pm3, pm3, pm3, plo3, sp_main, sp_main, sp_main, sp_lora, mu_main, mu_main, mu_main, mu_lora, *p_ops, z0)
    return o, jnp.swapaxes(zout, -1, -2)


def _rwkv_step_kernel(pr_ref, pk_ref, pv_ref, plo_ref, spr_ref, spk_ref, spv_ref, splo_ref,
                      mur_ref, muk_ref, muv_ref, mulo_ref, *rest, nseq, dseq):
    prm = rest[:10]
    z0_ref, o_ref, z_ref, tr_s, y_s = rest[10:]
    hd = HEAD_DIM_B
    m = nseq * dseq
    blockdiag = _pair_blockdiag()

    def shifted(x, first, mu):
        prev = jnp.concatenate([first, x[0:m - nseq]], axis=0)
        return x + mu * (prev - x)

    xr = shifted(pr_ref[...], spr_ref[...], mur_ref[...])
    xk = shifted(pk_ref[...], spk_ref[...], muk_ref[...])
    xv = shifted(pv_ref[...], spv_ref[...], muv_ref[...])
    xl = shifted(plo_ref[...], splo_ref[...], mulo_ref[...])
    lw, aneg, bb, kmod, g = _rwkv_pre(xk, xl, prm, blockdiag)
    decay = jnp.exp(lw)
    z_ref[...] = z0_ref[...]
    ys = []
    for t in range(dseq):
        rows = slice(t * nseq, (t + 1) * nseq)
        for q, val in enumerate((aneg, bb, decay, kmod, xr, xv)):
            tr_s[q] = val[rows].T
        for hh in range(2):
            base = hh * hd
            vcol = tr_s[5, base:base + hd, :]

            def sa_body(k, acc, hh=hh, base=base):
                return acc + z_ref[hh, k] * tr_s[0, pl.ds(base + k, 1), :]

            sa = lax.fori_loop(0, hd, sa_body, jnp.zeros((hd, nseq), F32))

            def upd_body(k, y, hh=hh, base=base, sa=sa, vcol=vcol):
                zk = (z_ref[hh, k] * tr_s[2, pl.ds(base + k, 1), :] + sa * tr_s[1, pl.ds(base + k, 1), :]
                      + vcol * tr_s[3, pl.ds(base + k, 1), :])
                z_ref[hh, k] = zk
                return y + zk * tr_s[4, pl.ds(base + k, 1), :]

            y_s[base:base + hd, :] = lax.fori_loop(0, hd, upd_body, jnp.zeros((hd, nseq), F32))
        ys.append(y_s[...].T)
    y = jnp.concatenate(ys, axis=0)
    o_ref[...] = _rwkv_post(y, xr, kmod, xv, g, prm, blockdiag).astype(o_ref.dtype)


def _rwkv_steps(pm2, shift_prev, wkv_prev, wp, *, nseq, dseq):
    plo2, lblk = pm2, wp["lora_blk"]
    m = pm2.shape[0]
    d_att, d_rwkv, lora, lora_p = wp["d_att"], wp["d_rwkv"], wp["lora"], wp["lora_p"]
    nh = d_rwkv // HEAD_DIM_B
    cb = 3 * d_att // LANE
    rb = d_rwkv // LANE
    sp_main = shift_prev[:, :3 * d_rwkv]
    sp_lora = jnp.pad(shift_prev[:, 3 * d_rwkv:], ((0, 0), (0, lora_p - lora)))
    mu_main, mu_lora = _mu_split(wp)
    z0 = jnp.transpose(wkv_prev, (1, 3, 2, 0))
    p_ops, p_specs = _rwkv_param_specs(wp, lambda h: (0, h))
    blk = lambda rows, off: pl.BlockSpec((rows, LANE), lambda h: (0, off + h))
    zspec = pl.BlockSpec((2, HEAD_DIM_B, HEAD_DIM_B, nseq), lambda h: (h, 0, 0, 0))
    o, zout = pl.pallas_call(
        functools.partial(_rwkv_step_kernel, nseq=nseq, dseq=dseq),
        grid=(rb,),
        in_specs=[
            blk(m, cb), blk(m, cb + rb), blk(m, cb + 2 * rb), pl.BlockSpec((m, lora_p), lambda h: (0, lblk)),
            blk(nseq, 0), blk(nseq, rb), blk(nseq, 2 * rb), pl.BlockSpec((nseq, lora_p), lambda h: (0, 0)),
            blk(1, 0), blk(1, rb), blk(1, 2 * rb), pl.BlockSpec((1, lora_p), lambda h: (0, 0)),
            *p_specs, zspec,
        ],
        out_specs=[blk(m, 0), zspec],
        out_shape=[jax.ShapeDtypeStruct((m, d_rwkv), BF16),
                   jax.ShapeDtypeStruct((nh, HEAD_DIM_B, HEAD_DIM_B, nseq), F32)],
        scratch_shapes=[pltpu.VMEM((6, LANE, nseq), F32), pltpu.VMEM((LANE, nseq), F32)],
        compiler_params=_cparams(("parallel",)),
        name="rwkv7_steps",
    )(pm2, pm2, pm2, plo2, sp_main, sp_main, sp_main, sp_lora, mu_main, mu_main, mu_main, mu_lora, *p_ops, z0)
    return o, jnp.transpose(zout, (3, 0, 2, 1))


ATT_SUB = DILATION_PAIRS[0][0] // DILATION_PAIRS[0][1]
assert all(w // d == ATT_SUB for w, d in DILATION_PAIRS)
ATT_UNITS = 8


def _rel_bucket(dist):
    max_exact = NUM_BUCKETS // 2
    d_f = jnp.maximum(dist, 1).astype(F32)
    large = max_exact + (jnp.log(d_f / max_exact) / math.log(REL_MAX_DIST / max_exact)
                         * (NUM_BUCKETS - max_exact)).astype(jnp.int32)
    large = jnp.minimum(large, NUM_BUCKETS - 1)
    return jnp.where(dist < max_exact, dist, large)


def _bias_rows(rel_bias, dist):
    bucket = _rel_bucket(dist)[..., None]
    out = jnp.zeros(dist.shape + (rel_bias.shape[1],), F32)
    for b in range(NUM_BUCKETS):
        out = jnp.where(bucket == b, rel_bias[b].astype(F32), out)
    return out


def _prompt_bias(rel_bias):
    sub = ATT_SUB
    qi = jnp.arange(sub)[:, None]
    ki = jnp.arange(2 * sub)[None, :]
    dsub = qi + sub - ki
    ok = ((dsub >= 0) & (dsub <= sub))[..., None]
    tabs = []
    for _, dil in DILATION_PAIRS:
        b = _bias_rows(rel_bias, dil * jnp.clip(dsub, 0, sub))
        tabs.append(jnp.transpose(jnp.where(ok, b, NEG_INF), (2, 0, 1)))
    return jnp.stack(tabs)


def _head_rms(x, g):
    return x * lax.rsqrt(jnp.mean(x * x, axis=-1, keepdims=True) + RMS_EPS) * g


def _attn_prompt_kernel(q_ref, k_ref, v_ref, bias_ref, qg_ref, kg_ref, o_ref, kn_ref, vkeep_ref,
                        qs, ks, acc, m_s, l_s, *, t):
    sub = ATT_SUB
    keep = kn_ref.shape[0]
    qs[...] = _head_rms(q_ref[...], qg_ref[...]) * (1.0 / math.sqrt(HEAD_DIM_A))
    ks[...] = _head_rms(k_ref[...], kg_ref[...])
    kn_ref[...] = ks[pl.ds(t - keep, keep), :]
    vkeep_ref[...] = v_ref[pl.ds(t - keep, keep), :]
    m_s[...] = jnp.full(m_s.shape, NEG_INF, F32)
    l_s[...] = jnp.zeros(l_s.shape, F32)
    acc[...] = jnp.zeros(acc.shape, F32)
    for g, (_, dil) in enumerate(DILATION_PAIRS):
        span = dil * sub
        nblk = t // span
        bias_prev = bias_ref[g, :, 0:sub]
        bias_cur = bias_ref[g, :, sub:2 * sub]
        per_r = min(dil, ATT_UNITS)
        per_n = ATT_UNITS // per_r
        for r0 in range(0, dil, per_r):
            def body(it, _, dil=dil, span=span, r0=r0, per_r=per_r, per_n=per_n,
                     bias_prev=bias_prev, bias_cur=bias_cur):
                units = []
                for dr in range(per_r):
                    for dn in range(per_n):
                        n = it * per_n + dn
                        start = r0 + dr + span * n
                        startp = r0 + dr + span * jnp.maximum(n - 1, 0)
                        if dil == 1:
                            units.append((n, pl.ds(start, sub), pl.ds(startp, sub)))
                        else:
                            units.append((n, pl.ds(start, sub, stride=dil), pl.ds(startp, sub, stride=dil)))
                qb = [qs[rows, :].astype(BF16) for _, rows, _ in units]
                sc = [_dot_nt(q, ks[rows, :].astype(BF16)) + bias_cur for q, (_, rows, _) in zip(qb, units)]
                sp = [jnp.where(n > 0, _dot_nt(q, ks[rowsp, :].astype(BF16)) + bias_prev, NEG_INF)
                      for q, (n, _, rowsp) in zip(qb, units)]
                m_old = [m_s[rows, :] for _, rows, _ in units]
                m_new = [jnp.maximum(mo, jnp.maximum(jnp.max(a, axis=-1, keepdims=True),
                                                     jnp.max(b, axis=-1, keepdims=True)))
                         for mo, a, b in zip(m_old, sc, sp)]
                pc = [jnp.exp(a - mn) for a, mn in zip(sc, m_new)]
                pp = [jnp.exp(b - mn) for b, mn in zip(sp, m_new)]
                pv = [_dot(a.astype(BF16), v_ref[rows, :].astype(BF16))
                      + _dot(b.astype(BF16), v_ref[rowsp, :].astype(BF16))
                      for a, b, (_, rows, rowsp) in zip(pc, pp, units)]
                for i, (_, rows, _) in enumerate(units):
                    alpha = jnp.exp(m_old[i] - m_new[i])
                    l_s[rows, :] = (alpha * l_s[rows, :] + jnp.sum(pc[i], axis=-1, keepdims=True)
                                    + jnp.sum(pp[i], axis=-1, keepdims=True))
                    acc[rows, :] = alpha * acc[rows, :] + pv[i]
                    m_s[rows, :] = m_new[i]
                return 0

            lax.fori_loop(0, nblk // per_n, body, 0)
    o_ref[...] = (acc[...] / l_s[...]).astype(o_ref.dtype)


def _attn_prompt(pm3, rel_bias, qg, kg, d_att):
    n, t, _ = pm3.shape
    nh = d_att // HEAD_DIM_A
    assert t % MAX_WINDOW == 0
    keep = min(MAX_WINDOW, t)
    bias = _prompt_bias(rel_bias)
    blk = lambda off: pl.BlockSpec((None, t, HEAD_DIM_A), lambda i, h: (i, 0, off + h))
    kblk = pl.BlockSpec((None, keep, HEAD_DIM_A), lambda i, h: (i, 0, h))
    vec = pl.BlockSpec((1, HEAD_DIM_A), lambda i, h: (0, 0))
    return pl.pallas_call(
        functools.partial(_attn_prompt_kernel, t=t),
        grid=(n, nh),
        in_specs=[blk(0), blk(nh), blk(2 * nh),
                  pl.BlockSpec((len(DILATION_PAIRS), None, ATT_SUB, 2 * ATT_SUB), lambda i, h: (0, h, 0, 0)),
                  vec, vec],
        out_specs=[blk(0), kblk, kblk],
        out_shape=[jax.ShapeDtypeStruct((n, t, d_att), BF16)] + [jax.ShapeDtypeStruct((n, keep, d_att), F32)] * 2,
        scratch_shapes=[pltpu.VMEM((t, HEAD_DIM_A), F32)] * 5,
        compiler_params=_cparams(("parallel", "parallel")),
        name="attn_prompt",
    )(pm3, pm3, pm3, bias, qg.reshape(1, -1), kg.reshape(1, -1))


ROWS_PAD = 8


def _sample_bias(rel_bias, dseq):
    sub = ATT_SUB
    nh = rel_bias.shape[1]
    rb = lambda dist: _bias_rows(rel_bias, dist)
    c = jnp.arange(sub)
    tabs = [rb(dil * (sub - c)) for _, dil in reversed(DILATION_PAIRS[1:])]
    for s in range(dseq):
        tabs.append(jnp.where((c >= s)[:, None], rb(jnp.clip(sub + s - c, 0, sub)), NEG_INF))
    cache = jnp.broadcast_to(jnp.stack(tabs)[..., None], (len(tabs), sub, nh, LANE))
    s = jnp.arange(dseq)[:, None]
    sp = jnp.arange(dseq)[None, :]
    new = []
    for _, dil in DILATION_PAIRS:
        ok = (sp <= s) if dil == 1 else (sp == s)
        new.append(jnp.where(ok[..., None], rb(dil * jnp.clip(s - sp, 0, sub)), NEG_INF))
    new = jnp.broadcast_to(jnp.stack(new)[..., None], (len(new), dseq, dseq, nh, LANE))
    return cache, new


def _attn_sample_kernel(x_ref, k16_ref, k4_ref, v16_ref, v4_ref, bc_ref, bn_ref, qg_ref, kg_ref,
                        o_ref, kn_ref, *, dseq):
    nh, dh = x_ref.shape[-2], x_ref.shape[-1]
    sub = ATT_SUB
    d4 = DILATION_PAIRS[1][1]
    qn = _head_rms(x_ref[0], qg_ref[...]) * (1.0 / math.sqrt(dh))
    kn = _head_rms(x_ref[1], kg_ref[...])
    vn = x_ref[2]
    kn_ref[...] = kn
    ones = jnp.ones((dh, LANE), F32)
    tail = sub // d4

    def lane_sum(x):
        keys = x.shape[0]
        return _dot(x.reshape(keys * nh, dh), ones).reshape(keys, nh, LANE)

    outs = []
    for s in range(dseq):
        q = qn[s]
        k1 = k4_ref[sub - tail:sub].reshape(sub, nh, dh)
        v1 = v4_ref[sub - tail:sub].reshape(sub, nh, dh)
        cache = [(k16_ref[:, s], v16_ref[:, s], bc_ref[0]), (k4_ref[:, s], v4_ref[:, s], bc_ref[1]),
                 (k1, v1, bc_ref[2 + s])]
        logits = [lane_sum(kk * q[None]) + bias for kk, _, bias in cache]
        new = []
        for s2 in range(s + 1):
            ln = jnp.broadcast_to(jnp.sum(q * kn[s2], axis=-1, keepdims=True), (nh, LANE))
            new += [ln + bn_ref[g, s, s2] for g in range(len(DILATION_PAIRS))]
        m = functools.reduce(jnp.maximum, [jnp.max(x, axis=0) for x in logits] + new)
        p = [jnp.exp(x - m[None]) for x in logits]
        pn = [jnp.exp(x - m) for x in new]
        denom = functools.reduce(jnp.add, [jnp.sum(x, axis=0) for x in p] + pn)
        o = functools.reduce(jnp.add, [jnp.sum(pi * vv, axis=0) for pi, (_, vv, _) in zip(p, cache)])
        for s2 in range(s + 1):
            w = functools.reduce(jnp.add, pn[s2 * len(DILATION_PAIRS):(s2 + 1) * len(DILATION_PAIRS)])
            o = o + w * vn[s2]
        outs.append(o / denom)
    outs += [jnp.zeros((nh, dh), F32)] * (ROWS_PAD - dseq)
    o_ref[...] = jnp.stack(outs)


def _attn_sample(x5, cache_k, cache_v, rel_bias, qg, kg, dseq):
    n, wb, nh, dh = cache_k.shape
    d4, d16 = DILATION_PAIRS[1][1], DILATION_PAIRS[2][1]
    assert wb == MAX_WINDOW == d16 * ATT_SUB and dseq <= d4 and DILATION_PAIRS[0][1] == 1 and dh == LANE
    bias_c, bias_n = _sample_bias(rel_bias, dseq)
    view = lambda c, d: c.reshape(n, wb // d, d, nh, dh)
    s16 = pl.BlockSpec((None, ATT_SUB, dseq, nh, dh), lambda i: (i, 0, 0, 0, 0))
    s4 = pl.BlockSpec((None, ATT_SUB, d4, nh, dh), lambda i: (i, wb // d4 // ATT_SUB - 1, 0, 0, 0))
    row_spec = pl.BlockSpec((None, ROWS_PAD, nh, dh), lambda i: (i, 0, 0, 0))
    vec = pl.BlockSpec((1, dh), lambda i: (0, 0))
    const = lambda a: pl.BlockSpec(a.shape, lambda i: (0,) * a.ndim, pipeline_mode=pl.Buffered(1))
    return pl.pallas_call(
        functools.partial(_attn_sample_kernel, dseq=dseq),
        grid=(n,),
        in_specs=[pl.BlockSpec((None, 3, ROWS_PAD, nh, dh), lambda i: (i, 0, 0, 0, 0)), s16, s4, s16, s4,
                  const(bias_c), const(bias_n), vec, vec],
        out_specs=[row_spec, row_spec],
        out_shape=[jax.ShapeDtypeStruct((n, ROWS_PAD, nh, dh), F32)] * 2,
        compiler_params=_cparams(("parallel",)),
        name="attn_sample",
    )(x5, view(cache_k, d16), view(cache_k, d4), view(cache_v, d16), view(cache_v, d4), bias_c, bias_n,
      qg.reshape(1, -1), kg.reshape(1, -1))


def _prep_weights(lp):
    (g_mix, w_in, q_norm_g, k_norm_g, mu_shift, w0, w2, a0, a2, g2, k_k, k_a, r_k, ln_x_w, ln_x_b, w_out,
     g_ffn, w_gate, w_up, conv_w, conv_b, w_down, g_ple, w_ple, w_ple_gate) = lp
    d_model = w_in.shape[0]
    d_rwkv = w0.shape[0]
    d_att = w_out.shape[0] - d_rwkv
    main = 3 * d_att + 3 * d_rwkv
    lora = w_in.shape[1] - main
    lora_p = _round_up(lora, LANE)
    d_ff = w_gate.shape[1]
    dffp = _round_up(d_ff, FF_ALIGN)
    assert main % lora_p == 0 and main % PROJ_TN == 0
    padc = lambda w, n: jnp.pad(w, ((0, 0), (0, n - w.shape[1])))
    wp = dict(
        d_model=d_model, d_att=d_att, d_rwkv=d_rwkv, lora=lora, lora_p=lora_p, lora_blk=main // lora_p,
        d_ff=d_ff, dffp=dffp,
        g_mix=g_mix, g_ffn=g_ffn, g_ple=g_ple, q_norm_g=q_norm_g, k_norm_g=k_norm_g,
        mu_shift=mu_shift, w0=w0, w2=w2, a0=a0, a2=a2, g2=g2, k_k=k_k, k_a=k_a, r_k=r_k,
        ln_x_w=ln_x_w, ln_x_b=ln_x_b,
        conv_w=padc(conv_w, dffp), conv_b=jnp.pad(conv_b, (0, dffp - d_ff)), w_ple=w_ple.astype(BF16),
    )
    big = dict(w_in_t=jnp.swapaxes(w_in, 0, 1), w_out=w_out, w_gate=w_gate, w_up=w_up, w_down=w_down,
               w_ple_gate=w_ple_gate)
    return wp, big


PROJ_TN = 512


def _layer(x2d, p2d, wp, big, mixer, *, tm, tf, tk_down, shift, conv_prev, tiles_per_group):
    d_model, dffp = wp["d_model"], wp["dffp"]
    emit = big["w_in_t"].dtype != BF16
    w16 = dict(big)

    def mm(name, a, key, **kw):
        res = _matmul(a, big[key], tm=tm, name=name, **kw)
        if emit:
            res, w16[key] = res
        return res

    xn = _rmsnorm(x2d, wp["g_mix"], min(tm, 256))
    proj = mm("in_proj", xn, "w_in_t", tn=PROJ_TN, tk=d_model, b_t=True)
    mix, aux = mixer(proj)
    h1 = mm("out_proj", mix, "w_out", tn=PROJ_TN, tk=mix.shape[1], mode="residual", extras=(x2d,))
    hn = _rmsnorm(h1, wp["g_ffn"], min(tm, 256))
    cprev = jnp.pad(conv_prev, ((0, 0), (0, 0), (0, dffp - conv_prev.shape[-1])))
    res = _ffn_act(hn, big["w_gate"], big["w_up"], wp["conv_w"], wp["conv_b"], cprev,
                   tm=tm, tf=tf, shift=shift, tiles_per_group=tiles_per_group)
    act, conv_tail = res[0], res[1]
    if emit:
        w16["w_gate"], w16["w_up"] = res[2], res[3]
    h2 = mm("ffn_down", act, "w_down", tn=min(1024, d_model), tk=tk_down, mode="residual", extras=(h1,))
    hn2 = _rmsnorm(h2, wp["g_ple"], min(tm, 256))
    y = mm("ple_gate", hn2, "w_ple_gate", tn=PROJ_TN, tk=d_model, mode="ple",
           extras=(h2, p2d.astype(BF16), wp["w_ple"]))
    conv_tail = conv_tail[tiles_per_group - 1::tiles_per_group, :, :wp["d_ff"]]
    return y, conv_tail, aux, w16


def kernel(x_prompt, x_sample, cache_k, cache_v, state_shift, state_wkv, state_conv, p_prompt, p_sample, rel_bias, g_mix, w_in, q_norm_g, k_norm_g, mu_shift, w0, w2, a0, a2, g2, k_k, k_a, r_k, ln_x_w, ln_x_b, w_out, g_ffn, w_gate, w_up, conv_w, conv_b, w_down, g_ple, w_ple, w_ple_gate):
    depth = g_mix.shape[0]
    nbp, seq, d_model = x_prompt.shape
    nbs, dseq, _ = x_sample.shape
    hp = x_prompt.reshape(nbp * seq, d_model)
    hs = jnp.swapaxes(x_sample, 0, 1).reshape(dseq * nbs, d_model)
    outs_p, outs_s = [], []
    for i in range(depth):
        lp = (g_mix[i], w_in[i], q_norm_g[i], k_norm_g[i], mu_shift[i], w0[i], w2[i], a0[i], a2[i], g2[i],
              k_k[i], k_a[i], r_k[i], ln_x_w[i], ln_x_b[i], w_out[i], g_ffn[i], w_gate[i], w_up[i],
              conv_w[i], conv_b[i], w_down[i], g_ple[i], w_ple[i], w_ple_gate[i])
        wp, big = _prep_weights(lp)
        d_att, d_rwkv, lora, d_ff, dffp = wp["d_att"], wp["d_rwkv"], wp["lora"], wp["d_ff"], wp["dffp"]
        nha = d_att // HEAD_DIM_A
        nhb = d_rwkv // HEAD_DIM_B
        qg, kg = q_norm_g[i], k_norm_g[i]
        state_cols = slice(3 * d_att, 3 * d_att + 3 * d_rwkv + lora)

        def mixer_prompt(proj, shift_prev, wkv_prev):
            pm = proj.reshape(nbp, seq, -1)
            o_att, k_keep, v_keep = _attn_prompt(pm, rel_bias, qg, kg, d_att)
            o_rwkv, wkv_new = _rwkv_chunked(pm, shift_prev, wkv_prev, wp, tb=min(seq, 512),
                                            nb=2 if nbp % 2 == 0 else 1)
            heads = lambda z: z.reshape(nbp, -1, nha, HEAD_DIM_A)
            mix = jnp.concatenate([o_att, o_rwkv], axis=-1).reshape(nbp * seq, -1)
            return mix, (heads(k_keep), heads(v_keep), pm[:, -1, state_cols], wkv_new)

        def mixer_sample(proj, ck, cv, shift_prev, wkv_prev):
            pm = proj.reshape(dseq, nbs, -1)
            x5 = jnp.transpose(pm[..., :3 * d_att].reshape(dseq, nbs, 3, nha, HEAD_DIM_A), (1, 2, 0, 3, 4))
            x5 = jnp.pad(x5, ((0, 0), (0, 0), (0, ROWS_PAD - dseq), (0, 0), (0, 0)))
            o_att, kn = _attn_sample(x5, ck, cv, rel_bias, qg, kg, dseq)
            o_att = jnp.swapaxes(o_att[:, :dseq], 0, 1).reshape(dseq * nbs, d_att).astype(BF16)
            o_rwkv, wkv_new = _rwkv_steps(proj, shift_prev, wkv_prev, wp, nseq=nbs, dseq=dseq)
            mix = jnp.concatenate([o_att, o_rwkv], axis=-1)
            v_new = jnp.swapaxes(pm[..., 2 * d_att:3 * d_att], 0, 1).reshape(nbs, dseq, nha, HEAD_DIM_A)
            return mix, (kn[:, :dseq], v_new, pm[-1, :, state_cols], wkv_new)

        mix_s = functools.partial(mixer_sample, ck=cache_k[i], cv=cache_v[i],
                                  shift_prev=state_shift[i], wkv_prev=state_wkv[i])
        conv_prev_s = jnp.swapaxes(state_conv[i], 0, 1).reshape(1, (CONV_WIDTH - 1) * nbs, d_ff)
        p_s = jnp.swapaxes(p_sample[i], 0, 1).reshape(dseq * nbs, -1)
        hs, conv_s, aux_s, w16 = _layer(hs, p_s, wp, big, mix_s, tm=dseq * nbs, tf=FF_ALIGN // 2,
                                        tk_down=FF_ALIGN, shift=nbs, conv_prev=conv_prev_s, tiles_per_group=1)
        conv_s = jnp.swapaxes(conv_s.reshape(CONV_WIDTH - 1, nbs, d_ff), 0, 1)
        outs_s.append((*aux_s, conv_s))

        tm_p = min(1024, seq)
        zero_shift = jnp.zeros((nbp, 3 * d_rwkv + lora), F32)
        zero_wkv = jnp.zeros((nbp, nhb, HEAD_DIM_B, HEAD_DIM_B), F32)
        zero_conv = jnp.zeros((nbp, CONV_WIDTH - 1, d_ff), F32)
        mix_p = functools.partial(mixer_prompt, shift_prev=zero_shift, wkv_prev=zero_wkv)
        tk_down_p = dffp // 4 if dffp % (4 * LANE) == 0 else FF_ALIGN
        hp, conv_p, aux_p, _ = _layer(hp, p_prompt[i].reshape(nbp * seq, -1), wp, w16, mix_p, tm=tm_p, tf=FF_ALIGN,
                                      tk_down=tk_down_p, shift=1, conv_prev=zero_conv,
                                      tiles_per_group=seq // tm_p)
        outs_p.append((*aux_p, conv_p))

    y_p = hp.reshape(nbp, seq, d_model)
    y_s = jnp.swapaxes(hs.reshape(dseq, nbs, d_model), 0, 1)
    st = lambda outs, idx: jnp.stack([o[idx] for o in outs])
    return (y_p, y_s, st(outs_p, 0), st(outs_p, 1), st(outs_p, 2), st(outs_p, 3), st(outs_p, 4),
            st(outs_s, 0), st(outs_s, 1), st(outs_s, 2), st(outs_s, 3), st(outs_s, 4))
```

```python
import functools
import math

import jax
import jax.numpy as jnp
from jax import lax
from jax.experimental import pallas as pl
from jax.experimental.pallas import tpu as pltpu

F32 = jnp.float32
BF16 = jnp.bfloat16

HEAD_DIM_A = 128
HEAD_DIM_B = 64
DILATION_PAIRS = ((128, 1), (512, 4), (2048, 16))
MAX_WINDOW = max(w for w, _ in DILATION_PAIRS)
NUM_BUCKETS = 32
REL_MAX_DIST = MAX_WINDOW
DECAY_LORA = 128
AAA_LORA = 128
CONV_WIDTH = 3
RMS_EPS = 1e-6
GN_EPS = 64e-5
NEG_INF = -1e30

LANE = 128
FF_ALIGN = 512
VMEM_LIMIT = 56 * 1024 * 1024
_HI = lax.Precision.HIGHEST


def _cparams(sem):
    return pltpu.CompilerParams(dimension_semantics=sem, vmem_limit_bytes=VMEM_LIMIT)


def _round_up(x, m):
    return -(-x // m) * m


def _dot(a, b, prec=None):
    return jnp.dot(a, b, preferred_element_type=F32, precision=prec)


def _dot_nt(a, b, prec=None):
    return lax.dot_general(a, b, (((1,), (1,)), ((), ())), preferred_element_type=F32, precision=prec)


def _bdot(a, b):
    return _dot(a.astype(BF16), b.astype(BF16))


def _rms_kernel(x_ref, g_ref, o_ref):
    x = x_ref[...]
    ms = jnp.mean(x * x, axis=-1, keepdims=True)
    o_ref[...] = (x * lax.rsqrt(ms + RMS_EPS) * g_ref[...]).astype(o_ref.dtype)


def _rmsnorm(x, g, tm):
    m, d = x.shape
    return pl.pallas_call(
        _rms_kernel,
        grid=(m // tm,),
        in_specs=[pl.BlockSpec((tm, d), lambda i: (i, 0)), pl.BlockSpec((1, d), lambda i: (0, 0))],
        out_specs=pl.BlockSpec((tm, d), lambda i: (i, 0)),
        out_shape=jax.ShapeDtypeStruct((m, d), BF16),
        compiler_params=_cparams(("parallel",)),
        name="rmsnorm",
    )(x, g.reshape(1, d))


_N_EXTRA = {"plain": 0, "residual": 1, "ple": 3}


def _cast_tile(b, row0, col0, k_valid, n_valid):
    b16 = b.astype(BF16)
    if k_valid is not None:
        r = row0 + lax.broadcasted_iota(jnp.int32, (b.shape[0], 1), 0)
        b16 = jnp.where(r < k_valid, b16, jnp.zeros_like(b16))
    if n_valid is not None:
        c = col0 + lax.broadcasted_iota(jnp.int32, (1, b.shape[1]), 1)
        b16 = jnp.where(c < n_valid, b16, jnp.zeros_like(b16))
    return b16


def _mm_kernel(*refs, nk, mode, emit, k_valid, n_valid, b_t):
    a_ref, b_ref = refs[0], refs[1]
    n_extra = _N_EXTRA[mode]
    o_ref = refs[2 + n_extra]
    acc_ref = refs[-1] if nk > 1 else None

    def epilogue(acc):
        if mode == "plain":
            return acc
        if mode == "residual":
            return refs[2][...] + acc
        h_ref, p_ref, wp_ref = refs[2], refs[3], refs[4]
        ple = _dot(p_ref[...], wp_ref[...])
        return h_ref[...] + ple * jax.nn.sigmoid(acc)

    b = b_ref[...]
    if emit and b_t:
        tn, tk = b.shape
        b = _cast_tile(b, pl.program_id(1) * tn, pl.program_id(2) * tk, n_valid, k_valid)
    elif emit:
        tk, tn = b.shape
        b = _cast_tile(b, pl.program_id(2) * tk, pl.program_id(1) * tn, k_valid, n_valid)
    if emit:
        refs[3 + n_extra][...] = b
    part = _dot_nt(a_ref[...], b) if b_t else _dot(a_ref[...], b)
    if nk == 1:
        o_ref[...] = epilogue(part).astype(o_ref.dtype)
        return
    k = pl.program_id(2)

    @pl.when(k == 0)
    def _():
        acc_ref[...] = part

    @pl.when(k > 0)
    def _():
        acc_ref[...] += part

    @pl.when(k == nk - 1)
    def _():
        o_ref[...] = epilogue(acc_ref[...]).astype(o_ref.dtype)


def _matmul(a, b, *, tm, tn, tk, mode="plain", extras=(), b_t=False, name="matmul"):
    m, kp = a.shape
    emit = b.dtype != BF16
    kb, nb_ = (b.shape[1], b.shape[0]) if b_t else b.shape
    np_ = _round_up(nb_, tn)
    assert kp % tk == 0 and m % tm == 0 and (emit or (kb, nb_) == (kp, np_)) and not (emit and m != tm)
    nk = kp // tk
    k_valid = kb if emit and kb != kp else None
    n_valid = nb_ if emit and nb_ != np_ else None
    last_k, last_j = pl.cdiv(kb, tk) - 1, pl.cdiv(nb_, tn) - 1
    if b_t:
        b_spec = pl.BlockSpec((tn, tk), lambda i, j, k: (jnp.minimum(j, last_j), jnp.minimum(k, last_k)))
        w16_spec, w16_shape = pl.BlockSpec((tn, tk), lambda i, j, k: (j, k)), (np_, kp)
    else:
        b_spec = pl.BlockSpec((tk, tn), lambda i, j, k: (jnp.minimum(k, last_k), jnp.minimum(j, last_j)))
        w16_spec, w16_shape = pl.BlockSpec((tk, tn), lambda i, j, k: (k, j)), (kp, np_)
    in_specs = [pl.BlockSpec((tm, tk), lambda i, j, k: (i, k)), b_spec]
    if mode == "residual":
        in_specs.append(pl.BlockSpec((tm, tn), lambda i, j, k: (i, j)))
    elif mode == "ple":
        pdim = extras[1].shape[1]
        in_specs += [pl.BlockSpec((tm, tn), lambda i, j, k: (i, j)),
                     pl.BlockSpec((tm, pdim), lambda i, j, k: (i, 0)),
                     pl.BlockSpec((pdim, tn), lambda i, j, k: (0, j))]
    out_specs = [pl.BlockSpec((tm, tn), lambda i, j, k: (i, j))]
    out_shape = [jax.ShapeDtypeStruct((m, np_), F32)]
    if emit:
        out_specs.append(w16_spec)
        out_shape.append(jax.ShapeDtypeStruct(w16_shape, BF16))
    scratch = [pltpu.VMEM((tm, tn), F32)] if nk > 1 else []
    res = pl.pallas_call(
        functools.partial(_mm_kernel, nk=nk, mode=mode, emit=emit, k_valid=k_valid, n_valid=n_valid, b_t=b_t),
        grid=(m // tm, np_ // tn, nk),
        in_specs=in_specs,
        out_specs=out_specs,
        out_shape=out_shape,
        scratch_shapes=scratch,
        compiler_params=_cparams(("parallel", "parallel", "arbitrary")),
        name=name,
    )(a, b, *extras)
    return tuple(res) if emit else res[0]


FFN_SUB = 256


def _ffn_kernel(a_ref, wg_ref, wu_ref, cw_ref, cb_ref, prev_ref, act_ref, tail_ref, *rest,
                tm, tf, shift, tiles_per_group, emit, n_valid):
    i = pl.program_id(0)
    j = pl.program_id(1)
    hist = (CONV_WIDTH - 1) * shift
    pad = _round_up(hist, 8)
    ext_ref, carry_ref = rest[-2], rest[-1]
    a = a_ref[...]
    sw = min(tf, FFN_SUB)
    subs = [slice(s, s + sw) for s in range(0, tf, sw)]

    def weight(w_ref, w16_ref, cols):
        w = w_ref[:, cols]
        if emit:
            w = _cast_tile(w, 0, j * tf + cols.start, None, n_valid)
            w16_ref[:, cols] = w
        return w

    @pl.when((i == 0) & (j == 0))
    def _():
        ext_ref[...] = jnp.zeros(ext_ref.shape, F32)

    @pl.when(i % tiles_per_group == 0)
    def _():
        ext_ref[pl.ds(pad - hist, hist), :] = prev_ref[0]

    @pl.when(i % tiles_per_group != 0)
    def _():
        ext_ref[pl.ds(pad - hist, hist), :] = carry_ref[j]

    row8 = lax.broadcasted_iota(jnp.int32, (8, 1), 0)

    def delayed(u, k, cols):
        off = k * shift
        if off % 8 == 0:
            return jnp.concatenate([ext_ref[pl.ds(pad - off, off), cols], u[0:tm - off]], axis=0)
        rolled = pltpu.roll(u, off, 0)
        head = jnp.where(row8 < off, ext_ref[pl.ds(pad - off, 8), cols], rolled[0:8])
        return jnp.concatenate([head, rolled[8:]], axis=0)

    for cols in subs:
        u = _dot(a, weight(wg_ref, rest[0], cols))
        up = _dot(a, weight(wu_ref, rest[1], cols))
        c = (cb_ref[:, cols] + cw_ref[0:1, cols] * delayed(u, 2, cols) + cw_ref[1:2, cols] * delayed(u, 1, cols)
             + cw_ref[2:3, cols] * u)
        act_ref[:, cols] = (jax.nn.silu(c) * up).astype(act_ref.dtype)
        ext_ref[pl.ds(pad, pad), cols] = u[tm - pad:tm]
    tail = ext_ref[pl.ds(2 * pad - hist, hist), :]
    carry_ref[j] = tail
    tail_ref[0] = tail


def _ffn_act(xn, wg, wu, conv_w, conv_b, conv_prev, *, tm, tf, shift, tiles_per_group):
    m, d = xn.shape
    dffp = conv_w.shape[1]
    emit = wg.dtype != BF16
    assert dffp % tf == 0 and (emit or wg.shape[1] == dffp) and not (emit and m != tm)
    hist = (CONV_WIDTH - 1) * shift
    nj = dffp // tf
    n_valid = wg.shape[1] if emit and wg.shape[1] != dffp else None
    kern = functools.partial(_ffn_kernel, tm=tm, tf=tf, shift=shift, tiles_per_group=tiles_per_group,
                             emit=emit, n_valid=n_valid)
    last = pl.cdiv(wg.shape[1], tf) - 1
    wspec = pl.BlockSpec((d, tf), lambda i, j: (0, jnp.minimum(j, last)))
    out_specs = [pl.BlockSpec((tm, tf), lambda i, j: (i, j)), pl.BlockSpec((1, hist, tf), lambda i, j: (i, 0, j))]
    out_shape = [jax.ShapeDtypeStruct((m, dffp), BF16), jax.ShapeDtypeStruct((m // tm, hist, dffp), F32)]
    if emit:
        out_specs += [pl.BlockSpec((d, tf), lambda i, j: (0, j))] * 2
        out_shape += [jax.ShapeDtypeStruct((d, dffp), BF16)] * 2
    return pl.pallas_call(
        kern,
        grid=(m // tm, nj),
        in_specs=[
            pl.BlockSpec((tm, d), lambda i, j: (i, 0)), wspec, wspec,
            pl.BlockSpec((CONV_WIDTH, tf), lambda i, j: (0, j)),
            pl.BlockSpec((1, tf), lambda i, j: (0, j)),
            pl.BlockSpec((1, hist, tf), lambda i, j: (i // tiles_per_group, 0, j)),
        ],
        out_specs=out_specs,
        out_shape=out_shape,
        scratch_shapes=[pltpu.VMEM((2 * _round_up(hist, 8), tf), F32), pltpu.VMEM((nj, hist, tf), F32)],
        compiler_params=_cparams(("arbitrary", "arbitrary")),
        name="ffn_gate_up_conv",
    )(xn, wg, wu, conv_w, conv_b.reshape(1, dffp), conv_prev)


RWKV_CHUNK = 64


def _softplus(z):
    return jnp.maximum(z, 0.0) + jnp.log(1.0 + jnp.exp(-jnp.abs(z)))


def _pair_blockdiag():
    li = lax.broadcasted_iota(jnp.int32, (LANE, LANE), 0)
    lj = lax.broadcasted_iota(jnp.int32, (LANE, LANE), 1)
    return (li // HEAD_DIM_B == lj // HEAD_DIM_B).astype(F32)


def _head_sum(x, blockdiag):
    bd = blockdiag.astype(BF16)
    hi = x.astype(BF16)
    lo = (x - hi.astype(F32)).astype(BF16)
    return _dot(hi, bd) + _dot(lo, bd)


def _rwkv_pre(xk, xl, prm, blockdiag):
    w0_ref, a0_ref, w2_ref, a2_ref, g2_ref, kk_ref, ka_ref = prm[:7]
    wd = xl[:, 0:DECAY_LORA]
    ad = xl[:, DECAY_LORA:DECAY_LORA + AAA_LORA]
    gd = xl[:, DECAY_LORA + AAA_LORA:]
    wl = w0_ref[...] + _dot(jnp.tanh(wd).astype(BF16), w2_ref[...])
    lw = -jnp.exp(-_softplus(-wl) - 0.5)
    a = jax.nn.sigmoid(a0_ref[...] + _dot(ad.astype(BF16), a2_ref[...]))
    g = _dot(jax.nn.sigmoid(gd).astype(BF16), g2_ref[...])
    kk = xk * kk_ref[...]
    kk = kk / jnp.maximum(jnp.sqrt(_head_sum(kk * kk, blockdiag)), 1e-12)
    kmod = xk * (1.0 + (a - 1.0) * ka_ref[...])
    return lw, -kk, kk * a, kmod, g


def _rwkv_post(y, xr, kmod, xv, g, prm, blockdiag):
    rk_ref, lnw_ref, lnb_ref = prm[7:10]
    inv = 1.0 / HEAD_DIM_B
    mean = _head_sum(y, blockdiag) * inv
    d = y - mean
    var = _head_sum(d * d, blockdiag) * inv
    yn = d * lax.rsqrt(var + GN_EPS) * lnw_ref[...] + lnb_ref[...]
    bonus = _head_sum(xr * kmod * rk_ref[...], blockdiag) * xv
    return (yn + bonus) * g


def _rwkv_chunk_kernel(pr_ref, pk_ref, pv_ref, plo_ref, spr_ref, spk_ref, spv_ref, splo_ref,
                       mur_ref, muk_ref, muv_ref, mulo_ref, *rest, tb, nb):
    prm = rest[:10]
    z0_ref, o_ref, zout_ref, z_s, cr_s, ck_s, cv_s, clo_s = rest[10:]
    ti = pl.program_id(2)
    nt = pl.num_programs(2)
    cs = RWKV_CHUNK
    hd = HEAD_DIM_B

    lane = lax.broadcasted_iota(jnp.int32, (1, LANE), 1)
    m0 = (lane < hd).astype(F32)
    m1 = 1.0 - m0
    blockdiag = _pair_blockdiag()
    li = lax.broadcasted_iota(jnp.int32, (LANE, LANE), 0)
    lj = lax.broadcasted_iota(jnp.int32, (LANE, LANE), 1)
    eye = (li == lj).astype(F32)
    same = li // cs == lj // cs
    mask_incl = (same & (li >= lj)).astype(F32)
    mask_strict = (same & (li > lj)).astype(F32)
    ci = lax.broadcasted_iota(jnp.int32, (cs, cs), 0)
    cj = lax.broadcasted_iota(jnp.int32, (cs, cs), 1)
    tril_incl = (ci >= cj).astype(F32)
    row = lax.broadcasted_iota(jnp.int32, (tb, 1), 0)

    @pl.when(ti == 0)
    def _():
        z_s[...] = jnp.zeros(z_s.shape, F32)
        for q in range(nb):
            z_s[q, 0:hd, 0:hd] = z0_ref[q, 0]
            z_s[q, hd:2 * hd, hd:2 * hd] = z0_ref[q, 1]
        cr_s[...] = spr_ref[...]
        ck_s[...] = spk_ref[...]
        cv_s[...] = spv_ref[...]
        clo_s[...] = splo_ref[...]

    def shifted(x, carry_row, mu):
        prev = jnp.where(row == 0, carry_row, pltpu.roll(x, 1, 0))
        return x + mu * (prev - x)

    two = lambda x: jnp.concatenate([x * m0, x * m1], axis=0)
    fold = lambda x: x[0:cs] + x[cs:2 * cs]

    seqs = []
    for q in range(nb):
        pr, pk, pv, plo = pr_ref[q], pk_ref[q], pv_ref[q], plo_ref[q]
        xr = shifted(pr, cr_s[q], mur_ref[...])
        xk = shifted(pk, ck_s[q], muk_ref[...])
        xv = shifted(pv, cv_s[q], muv_ref[...])
        xl = shifted(plo, clo_s[q], mulo_ref[...])
        cr_s[q] = pr[tb - 1:tb]
        ck_s[q] = pk[tb - 1:tb]
        cv_s[q] = pv[tb - 1:tb]
        clo_s[q] = plo[tb - 1:tb]
        lw, aneg, bb, kmod, g = _rwkv_pre(xk, xl, prm, blockdiag)
        seqs.append(dict(xr=xr, xv=xv, lw=lw, aneg=aneg, bb=bb, kmod=kmod, g=g))

    nchunk = tb // cs
    units = []
    for q in range(nb):
        for c in range(nchunk):
            rows = slice(c * cs, (c + 1) * cs)
            units.append({k: v[rows] for k, v in seqs[q].items() if k != "g"})
    tril16 = tril_incl.astype(BF16)
    for u in units:
        lw1 = u["lw"].astype(BF16)
        r1 = u["lw"] - lw1.astype(F32)
        lw2 = r1.astype(BF16)
        lw3 = (r1 - lw2.astype(F32)).astype(BF16)
        u["cum"] = _dot(tril16, lw1) + (_dot(tril16, lw2) + _dot(tril16, lw3))
    for u in units:
        cum = u["cum"]
        tot = cum[cs - 1:cs]
        e_inv = jnp.exp(-cum)
        e_end = jnp.exp(tot - cum)
        u["rt"] = u["xr"] * jnp.exp(cum)
        u["la"] = two(u["aneg"] * jnp.exp(cum - u["lw"]))
        u["mt"] = jnp.concatenate([u["bb"] * e_end, u["kmod"] * e_end], axis=0).T
        u["pc_col"] = jnp.broadcast_to(jnp.exp(tot), (LANE, LANE)).T
        u["gm"] = _dot_nt(jnp.concatenate([u["la"], two(u["rt"])], axis=0).astype(BF16),
                          jnp.concatenate([two(u["bb"] * e_inv), two(u["kmod"] * e_inv)], axis=0).astype(BF16))
    for u in units:
        gm = u.pop("gm")
        u["n_pow"] = gm[0:LANE, 0:LANE] * mask_strict
        u["tinv"] = eye + u["n_pow"]
        u["arb"] = gm[LANE:2 * LANE, 0:LANE] * mask_incl
        u["av"] = _bdot(jnp.concatenate([gm[0:LANE, LANE:2 * LANE] * mask_strict,
                                         gm[LANE:2 * LANE, LANE:2 * LANE] * mask_incl], axis=0), two(u["xv"]))
    for _ in range(int(math.log2(cs)) - 1):
        for u in units:
            u["n_pow"] = _bdot(u["n_pow"], u["n_pow"])
        for u in units:
            u["tinv"] = u["tinv"] + _bdot(u["tinv"], u["n_pow"])
    for u in units:
        wu = _bdot(u["tinv"], jnp.concatenate([u["la"], u["av"][0:LANE]], axis=1))
        u["w"] = fold(wu[:, 0:LANE])
        u["u0"] = fold(wu[:, LANE:2 * LANE])
        u["y0"] = fold(u["av"][LANE:2 * LANE])
    for u in units:
        vc = u["xv"]
        zz = _bdot(u["mt"], jnp.concatenate([jnp.concatenate([u["w"], u["u0"]], axis=1),
                                             jnp.concatenate([jnp.zeros_like(vc), vc], axis=1)], axis=0))
        u["zm"] = zz[:, 0:LANE] * blockdiag
        u["zc"] = zz[:, LANE:2 * LANE] * blockdiag

    zs = [z_s[q] for q in range(nb)]
    ys = [[] for _ in range(nb)]
    for c in range(nchunk):
        yus = [_bdot(jnp.concatenate([units[q * nchunk + c]["rt"], units[q * nchunk + c]["w"]], axis=0), zs[q])
               for q in range(nb)]
        zms = [_bdot(units[q * nchunk + c]["zm"], zs[q]) for q in range(nb)]
        for q in range(nb):
            u = units[q * nchunk + c]
            uu = yus[q][cs:2 * cs] + u["u0"]
            ys[q].append(yus[q][0:cs] + u["y0"] + fold(_bdot(u["arb"], two(uu))))
            zs[q] = u["pc_col"] * zs[q] + zms[q] + u["zc"]
    for q in range(nb):
        z_s[q] = zs[q]
        s = seqs[q]
        y = jnp.concatenate(ys[q], axis=0)
        o_ref[q] = _rwkv_post(y, s["xr"], s["kmod"], s["xv"], s["g"], prm, blockdiag).astype(o_ref.dtype)

    @pl.when(ti == nt - 1)
    def _():
        for q in range(nb):
            zout_ref[q, 0] = z_s[q, 0:hd, 0:hd]
            zout_ref[q, 1] = z_s[q, hd:2 * hd, hd:2 * hd]


def _rwkv_param_specs(wp, imap):
    d_rwkv, lora, lora_p = wp["d_rwkv"], wp["lora"], wp["lora_p"]
    gpad = lora_p - DECAY_LORA - AAA_LORA
    glora = lora - DECAY_LORA - AAA_LORA
    row2 = lambda v: v.reshape(1, d_rwkv)
    g2p = jnp.pad(wp["g2"], ((0, gpad - glora), (0, 0))).astype(BF16)
    vec = pl.BlockSpec((1, LANE), imap)
    ops = [row2(wp["w0"]), row2(wp["a0"]), wp["w2"].astype(BF16), wp["a2"].astype(BF16), g2p,
           row2(wp["k_k"]), row2(wp["k_a"]), row2(wp["r_k"]), row2(wp["ln_x_w"]), row2(wp["ln_x_b"])]
    specs = [vec, vec, pl.BlockSpec((DECAY_LORA, LANE), imap), pl.BlockSpec((AAA_LORA, LANE), imap),
             pl.BlockSpec((gpad, LANE), imap), vec, vec, vec, vec, vec]
    return ops, specs


def _mu_split(wp):
    d_rwkv, lora, lora_p = wp["d_rwkv"], wp["lora"], wp["lora_p"]
    mu = wp["mu_shift"]
    return mu[None, :3 * d_rwkv], jnp.pad(mu[None, 3 * d_rwkv:], ((0, 0), (0, lora_p - lora)))


def _rwkv_chunked(pm3, shift_prev, wkv_prev, wp, *, tb, nb):
    plo3, lblk = pm3, wp["lora_blk"]
    n, t, _ = pm3.shape
    d_att, d_rwkv, lora, lora_p = wp["d_att"], wp["d_rwkv"], wp["lora"], wp["lora_p"]
    nh = d_rwkv // HEAD_DIM_B
    cb = 3 * d_att // LANE
    rb = d_rwkv // LANE
    sp_main = shift_prev[:, None, :3 * d_rwkv]
    sp_lora = jnp.pad(shift_prev[:, None, 3 * d_rwkv:], ((0, 0), (0, 0), (0, lora_p - lora)))
    mu_main, mu_lora = _mu_split(wp)
    z0 = jnp.swapaxes(wkv_prev, -1, -2)
    p_ops, p_specs = _rwkv_param_specs(wp, lambda i, h, j: (0, h))

    blk3 = lambda off: pl.BlockSpec((nb, tb, LANE), lambda i, h, j: (i, j, off + h))
    sp3 = lambda off: pl.BlockSpec((nb, 1, LANE), lambda i, h, j: (i, 0, off + h))
    vec = lambda off: pl.BlockSpec((1, LANE), lambda i, h, j: (0, off + h))
    o, zout = pl.pallas_call(
        functools.partial(_rwkv_chunk_kernel, tb=tb, nb=nb),
        grid=(n // nb, rb, t // tb),
        in_specs=[
            blk3(cb), blk3(cb + rb), blk3(cb + 2 * rb),
            pl.BlockSpec((nb, tb, lora_p), lambda i, h, j: (i, j, lblk)),
            sp3(0), sp3(rb), sp3(2 * rb),
            pl.BlockSpec((nb, 1, lora_p), lambda i, h, j: (i, 0, 0)),
            vec(0), vec(rb), vec(2 * rb), pl.BlockSpec((1, lora_p), lambda i, h, j: (0, 0)),
            *p_specs,
            pl.BlockSpec((nb, 2, HEAD_DIM_B, HEAD_DIM_B), lambda i, h, j: (i, h, 0, 0)),
        ],
        out_specs=[
            pl.BlockSpec((nb, tb, LANE), lambda i, h, j: (i, j, h)),
            pl.BlockSpec((nb, 2, HEAD_DIM_B, HEAD_DIM_B), lambda i, h, j: (i, h, 0, 0)),
        ],
        out_shape=[jax.ShapeDtypeStruct((n, t, d_rwkv), BF16),
                   jax.ShapeDtypeStruct((n, nh, HEAD_DIM_B, HEAD_DIM_B), F32)],
        scratch_shapes=[pltpu.VMEM((nb, LANE, LANE), F32), pltpu.VMEM((nb, 1, LANE), F32),
                        pltpu.VMEM((nb, 1, LANE), F32), pltpu.VMEM((nb, 1, LANE), F32),
                        pltpu.VMEM((nb, 1, lora_p), F32)],
        compiler_params=_cparams(("parallel", "parallel", "arbitrary")),
        name="rwkv7_chunked",
    )(pm3, pm3, pm3, plo3, sp_main, sp_main, sp_main, sp_lora, mu_main, mu_main, mu_main, mu_lora, *p_ops, z0)
    return o, jnp.swapaxes(zout, -1, -2)


def _rwkv_step_kernel(pr_ref, pk_ref, pv_ref, plo_ref, spr_ref, spk_ref, spv_ref, splo_ref,
                      mur_ref, muk_ref, muv_ref, mulo_ref, *rest, nseq, dseq):
    prm = rest[:10]
    z0_ref, o_ref, z_ref, tr_s, y_s = rest[10:]
    hd = HEAD_DIM_B
    m = nseq * dseq
    blockdiag = _pair_blockdiag()

    def shifted(x, first, mu):
        prev = jnp.concatenate([first, x[0:m - nseq]], axis=0)
        return x + mu * (prev - x)

    xr = shifted(pr_ref[...], spr_ref[...], mur_ref[...])
    xk = shifted(pk_ref[...], spk_ref[...], muk_ref[...])
    xv = shifted(pv_ref[...], spv_ref[...], muv_ref[...])
    xl = shifted(plo_ref[...], splo_ref[...], mulo_ref[...])
    lw, aneg, bb, kmod, g = _rwkv_pre(xk, xl, prm, blockdiag)
    decay = jnp.exp(lw)
    z_ref[...] = z0_ref[...]
    ys = []
    for t in range(dseq):
        rows = slice(t * nseq, (t + 1) * nseq)
        for q, val in enumerate((aneg, bb, decay, kmod, xr, xv)):
            tr_s[q] = val[rows].T
        for hh in range(2):
            base = hh * hd
            vcol = tr_s[5, base:base + hd, :]

            def sa_body(k, acc, hh=hh, base=base):
                return acc + z_ref[hh, k] * tr_s[0, pl.ds(base + k, 1), :]

            sa = lax.fori_loop(0, hd, sa_body, jnp.zeros((hd, nseq), F32))

            def upd_body(k, y, hh=hh, base=base, sa=sa, vcol=vcol):
                zk = (z_ref[hh, k] * tr_s[2, pl.ds(base + k, 1), :] + sa * tr_s[1, pl.ds(base + k, 1), :]
                      + vcol * tr_s[3, pl.ds(base + k, 1), :])
                z_ref[hh, k] = zk
                return y + zk * tr_s[4, pl.ds(base + k, 1), :]

            y_s[base:base + hd, :] = lax.fori_loop(0, hd, upd_body, jnp.zeros((hd, nseq), F32))
        ys.append(y_s[...].T)
    y = jnp.concatenate(ys, axis=0)
    o_ref[...] = _rwkv_post(y, xr, kmod, xv, g, prm, blockdiag).astype(o_ref.dtype)


def _rwkv_steps(pm2, shift_prev, wkv_prev, wp, *, nseq, dseq):
    plo2, lblk = pm2, wp["lora_blk"]
    m = pm2.shape[0]
    d_att, d_rwkv, lora, lora_p = wp["d_att"], wp["d_rwkv"], wp["lora"], wp["lora_p"]
    nh = d_rwkv // HEAD_DIM_B
    cb = 3 * d_att // LANE
    rb = d_rwkv // LANE
    sp_main = shift_prev[:, :3 * d_rwkv]
    sp_lora = jnp.pad(shift_prev[:, 3 * d_rwkv:], ((0, 0), (0, lora_p - lora)))
    mu_main, mu_lora = _mu_split(wp)
    z0 = jnp.transpose(wkv_prev, (1, 3, 2, 0))
    p_ops, p_specs = _rwkv_param_specs(wp, lambda h: (0, h))
    blk = lambda rows, off: pl.BlockSpec((rows, LANE), lambda h: (0, off + h))
    zspec = pl.BlockSpec((2, HEAD_DIM_B, HEAD_DIM_B, nseq), lambda h: (h, 0, 0, 0))
    o, zout = pl.pallas_call(
        functools.partial(_rwkv_step_kernel, nseq=nseq, dseq=dseq),
        grid=(rb,),
        in_specs=[
            blk(m, cb), blk(m, cb + rb), blk(m, cb + 2 * rb), pl.BlockSpec((m, lora_p), lambda h: (0, lblk)),
            blk(nseq, 0), blk(nseq, rb), blk(nseq, 2 * rb), pl.BlockSpec((nseq, lora_p), lambda h: (0, 0)),
            blk(1, 0), blk(1, rb), blk(1, 2 * rb), pl.BlockSpec((1, lora_p), lambda h: (0, 0)),
            *p_specs, zspec,
        ],
        out_specs=[blk(m, 0), zspec],
        out_shape=[jax.ShapeDtypeStruct((m, d_rwkv), BF16),
                   jax.ShapeDtypeStruct((nh, HEAD_DIM_B, HEAD_DIM_B, nseq), F32)],
        scratch_shapes=[pltpu.VMEM((6, LANE, nseq), F32), pltpu.VMEM((LANE, nseq), F32)],
        compiler_params=_cparams(("parallel",)),
        name="rwkv7_steps",
    )(pm2, pm2, pm2, plo2, sp_main, sp_main, sp_main, sp_lora, mu_main, mu_main, mu_main, mu_lora, *p_ops, z0)
    return o, jnp.transpose(zout, (3, 0, 2, 1))


ATT_SUB = DILATION_PAIRS[0][0] // DILATION_PAIRS[0][1]
assert all(w // d == ATT_SUB for w, d in DILATION_PAIRS)
ATT_UNITS = 8


def _rel_bucket(dist):
    max_exact = NUM_BUCKETS // 2
    d_f = jnp.maximum(dist, 1).astype(F32)
    large = max_exact + (jnp.log(d_f / max_exact) / math.log(REL_MAX_DIST / max_exact)
                         * (NUM_BUCKETS - max_exact)).astype(jnp.int32)
    large = jnp.minimum(large, NUM_BUCKETS - 1)
    return jnp.where(dist < max_exact, dist, large)


def _bias_rows(rel_bias, dist):
    bucket = _rel_bucket(dist)[..., None]
    out = jnp.zeros(dist.shape + (rel_bias.shape[1],), F32)
    for b in range(NUM_BUCKETS):
        out = jnp.where(bucket == b, rel_bias[b].astype(F32), out)
    return out


def _prompt_bias(rel_bias):
    sub = ATT_SUB
    qi = jnp.arange(sub)[:, None]
    ki = jnp.arange(2 * sub)[None, :]
    dsub = qi + sub - ki
    ok = ((dsub >= 0) & (dsub <= sub))[..., None]
    tabs = []
    for _, dil in DILATION_PAIRS:
        b = _bias_rows(rel_bias, dil * jnp.clip(dsub, 0, sub))
        tabs.append(jnp.transpose(jnp.where(ok, b, NEG_INF), (2, 0, 1)))
    return jnp.stack(tabs)


def _head_rms(x, g):
    return x * lax.rsqrt(jnp.mean(x * x, axis=-1, keepdims=True) + RMS_EPS) * g


def _attn_prompt_kernel(q_ref, k_ref, v_ref, bias_ref, qg_ref, kg_ref, o_ref, kn_ref, vkeep_ref,
                        qs, ks, acc, m_s, l_s, *, t):
    sub = ATT_SUB
    keep = kn_ref.shape[0]
    qs[...] = _head_rms(q_ref[...], qg_ref[...]) * (1.0 / math.sqrt(HEAD_DIM_A))
    ks[...] = _head_rms(k_ref[...], kg_ref[...])
    kn_ref[...] = ks[pl.ds(t - keep, keep), :]
    vkeep_ref[...] = v_ref[pl.ds(t - keep, keep), :]
    m_s[...] = jnp.full(m_s.shape, NEG_INF, F32)
    l_s[...] = jnp.zeros(l_s.shape, F32)
    acc[...] = jnp.zeros(acc.shape, F32)
    for g, (_, dil) in enumerate(DILATION_PAIRS):
        span = dil * sub
        nblk = t // span
        bias_prev = bias_ref[g, :, 0:sub]
        bias_cur = bias_ref[g, :, sub:2 * sub]
        per_r = min(dil, ATT_UNITS)
        per_n = ATT_UNITS // per_r
        for r0 in range(0, dil, per_r):
            def body(it, _, dil=dil, span=span, r0=r0, per_r=per_r, per_n=per_n,
                     bias_prev=bias_prev, bias_cur=bias_cur):
                units = []
                for dr in range(per_r):
                    for dn in range(per_n):
                        n = it * per_n + dn
                        start = r0 + dr + span * n
                        startp = r0 + dr + span * jnp.maximum(n - 1, 0)
                        if dil == 1:
                            units.append((n, pl.ds(start, sub), pl.ds(startp, sub)))
                        else:
                            units.append((n, pl.ds(start, sub, stride=dil), pl.ds(startp, sub, stride=dil)))
                qb = [qs[rows, :].astype(BF16) for _, rows, _ in units]
                sc = [_dot_nt(q, ks[rows, :].astype(BF16)) + bias_cur for q, (_, rows, _) in zip(qb, units)]
                sp = [jnp.where(n > 0, _dot_nt(q, ks[rowsp, :].astype(BF16)) + bias_prev, NEG_INF)
                      for q, (n, _, rowsp) in zip(qb, units)]
                m_old = [m_s[rows, :] for _, rows, _ in units]
                m_new = [jnp.maximum(mo, jnp.maximum(jnp.max(a, axis=-1, keepdims=True),
                                                     jnp.max(b, axis=-1, keepdims=True)))
                         for mo, a, b in zip(m_old, sc, sp)]
                pc = [jnp.exp(a - mn) for a, mn in zip(sc, m_new)]
                pp = [jnp.exp(b - mn) for b, mn in zip(sp, m_new)]
                pv = [_dot(a.astype(BF16), v_ref[rows, :].astype(BF16))
                      + _dot(b.astype(BF16), v_ref[rowsp, :].astype(BF16))
                      for a, b, (_, rows, rowsp) in zip(pc, pp, units)]
                for i, (_, rows, _) in enumerate(units):
                    alpha = jnp.exp(m_old[i] - m_new[i])
                    l_s[rows, :] = (alpha * l_s[rows, :] + jnp.sum(pc[i], axis=-1, keepdims=True)
                                    + jnp.sum(pp[i], axis=-1, keepdims=True))
                    acc[rows, :] = alpha * acc[rows, :] + pv[i]
                    m_s[rows, :] = m_new[i]
                return 0

            lax.fori_loop(0, nblk // per_n, body, 0)
    o_ref[...] = (acc[...] / l_s[...]).astype(o_ref.dtype)


def _attn_prompt(pm3, rel_bias, qg, kg, d_att):
    n, t, _ = pm3.shape
    nh = d_att // HEAD_DIM_A
    assert t % MAX_WINDOW == 0
    keep = min(MAX_WINDOW, t)
    bias = _prompt_bias(rel_bias)
    blk = lambda off: pl.BlockSpec((None, t, HEAD_DIM_A), lambda i, h: (i, 0, off + h))
    kblk = pl.BlockSpec((None, keep, HEAD_DIM_A), lambda i, h: (i, 0, h))
    vec = pl.BlockSpec((1, HEAD_DIM_A), lambda i, h: (0, 0))
    return pl.pallas_call(
        functools.partial(_attn_prompt_kernel, t=t),
        grid=(n, nh),
        in_specs=[blk(0), blk(nh), blk(2 * nh),
                  pl.BlockSpec((len(DILATION_PAIRS), None, ATT_SUB, 2 * ATT_SUB), lambda i, h: (0, h, 0, 0)),
                  vec, vec],
        out_specs=[blk(0), kblk, kblk],
        out_shape=[jax.ShapeDtypeStruct((n, t, d_att), BF16)] + [jax.ShapeDtypeStruct((n, keep, d_att), F32)] * 2,
        scratch_shapes=[pltpu.VMEM((t, HEAD_DIM_A), F32)] * 5,
        compiler_params=_cparams(("parallel", "parallel")),
        name="attn_prompt",
    )(pm3, pm3, pm3, bias, qg.reshape(1, -1), kg.reshape(1, -1))


ROWS_PAD = 8


def _sample_bias(rel_bias, dseq):
    sub = ATT_SUB
    nh = rel_bias.shape[1]
    rb = lambda dist: _bias_rows(rel_bias, dist)
    c = jnp.arange(sub)
    tabs = [rb(dil * (sub - c)) for _, dil in reversed(DILATION_PAIRS[1:])]
    for s in range(dseq):
        tabs.append(jnp.where((c >= s)[:, None], rb(jnp.clip(sub + s - c, 0, sub)), NEG_INF))
    cache = jnp.broadcast_to(jnp.stack(tabs)[..., None], (len(tabs), sub, nh, LANE))
    s = jnp.arange(dseq)[:, None]
    sp = jnp.arange(dseq)[None, :]
    new = []
    for _, dil in DILATION_PAIRS:
        ok = (sp <= s) if dil == 1 else (sp == s)
        new.append(jnp.where(ok[..., None], rb(dil * jnp.clip(s - sp, 0, sub)), NEG_INF))
    new = jnp.broadcast_to(jnp.stack(new)[..., None], (len(new), dseq, dseq, nh, LANE))
    return cache, new


def _attn_sample_kernel(x_ref, k16_ref, k4_ref, v16_ref, v4_ref, bc_ref, bn_ref, qg_ref, kg_ref,
                        o_ref, kn_ref, *, dseq):
    nh, dh = x_ref.shape[-2], x_ref.shape[-1]
    sub = ATT_SUB
    d4, d16 = DILATION_PAIRS[1][1], DILATION_PAIRS[2][1]
    far = k16_ref.shape[0]
    qn = _head_rms(x_ref[0], qg_ref[...]) * (1.0 / math.sqrt(dh))
    kn = _head_rms(x_ref[1], kg_ref[...])
    vn = x_ref[2]
    kn_ref[...] = kn
    ones = jnp.ones((dh, LANE), F32)
    tail = sub // d4

    def lane_sum(x):
        keys = x.shape[0]
        return _dot(x.reshape(keys * nh, dh), ones).reshape(keys, nh, LANE)

    outs = []
    for s in range(dseq):
        q = qn[s]
        k1 = k4_ref[sub - tail:sub].reshape(sub, nh, dh)
        v1 = v4_ref[sub - tail:sub].reshape(sub, nh, dh)
        near = pl.ds(0, sub - far, stride=d16 // d4)
        cache = [(k16_ref[:, s], v16_ref[:, s], bc_ref[0, 0:far]),
                 (k4_ref[near, s], v4_ref[near, s], bc_ref[0, far:sub]),
                 (k4_ref[:, s], v4_ref[:, s], bc_ref[1]), (k1, v1, bc_ref[2 + s])]
        logits = [lane_sum(kk * q[None]) + bias for kk, _, bias in cache]
        new = []
        for s2 in range(s + 1):
            ln = jnp.broadcast_to(jnp.sum(q * kn[s2], axis=-1, keepdims=True), (nh, LANE))
            new += [ln + bn_ref[g, s, s2] for g in range(len(DILATION_PAIRS))]
        m = functools.reduce(jnp.maximum, [jnp.max(x, axis=0) for x in logits] + new)
        p = [jnp.exp(x - m[None]) for x in logits]
        pn = [jnp.exp(x - m) for x in new]
        denom = functools.reduce(jnp.add, [jnp.sum(x, axis=0) for x in p] + pn)
        o = functools.reduce(jnp.add, [jnp.sum(pi * vv, axis=0) for pi, (_, vv, _) in zip(p, cache)])
        for s2 in range(s + 1):
            w = functools.reduce(jnp.add, pn[s2 * len(DILATION_PAIRS):(s2 + 1) * len(DILATION_PAIRS)])
            o = o + w * vn[s2]
        outs.append(o / denom)
    outs += [jnp.zeros((nh, dh), F32)] * (ROWS_PAD - dseq)
    o_ref[...] = jnp.stack(outs)


def _attn_sample(x5, cache_k, cache_v, rel_bias, qg, kg, dseq):
    n, wb, nh, dh = cache_k.shape
    d4, d16 = DILATION_PAIRS[1][1], DILATION_PAIRS[2][1]
    assert wb == MAX_WINDOW == d16 * ATT_SUB and dseq <= d4 and DILATION_PAIRS[0][1] == 1 and dh == LANE
    bias_c, bias_n = _sample_bias(rel_bias, dseq)
    view = lambda c, d: c.reshape(n, wb // d, d, nh, dh)
    far = ATT_SUB - ATT_SUB * d4 // d16
    s16 = pl.BlockSpec((None, far, dseq, nh, dh), lambda i: (i, 0, 0, 0, 0))
    s4 = pl.BlockSpec((None, ATT_SUB, d4, nh, dh), lambda i: (i, wb // d4 // ATT_SUB - 1, 0, 0, 0))
    row_spec = pl.BlockSpec((None, ROWS_PAD, nh, dh), lambda i: (i, 0, 0, 0))
    vec = pl.BlockSpec((1, dh), lambda i: (0, 0))
    const = lambda a: pl.BlockSpec(a.shape, lambda i: (0,) * a.ndim, pipeline_mode=pl.Buffered(1))
    return pl.pallas_call(
        functools.partial(_attn_sample_kernel, dseq=dseq),
        grid=(n,),
        in_specs=[pl.BlockSpec((None, 3, ROWS_PAD, nh, dh), lambda i: (i, 0, 0, 0, 0)), s16, s4, s16, s4,
                  const(bias_c), const(bias_n), vec, vec],
        out_specs=[row_spec, row_spec],
        out_shape=[jax.ShapeDtypeStruct((n, ROWS_PAD, nh, dh), F32)] * 2,
        compiler_params=_cparams(("parallel",)),
        name="attn_sample",
    )(x5, view(cache_k, d16), view(cache_k, d4), view(cache_v, d16), view(cache_v, d4), bias_c, bias_n,
      qg.reshape(1, -1), kg.reshape(1, -1))


def _prep_weights(lp):
    (g_mix, w_in, q_norm_g, k_norm_g, mu_shift, w0, w2, a0, a2, g2, k_k, k_a, r_k, ln_x_w, ln_x_b, w_out,
     g_ffn, w_gate, w_up, conv_w, conv_b, w_down, g_ple, w_ple, w_ple_gate) = lp
    d_model = w_in.shape[0]
    d_rwkv = w0.shape[0]
    d_att = w_out.shape[0] - d_rwkv
    main = 3 * d_att + 3 * d_rwkv
    lora = w_in.shape[1] - main
    lora_p = _round_up(lora, LANE)
    d_ff = w_gate.shape[1]
    dffp = _round_up(d_ff, FF_ALIGN)
    assert main % lora_p == 0 and main % PROJ_TN == 0
    padc = lambda w, n: jnp.pad(w, ((0, 0), (0, n - w.shape[1])))
    wp = dict(
        d_model=d_model, d_att=d_att, d_rwkv=d_rwkv, lora=lora, lora_p=lora_p, lora_blk=main // lora_p,
        d_ff=d_ff, dffp=dffp,
        g_mix=g_mix, g_ffn=g_ffn, g_ple=g_ple, q_norm_g=q_norm_g, k_norm_g=k_norm_g,
        mu_shift=mu_shift, w0=w0, w2=w2, a0=a0, a2=a2, g2=g2, k_k=k_k, k_a=k_a, r_k=r_k,
        ln_x_w=ln_x_w, ln_x_b=ln_x_b,
        conv_w=padc(conv_w, dffp), conv_b=jnp.pad(conv_b, (0, dffp - d_ff)), w_ple=w_ple.astype(BF16),
    )
    big = dict(w_in_t=jnp.swapaxes(w_in, 0, 1), w_out=w_out, w_gate=w_gate, w_up=w_up, w_down=w_down,
               w_ple_gate=w_ple_gate)
    return wp, big


PROJ_TN = 512


def _layer(x2d, p2d, wp, big, mixer, *, tm, tf, down, shift, conv_prev, tiles_per_group):
    d_model, dffp = wp["d_model"], wp["dffp"]
    emit = big["w_in_t"].dtype != BF16
    w16 = dict(big)

    def mm(name, a, key, tm=tm, **kw):
        res = _matmul(a, big[key], tm=tm, name=name, **kw)
        if emit:
            res, w16[key] = res
        return res

    xn = _rmsnorm(x2d, wp["g_mix"], min(tm, 256))
    proj = mm("in_proj", xn, "w_in_t", tn=PROJ_TN, tk=d_model, b_t=True)
    mix, aux = mixer(proj)
    h1 = mm("out_proj", mix, "w_out", tn=PROJ_TN, tk=mix.shape[1], mode="residual", extras=(x2d,))
    hn = _rmsnorm(h1, wp["g_ffn"], min(tm, 256))
    cprev = jnp.pad(conv_prev, ((0, 0), (0, 0), (0, dffp - conv_prev.shape[-1])))
    res = _ffn_act(hn, big["w_gate"], big["w_up"], wp["conv_w"], wp["conv_b"], cprev,
                   tm=tm, tf=tf, shift=shift, tiles_per_group=tiles_per_group)
    act, conv_tail = res[0], res[1]
    if emit:
        w16["w_gate"], w16["w_up"] = res[2], res[3]
    h2 = mm("ffn_down", act, "w_down", mode="residual", extras=(h1,), **down)
    hn2 = _rmsnorm(h2, wp["g_ple"], min(tm, 256))
    y = mm("ple_gate", hn2, "w_ple_gate", tn=PROJ_TN, tk=d_model, mode="ple",
           extras=(h2, p2d.astype(BF16), wp["w_ple"]))
    conv_tail = conv_tail[tiles_per_group - 1::tiles_per_group, :, :wp["d_ff"]]
    return y, conv_tail, aux, w16


def kernel(x_prompt, x_sample, cache_k, cache_v, state_shift, state_wkv, state_conv, p_prompt, p_sample, rel_bias, g_mix, w_in, q_norm_g, k_norm_g, mu_shift, w0, w2, a0, a2, g2, k_k, k_a, r_k, ln_x_w, ln_x_b, w_out, g_ffn, w_gate, w_up, conv_w, conv_b, w_down, g_ple, w_ple, w_ple_gate):
    depth = g_mix.shape[0]
    nbp, seq, d_model = x_prompt.shape
    nbs, dseq, _ = x_sample.shape
    hp = x_prompt.reshape(nbp * seq, d_model)
    hs = jnp.swapaxes(x_sample, 0, 1).reshape(dseq * nbs, d_model)
    outs_p, outs_s = [], []
    for i in range(depth):
        lp = (g_mix[i], w_in[i], q_norm_g[i], k_norm_g[i], mu_shift[i], w0[i], w2[i], a0[i], a2[i], g2[i],
              k_k[i], k_a[i], r_k[i], ln_x_w[i], ln_x_b[i], w_out[i], g_ffn[i], w_gate[i], w_up[i],
              conv_w[i], conv_b[i], w_down[i], g_ple[i], w_ple[i], w_ple_gate[i])
        wp, big = _prep_weights(lp)
        d_att, d_rwkv, lora, d_ff, dffp = wp["d_att"], wp["d_rwkv"], wp["lora"], wp["d_ff"], wp["dffp"]
        nha = d_att // HEAD_DIM_A
        nhb = d_rwkv // HEAD_DIM_B
        qg, kg = q_norm_g[i], k_norm_g[i]
        state_cols = slice(3 * d_att, 3 * d_att + 3 * d_rwkv + lora)

        def mixer_prompt(proj, shift_prev, wkv_prev):
            pm = proj.reshape(nbp, seq, -1)
            o_att, k_keep, v_keep = _attn_prompt(pm, rel_bias, qg, kg, d_att)
            o_rwkv, wkv_new = _rwkv_chunked(pm, shift_prev, wkv_prev, wp, tb=min(seq, 512),
                                            nb=2 if nbp % 2 == 0 else 1)
            heads = lambda z: z.reshape(nbp, -1, nha, HEAD_DIM_A)
            mix = jnp.concatenate([o_att, o_rwkv], axis=-1).reshape(nbp * seq, -1)
            return mix, (heads(k_keep), heads(v_keep), pm[:, -1, state_cols], wkv_new)

        def mixer_sample(proj, ck, cv, shift_prev, wkv_prev):
            pm = proj.reshape(dseq, nbs, -1)
            x5 = jnp.transpose(pm[..., :3 * d_att].reshape(dseq, nbs, 3, nha, HEAD_DIM_A), (1, 2, 0, 3, 4))
            x5 = jnp.pad(x5, ((0, 0), (0, 0), (0, ROWS_PAD - dseq), (0, 0), (0, 0)))
            o_att, kn = _attn_sample(x5, ck, cv, rel_bias, qg, kg, dseq)
            o_att = jnp.swapaxes(o_att[:, :dseq], 0, 1).reshape(dseq * nbs, d_att).astype(BF16)
            o_rwkv, wkv_new = _rwkv_steps(proj, shift_prev, wkv_prev, wp, nseq=nbs, dseq=dseq)
            mix = jnp.concatenate([o_att, o_rwkv], axis=-1)
            v_new = jnp.swapaxes(pm[..., 2 * d_att:3 * d_att], 0, 1).reshape(nbs, dseq, nha, HEAD_DIM_A)
            return mix, (kn[:, :dseq], v_new, pm[-1, :, state_cols], wkv_new)

        mix_s = functools.partial(mixer_sample, ck=cache_k[i], cv=cache_v[i],
                                  shift_prev=state_shift[i], wkv_prev=state_wkv[i])
        conv_prev_s = jnp.swapaxes(state_conv[i], 0, 1).reshape(1, (CONV_WIDTH - 1) * nbs, d_ff)
        p_s = jnp.swapaxes(p_sample[i], 0, 1).reshape(dseq * nbs, -1)
        down_s = dict(tm=dseq * nbs, tn=min(1024, d_model), tk=2 * FF_ALIGN if dffp % (2 * FF_ALIGN) == 0 else FF_ALIGN)
        hs, conv_s, aux_s, w16 = _layer(hs, p_s, wp, big, mix_s, tm=dseq * nbs, tf=FF_ALIGN // 2,
                                        down=down_s, shift=nbs, conv_prev=conv_prev_s, tiles_per_group=1)
        conv_s = jnp.swapaxes(conv_s.reshape(CONV_WIDTH - 1, nbs, d_ff), 0, 1)
        outs_s.append((*aux_s, conv_s))

        tm_p = min(1024, seq)
        zero_shift = jnp.zeros((nbp, 3 * d_rwkv + lora), F32)
        zero_wkv = jnp.zeros((nbp, nhb, HEAD_DIM_B, HEAD_DIM_B), F32)
        zero_conv = jnp.zeros((nbp, CONV_WIDTH - 1, d_ff), F32)
        mix_p = functools.partial(mixer_prompt, shift_prev=zero_shift, wkv_prev=zero_wkv)
        down_p = dict(tm=tm_p // 2, tn=PROJ_TN, tk=dffp)
        hp, conv_p, aux_p, _ = _layer(hp, p_prompt[i].reshape(nbp * seq, -1), wp, w16, mix_p, tm=tm_p, tf=FF_ALIGN,
                                      down=down_p, shift=1, conv_prev=zero_conv,
                                      tiles_per_group=seq // tm_p)
        outs_p.append((*aux_p, conv_p))

    y_p = hp.reshape(nbp, seq, d_model)
    y_s = jnp.swapaxes(hs.reshape(dseq, nbs, d_model), 0, 1)
    st = lambda outs, idx: jnp.stack([o[idx] for o in outs])
    return (y_p, y_s, st(outs_p, 0), st(outs_p, 1), st(outs_p, 2), st(outs_p, 3), st(outs_p, 4),
            st(outs_s, 0), st(outs_s, 1), st(outs_s, 2), st(outs_s, 3), st(outs_s, 4))
```

```python
import functools
import math

import jax
import jax.numpy as jnp
from jax import lax
from jax.experimental import pallas as pl
from jax.experimental.pallas import tpu as pltpu

F32 = jnp.float32
BF16 = jnp.bfloat16

HEAD_DIM_A = 128
HEAD_DIM_B = 64
DILATION_PAIRS = ((128, 1), (512, 4), (2048, 16))
MAX_WINDOW = max(w for w, _ in DILATION_PAIRS)
NUM_BUCKETS = 32
REL_MAX_DIST = MAX_WINDOW
DECAY_LORA = 128
AAA_LORA = 128
CONV_WIDTH = 3
RMS_EPS = 1e-6
GN_EPS = 64e-5
NEG_INF = -1e30

LANE = 128
FF_ALIGN = 512
VMEM_LIMIT = 56 * 1024 * 1024


def _cparams(sem):
    return pltpu.CompilerParams(dimension_semantics=sem, vmem_limit_bytes=VMEM_LIMIT)


def _round_up(x, m):
    return -(-x // m) * m


def _dot(a, b, prec=None):
    return jnp.dot(a, b, preferred_element_type=F32, precision=prec)


def _dot_nt(a, b, prec=None):
    return lax.dot_general(a, b, (((1,), (1,)), ((), ())), preferred_element_type=F32, precision=prec)


def _bdot(a, b):
    return _dot(a.astype(BF16), b.astype(BF16))


def _rms_kernel(x_ref, g_ref, o_ref):
    x = x_ref[...]
    ms = jnp.mean(x * x, axis=-1, keepdims=True)
    o_ref[...] = (x * lax.rsqrt(ms + RMS_EPS) * g_ref[...]).astype(o_ref.dtype)


def _rmsnorm(x, g, tm):
    m, d = x.shape
    return pl.pallas_call(
        _rms_kernel,
        grid=(m // tm,),
        in_specs=[pl.BlockSpec((tm, d), lambda i: (i, 0)), pl.BlockSpec((1, d), lambda i: (0, 0))],
        out_specs=pl.BlockSpec((tm, d), lambda i: (i, 0)),
        out_shape=jax.ShapeDtypeStruct((m, d), BF16),
        compiler_params=_cparams(("parallel",)),
        name="rmsnorm",
    )(x, g.reshape(1, d))


_N_EXTRA = {"plain": 0, "residual": 1, "ple": 3}


def _cast_tile(b, row0, col0, k_valid, n_valid):
    b16 = b.astype(BF16)
    if k_valid is not None:
        r = row0 + lax.broadcasted_iota(jnp.int32, (b.shape[0], 1), 0)
        b16 = jnp.where(r < k_valid, b16, jnp.zeros_like(b16))
    if n_valid is not None:
        c = col0 + lax.broadcasted_iota(jnp.int32, (1, b.shape[1]), 1)
        b16 = jnp.where(c < n_valid, b16, jnp.zeros_like(b16))
    return b16


def _mm_kernel(*refs, nk, mode, emit, k_valid, n_valid, b_t):
    a_ref, b_ref = refs[0], refs[1]
    n_extra = _N_EXTRA[mode]
    o_ref = refs[2 + n_extra]
    acc_ref = refs[-1] if nk > 1 else None

    def epilogue(acc):
        if mode == "plain":
            return acc
        if mode == "residual":
            return refs[2][...] + acc
        h_ref, p_ref, wp_ref = refs[2], refs[3], refs[4]
        ple = _dot(p_ref[...], wp_ref[...])
        return h_ref[...] + ple * jax.nn.sigmoid(acc)

    b = b_ref[...]
    if emit and b_t:
        tn, tk = b.shape
        b = _cast_tile(b, pl.program_id(1) * tn, pl.program_id(2) * tk, n_valid, k_valid)
    elif emit:
        tk, tn = b.shape
        b = _cast_tile(b, pl.program_id(2) * tk, pl.program_id(1) * tn, k_valid, n_valid)
    if emit:
        refs[3 + n_extra][...] = b
    part = _dot_nt(a_ref[...], b) if b_t else _dot(a_ref[...], b)
    if nk == 1:
        o_ref[...] = epilogue(part).astype(o_ref.dtype)
        return
    k = pl.program_id(2)

    @pl.when(k == 0)
    def _():
        acc_ref[...] = part

    @pl.when(k > 0)
    def _():
        acc_ref[...] += part

    @pl.when(k == nk - 1)
    def _():
        o_ref[...] = epilogue(acc_ref[...]).astype(o_ref.dtype)


def _matmul(a, b, *, tm, tn, tk, mode="plain", extras=(), b_t=False, name="matmul"):
    m, kp = a.shape
    emit = b.dtype != BF16
    kb, nb_ = (b.shape[1], b.shape[0]) if b_t else b.shape
    np_ = _round_up(nb_, tn)
    assert kp % tk == 0 and m % tm == 0 and (emit or (kb, nb_) == (kp, np_)) and not (emit and m != tm)
    nk = kp // tk
    k_valid = kb if emit and kb != kp else None
    n_valid = nb_ if emit and nb_ != np_ else None
    last_k, last_j = pl.cdiv(kb, tk) - 1, pl.cdiv(nb_, tn) - 1
    if b_t:
        b_spec = pl.BlockSpec((tn, tk), lambda i, j, k: (jnp.minimum(j, last_j), jnp.minimum(k, last_k)))
        w16_spec, w16_shape = pl.BlockSpec((tn, tk), lambda i, j, k: (j, k)), (np_, kp)
    else:
        b_spec = pl.BlockSpec((tk, tn), lambda i, j, k: (jnp.minimum(k, last_k), jnp.minimum(j, last_j)))
        w16_spec, w16_shape = pl.BlockSpec((tk, tn), lambda i, j, k: (k, j)), (kp, np_)
    in_specs = [pl.BlockSpec((tm, tk), lambda i, j, k: (i, k)), b_spec]
    if mode == "residual":
        in_specs.append(pl.BlockSpec((tm, tn), lambda i, j, k: (i, j)))
    elif mode == "ple":
        pdim = extras[1].shape[1]
        in_specs += [pl.BlockSpec((tm, tn), lambda i, j, k: (i, j)),
                     pl.BlockSpec((tm, pdim), lambda i, j, k: (i, 0)),
                     pl.BlockSpec((pdim, tn), lambda i, j, k: (0, j))]
    out_specs = [pl.BlockSpec((tm, tn), lambda i, j, k: (i, j))]
    out_shape = [jax.ShapeDtypeStruct((m, np_), F32)]
    if emit:
        out_specs.append(w16_spec)
        out_shape.append(jax.ShapeDtypeStruct(w16_shape, BF16))
    scratch = [pltpu.VMEM((tm, tn), F32)] if nk > 1 else []
    res = pl.pallas_call(
        functools.partial(_mm_kernel, nk=nk, mode=mode, emit=emit, k_valid=k_valid, n_valid=n_valid, b_t=b_t),
        grid=(m // tm, np_ // tn, nk),
        in_specs=in_specs,
        out_specs=out_specs,
        out_shape=out_shape,
        scratch_shapes=scratch,
        compiler_params=_cparams(("parallel", "parallel", "arbitrary")),
        name=name,
    )(a, b, *extras)
    return tuple(res) if emit else res[0]


FFN_SUB = 256


def _ffn_kernel(a_ref, wg_ref, wu_ref, cw_ref, cb_ref, prev_ref, act_ref, tail_ref, *rest,
                tm, tf, shift, tiles_per_group, emit, n_valid):
    i = pl.program_id(0)
    j = pl.program_id(1)
    hist = (CONV_WIDTH - 1) * shift
    pad = _round_up(hist, 8)
    ext_ref, carry_ref = rest[-2], rest[-1]
    a = a_ref[...]
    sw = min(tf, FFN_SUB)
    subs = [slice(s, s + sw) for s in range(0, tf, sw)]

    def weight(w_ref, w16_ref, cols):
        w = w_ref[:, cols]
        if emit:
            w = _cast_tile(w, 0, j * tf + cols.start, None, n_valid)
            w16_ref[:, cols] = w
        return w

    @pl.when((i == 0) & (j == 0))
    def _():
        ext_ref[...] = jnp.zeros(ext_ref.shape, F32)

    @pl.when(i % tiles_per_group == 0)
    def _():
        ext_ref[pl.ds(pad - hist, hist), :] = prev_ref[0]

    @pl.when(i % tiles_per_group != 0)
    def _():
        ext_ref[pl.ds(pad - hist, hist), :] = carry_ref[j]

    row8 = lax.broadcasted_iota(jnp.int32, (8, 1), 0)

    def delayed(u, k, cols):
        off = k * shift
        if off % 8 == 0:
            return jnp.concatenate([ext_ref[pl.ds(pad - off, off), cols], u[0:tm - off]], axis=0)
        rolled = pltpu.roll(u, off, 0)
        head = jnp.where(row8 < off, ext_ref[pl.ds(pad - off, 8), cols], rolled[0:8])
        return jnp.concatenate([head, rolled[8:]], axis=0)

    for cols in subs:
        u = _dot(a, weight(wg_ref, rest[0], cols))
        up = _dot(a, weight(wu_ref, rest[1], cols))
        c = (cb_ref[:, cols] + cw_ref[0:1, cols] * delayed(u, 2, cols) + cw_ref[1:2, cols] * delayed(u, 1, cols)
             + cw_ref[2:3, cols] * u)
        act_ref[:, cols] = (jax.nn.silu(c) * up).astype(act_ref.dtype)
        ext_ref[pl.ds(pad, pad), cols] = u[tm - pad:tm]
    tail = ext_ref[pl.ds(2 * pad - hist, hist), :]
    carry_ref[j] = tail
    tail_ref[0] = tail


def _ffn_act(xn, wg, wu, conv_w, conv_b, conv_prev, *, tm, tf, shift, tiles_per_group):
    m, d = xn.shape
    dffp = conv_w.shape[1]
    emit = wg.dtype != BF16
    assert dffp % tf == 0 and (emit or wg.shape[1] == dffp) and not (emit and m != tm)
    hist = (CONV_WIDTH - 1) * shift
    nj = dffp // tf
    n_valid = wg.shape[1] if emit and wg.shape[1] != dffp else None
    kern = functools.partial(_ffn_kernel, tm=tm, tf=tf, shift=shift, tiles_per_group=tiles_per_group,
                             emit=emit, n_valid=n_valid)
    last = pl.cdiv(wg.shape[1], tf) - 1
    wspec = pl.BlockSpec((d, tf), lambda i, j: (0, jnp.minimum(j, last)))
    out_specs = [pl.BlockSpec((tm, tf), lambda i, j: (i, j)), pl.BlockSpec((1, hist, tf), lambda i, j: (i, 0, j))]
    out_shape = [jax.ShapeDtypeStruct((m, dffp), BF16), jax.ShapeDtypeStruct((m // tm, hist, dffp), F32)]
    if emit:
        out_specs += [pl.BlockSpec((d, tf), lambda i, j: (0, j))] * 2
        out_shape += [jax.ShapeDtypeStruct((d, dffp), BF16)] * 2
    return pl.pallas_call(
        kern,
        grid=(m // tm, nj),
        in_specs=[
            pl.BlockSpec((tm, d), lambda i, j: (i, 0)), wspec, wspec,
            pl.BlockSpec((CONV_WIDTH, tf), lambda i, j: (0, j)),
            pl.BlockSpec((1, tf), lambda i, j: (0, j)),
            pl.BlockSpec((1, hist, tf), lambda i, j: (i // tiles_per_group, 0, j)),
        ],
        out_specs=out_specs,
        out_shape=out_shape,
        scratch_shapes=[pltpu.VMEM((2 * _round_up(hist, 8), tf), F32), pltpu.VMEM((nj, hist, tf), F32)],
        compiler_params=_cparams(("arbitrary", "arbitrary")),
        name="ffn_gate_up_conv",
    )(xn, wg, wu, conv_w, conv_b.reshape(1, dffp), conv_prev)


RWKV_CHUNK = 64


def _softplus(z):
    return jnp.maximum(z, 0.0) + jnp.log(1.0 + jnp.exp(-jnp.abs(z)))


def _pair_blockdiag():
    li = lax.broadcasted_iota(jnp.int32, (LANE, LANE), 0)
    lj = lax.broadcasted_iota(jnp.int32, (LANE, LANE), 1)
    return (li // HEAD_DIM_B == lj // HEAD_DIM_B).astype(F32)


def _head_sum(x, blockdiag):
    bd = blockdiag.astype(BF16)
    hi = x.astype(BF16)
    lo = (x - hi.astype(F32)).astype(BF16)
    return _dot(hi, bd) + _dot(lo, bd)


def _lora_act_kernel(x_ref, first_ref, mu_ref, o_ref, *, shift):
    x = x_ref[...]
    rows = x.shape[0]
    if shift % 8 == 0:
        prev = jnp.concatenate([first_ref[...], x[0:rows - shift]], axis=0)
    else:
        row = lax.broadcasted_iota(jnp.int32, (rows, 1), 0)
        prev = jnp.where(row < shift, first_ref[...], pltpu.roll(x, shift, 0))
    xs = x + mu_ref[...] * (prev - x)
    c1, c2 = DECAY_LORA, DECAY_LORA + AAA_LORA
    o_ref[:, 0:c1] = jnp.tanh(xs[:, 0:c1]).astype(o_ref.dtype)
    o_ref[:, c1:c2] = xs[:, c1:c2].astype(o_ref.dtype)
    o_ref[:, c2:] = jax.nn.sigmoid(xs[:, c2:]).astype(o_ref.dtype)


def _lora_act(x3, first3, wp, *, shift):
    g, r, _ = x3.shape
    lora, lora_p, lblk = wp["lora"], wp["lora_p"], wp["lora_blk"]
    assert shift == 1 or shift % 8 == 0
    first = jnp.pad(first3, ((0, 0), (0, 0), (0, lora_p - lora)))
    return pl.pallas_call(
        functools.partial(_lora_act_kernel, shift=shift),
        grid=(g,),
        in_specs=[pl.BlockSpec((None, r, lora_p), lambda i: (i, 0, lblk)),
                  pl.BlockSpec((None, shift, lora_p), lambda i: (i, 0, 0)),
                  pl.BlockSpec((1, lora_p), lambda i: (0, 0))],
        out_specs=pl.BlockSpec((None, r, lora_p), lambda i: (i, 0, 0)),
        out_shape=jax.ShapeDtypeStruct((g, r, lora_p), BF16),
        compiler_params=_cparams(("parallel",)),
        name="rwkv_lora_act",
    )(x3, first, _mu_split(wp)[1])


def _rwkv_pre(xk, la, prm, blockdiag):
    w0_ref, a0_ref, w2_ref, a2_ref, g2_ref, kk_ref, ka_ref = prm[:7]
    c1, c2 = DECAY_LORA, DECAY_LORA + AAA_LORA
    wl = w0_ref[...] + _dot(la[:, 0:c1], w2_ref[...])
    lw = -jnp.exp(-_softplus(-wl) - 0.5)
    a = jax.nn.sigmoid(a0_ref[...] + _dot(la[:, c1:c2], a2_ref[...]))
    g = _dot(la[:, c2:], g2_ref[...])
    kk = xk * kk_ref[...]
    kk = kk / jnp.maximum(jnp.sqrt(_head_sum(kk * kk, blockdiag)), 1e-12)
    kmod = xk * (1.0 + (a - 1.0) * ka_ref[...])
    return lw, -kk, kk * a, kmod, g


def _rwkv_post(y, xr, kmod, xv, g, prm, blockdiag):
    rk_ref, lnw_ref, lnb_ref = prm[7:10]
    inv = 1.0 / HEAD_DIM_B
    mean = _head_sum(y, blockdiag) * inv
    d = y - mean
    var = _head_sum(d * d, blockdiag) * inv
    yn = d * lax.rsqrt(var + GN_EPS) * lnw_ref[...] + lnb_ref[...]
    bonus = _head_sum(xr * kmod * rk_ref[...], blockdiag) * xv
    return (yn + bonus) * g


def _rwkv_chunk_kernel(pr_ref, pk_ref, pv_ref, la_ref, spr_ref, spk_ref, spv_ref,
                       mur_ref, muk_ref, muv_ref, *rest, tb, nb):
    prm = rest[:10]
    z0_ref, o_ref, zout_ref, z_s, cr_s, ck_s, cv_s = rest[10:]
    ti = pl.program_id(2)
    nt = pl.num_programs(2)
    cs = RWKV_CHUNK
    hd = HEAD_DIM_B

    lane = lax.broadcasted_iota(jnp.int32, (1, LANE), 1)
    m0 = (lane < hd).astype(F32)
    m1 = 1.0 - m0
    blockdiag = _pair_blockdiag()
    li = lax.broadcasted_iota(jnp.int32, (LANE, LANE), 0)
    lj = lax.broadcasted_iota(jnp.int32, (LANE, LANE), 1)
    eye = (li == lj).astype(F32)
    same = li // cs == lj // cs
    mask_incl = (same & (li >= lj)).astype(F32)
    mask_strict = (same & (li > lj)).astype(F32)
    ci = lax.broadcasted_iota(jnp.int32, (cs, cs), 0)
    cj = lax.broadcasted_iota(jnp.int32, (cs, cs), 1)
    tril_incl = (ci >= cj).astype(F32)
    row = lax.broadcasted_iota(jnp.int32, (tb, 1), 0)

    @pl.when(ti == 0)
    def _():
        z_s[...] = jnp.zeros(z_s.shape, F32)
        for q in range(nb):
            z_s[q, 0:hd, 0:hd] = z0_ref[q, 0]
            z_s[q, hd:2 * hd, hd:2 * hd] = z0_ref[q, 1]
        cr_s[...] = spr_ref[...]
        ck_s[...] = spk_ref[...]
        cv_s[...] = spv_ref[...]

    def shifted(x, carry_row, mu):
        prev = jnp.where(row == 0, carry_row, pltpu.roll(x, 1, 0))
        return x + mu * (prev - x)

    two = lambda x: jnp.concatenate([x * m0, x * m1], axis=0)
    fold = lambda x: x[0:cs] + x[cs:2 * cs]

    seqs = []
    for q in range(nb):
        pr, pk, pv = pr_ref[q], pk_ref[q], pv_ref[q]
        xr = shifted(pr, cr_s[q], mur_ref[...])
        xk = shifted(pk, ck_s[q], muk_ref[...])
        xv = shifted(pv, cv_s[q], muv_ref[...])
        cr_s[q] = pr[tb - 1:tb]
        ck_s[q] = pk[tb - 1:tb]
        cv_s[q] = pv[tb - 1:tb]
        lw, aneg, bb, kmod, g = _rwkv_pre(xk, la_ref[q], prm, blockdiag)
        seqs.append(dict(xr=xr, xv=xv, lw=lw, aneg=aneg, bb=bb, kmod=kmod, g=g))

    nchunk = tb // cs
    units = []
    for q in range(nb):
        for c in range(nchunk):
            rows = slice(c * cs, (c + 1) * cs)
            units.append({k: v[rows] for k, v in seqs[q].items() if k != "g"})
    tril16 = tril_incl.astype(BF16)
    for u in units:
        lw1 = u["lw"].astype(BF16)
        r1 = u["lw"] - lw1.astype(F32)
        lw2 = r1.astype(BF16)
        lw3 = (r1 - lw2.astype(F32)).astype(BF16)
        u["cum"] = _dot(tril16, lw1) + (_dot(tril16, lw2) + _dot(tril16, lw3))
    for u in units:
        cum = u["cum"]
        tot = cum[cs - 1:cs]
        e_inv = jnp.exp(-cum)
        e_end = jnp.exp(tot - cum)
        u["rt"] = u["xr"] * jnp.exp(cum)
        u["la"] = two(u["aneg"] * jnp.exp(cum - u["lw"]))
        u["mt"] = jnp.concatenate([u["bb"] * e_end, u["kmod"] * e_end], axis=0).T
        u["pc_col"] = jnp.broadcast_to(jnp.exp(tot), (LANE, LANE)).T
        u["gm"] = _dot_nt(jnp.concatenate([u["la"], two(u["rt"])], axis=0).astype(BF16),
                          jnp.concatenate([two(u["bb"] * e_inv), two(u["kmod"] * e_inv)], axis=0).astype(BF16))
    for u in units:
        gm = u.pop("gm")
        u["n_pow"] = gm[0:LANE, 0:LANE] * mask_strict
        u["tinv"] = eye + u["n_pow"]
        u["arb"] = gm[LANE:2 * LANE, 0:LANE] * mask_incl
        u["av"] = _bdot(jnp.concatenate([gm[0:LANE, LANE:2 * LANE] * mask_strict,
                                         gm[LANE:2 * LANE, LANE:2 * LANE] * mask_incl], axis=0), two(u["xv"]))
    for _ in range(int(math.log2(cs)) - 1):
        for u in units:
            u["n_pow"] = _bdot(u["n_pow"], u["n_pow"])
        for u in units:
            u["tinv"] = u["tinv"] + _bdot(u["tinv"], u["n_pow"])
    for u in units:
        wu = _bdot(u["tinv"], jnp.concatenate([u["la"], u["av"][0:LANE]], axis=1))
        u["w"] = fold(wu[:, 0:LANE])
        u["u0"] = fold(wu[:, LANE:2 * LANE])
        u["y0"] = fold(u["av"][LANE:2 * LANE])
    for u in units:
        vc = u["xv"]
        zz = _bdot(u["mt"], jnp.concatenate([jnp.concatenate([u["w"], u["u0"]], axis=1),
                                             jnp.concatenate([jnp.zeros_like(vc), vc], axis=1)], axis=0))
        u["zm"] = zz[:, 0:LANE] * blockdiag
        u["zc"] = zz[:, LANE:2 * LANE] * blockdiag

    zs = [z_s[q] for q in range(nb)]
    ys = [[] for _ in range(nb)]
    for c in range(nchunk):
        yus = [_bdot(jnp.concatenate([units[q * nchunk + c]["rt"], units[q * nchunk + c]["w"]], axis=0), zs[q])
               for q in range(nb)]
        zms = [_bdot(units[q * nchunk + c]["zm"], zs[q]) for q in range(nb)]
        for q in range(nb):
            u = units[q * nchunk + c]
            uu = yus[q][cs:2 * cs] + u["u0"]
            ys[q].append(yus[q][0:cs] + u["y0"] + fold(_bdot(u["arb"], two(uu))))
            zs[q] = u["pc_col"] * zs[q] + zms[q] + u["zc"]
    for q in range(nb):
        z_s[q] = zs[q]
        s = seqs[q]
        y = jnp.concatenate(ys[q], axis=0)
        o_ref[q] = _rwkv_post(y, s["xr"], s["kmod"], s["xv"], s["g"], prm, blockdiag).astype(o_ref.dtype)

    @pl.when(ti == nt - 1)
    def _():
        for q in range(nb):
            zout_ref[q, 0] = z_s[q, 0:hd, 0:hd]
            zout_ref[q, 1] = z_s[q, hd:2 * hd, hd:2 * hd]


def _rwkv_param_specs(wp, imap):
    d_rwkv, lora, lora_p = wp["d_rwkv"], wp["lora"], wp["lora_p"]
    gpad = lora_p - DECAY_LORA - AAA_LORA
    glora = lora - DECAY_LORA - AAA_LORA
    row2 = lambda v: v.reshape(1, d_rwkv)
    g2p = jnp.pad(wp["g2"], ((0, gpad - glora), (0, 0))).astype(BF16)
    vec = pl.BlockSpec((1, LANE), imap)
    ops = [row2(wp["w0"]), row2(wp["a0"]), wp["w2"].astype(BF16), wp["a2"].astype(BF16), g2p,
           row2(wp["k_k"]), row2(wp["k_a"]), row2(wp["r_k"]), row2(wp["ln_x_w"]), row2(wp["ln_x_b"])]
    specs = [vec, vec, pl.BlockSpec((DECAY_LORA, LANE), imap), pl.BlockSpec((AAA_LORA, LANE), imap),
             pl.BlockSpec((gpad, LANE), imap), vec, vec, vec, vec, vec]
    return ops, specs


def _mu_split(wp):
    d_rwkv, lora, lora_p = wp["d_rwkv"], wp["lora"], wp["lora_p"]
    mu = wp["mu_shift"]
    return mu[None, :3 * d_rwkv], jnp.pad(mu[None, 3 * d_rwkv:], ((0, 0), (0, lora_p - lora)))


def _rwkv_chunked(pm3, shift_prev, wkv_prev, wp, *, tb, nb):
    n, t, _ = pm3.shape
    d_att, d_rwkv, lora_p = wp["d_att"], wp["d_rwkv"], wp["lora_p"]
    nh = d_rwkv // HEAD_DIM_B
    cb = 3 * d_att // LANE
    rb = d_rwkv // LANE
    sp_main = shift_prev[:, None, :3 * d_rwkv]
    lact = _lora_act(pm3, shift_prev[:, None, 3 * d_rwkv:], wp, shift=1)
    mu_main, _ = _mu_split(wp)
    z0 = jnp.swapaxes(wkv_prev, -1, -2)
    p_ops, p_specs = _rwkv_param_specs(wp, lambda i, h, j: (0, h))

    blk3 = lambda off: pl.BlockSpec((nb, tb, LANE), lambda i, h, j: (i, j, off + h))
    sp3 = lambda off: pl.BlockSpec((nb, 1, LANE), lambda i, h, j: (i, 0, off + h))
    vec = lambda off: pl.BlockSpec((1, LANE), lambda i, h, j: (0, off + h))
    o, zout = pl.pallas_call(
        functools.partial(_rwkv_chunk_kernel, tb=tb, nb=nb),
        grid=(n // nb, rb, t // tb),
        in_specs=[
            blk3(cb), blk3(cb + rb), blk3(cb + 2 * rb),
            pl.BlockSpec((nb, tb, lora_p), lambda i, h, j: (i, j, 0)),
            sp3(0), sp3(rb), sp3(2 * rb),
            vec(0), vec(rb), vec(2 * rb),
            *p_specs,
            pl.BlockSpec((nb, 2, HEAD_DIM_B, HEAD_DIM_B), lambda i, h, j: (i, h, 0, 0)),
        ],
        out_specs=[
            pl.BlockSpec((nb, tb, LANE), lambda i, h, j: (i, j, h)),
            pl.BlockSpec((nb, 2, HEAD_DIM_B, HEAD_DIM_B), lambda i, h, j: (i, h, 0, 0)),
        ],
        out_shape=[jax.ShapeDtypeStruct((n, t, d_rwkv), BF16),
                   jax.ShapeDtypeStruct((n, nh, HEAD_DIM_B, HEAD_DIM_B), F32)],
        scratch_shapes=[pltpu.VMEM((nb, LANE, LANE), F32), pltpu.VMEM((nb, 1, LANE), F32),
                        pltpu.VMEM((nb, 1, LANE), F32), pltpu.VMEM((nb, 1, LANE), F32)],
        compiler_params=_cparams(("parallel", "parallel", "arbitrary")),
        name="rwkv7_chunked",
    )(pm3, pm3, pm3, lact, sp_main, sp_main, sp_main, mu_main, mu_main, mu_main, *p_ops, z0)
    return o, jnp.swapaxes(zout, -1, -2)


def _rwkv_step_kernel(pr_ref, pk_ref, pv_ref, la_ref, spr_ref, spk_ref, spv_ref,
                      mur_ref, muk_ref, muv_ref, *rest, nseq, dseq):
    prm = rest[:10]
    z0_ref, o_ref, z_ref, tr_s, y_s = rest[10:]
    hd = HEAD_DIM_B
    m = nseq * dseq
    blockdiag = _pair_blockdiag()

    def shifted(x, first, mu):
        prev = jnp.concatenate([first, x[0:m - nseq]], axis=0)
        return x + mu * (prev - x)

    xr = shifted(pr_ref[...], spr_ref[...], mur_ref[...])
    xk = shifted(pk_ref[...], spk_ref[...], muk_ref[...])
    xv = shifted(pv_ref[...], spv_ref[...], muv_ref[...])
    lw, aneg, bb, kmod, g = _rwkv_pre(xk, la_ref[...], prm, blockdiag)
    decay = jnp.exp(lw)
    z_ref[...] = z0_ref[...]
    ys = []
    for t in range(dseq):
        rows = slice(t * nseq, (t + 1) * nseq)
        for q, val in enumerate((aneg, bb, decay, kmod, xr, xv)):
            tr_s[q] = val[rows].T
        for hh in range(2):
            base = hh * hd
            vcol = tr_s[5, base:base + hd, :]

            def sa_body(k, acc, hh=hh, base=base):
                return acc + z_ref[hh, k] * tr_s[0, pl.ds(base + k, 1), :]

            sa = lax.fori_loop(0, hd, sa_body, jnp.zeros((hd, nseq), F32))

            def upd_body(k, y, hh=hh, base=base, sa=sa, vcol=vcol):
                zk = (z_ref[hh, k] * tr_s[2, pl.ds(base + k, 1), :] + sa * tr_s[1, pl.ds(base + k, 1), :]
                      + vcol * tr_s[3, pl.ds(base + k, 1), :])
                z_ref[hh, k] = zk
                return y + zk * tr_s[4, pl.ds(base + k, 1), :]

            y_s[base:base + hd, :] = lax.fori_loop(0, hd, upd_body, jnp.zeros((hd, nseq), F32))
        ys.append(y_s[...].T)
    y = jnp.concatenate(ys, axis=0)
    o_ref[...] = _rwkv_post(y, xr, kmod, xv, g, prm, blockdiag).astype(o_ref.dtype)


def _rwkv_steps(pm2, shift_prev, wkv_prev, wp, *, nseq, dseq):
    m = pm2.shape[0]
    d_att, d_rwkv, lora_p = wp["d_att"], wp["d_rwkv"], wp["lora_p"]
    nh = d_rwkv // HEAD_DIM_B
    cb = 3 * d_att // LANE
    rb = d_rwkv // LANE
    sp_main = shift_prev[:, :3 * d_rwkv]
    lact = _lora_act(pm2[None], shift_prev[None, :, 3 * d_rwkv:], wp, shift=nseq)[0]
    mu_main, _ = _mu_split(wp)
    z0 = jnp.transpose(wkv_prev, (1, 3, 2, 0))
    p_ops, p_specs = _rwkv_param_specs(wp, lambda h: (0, h))
    blk = lambda rows, off: pl.BlockSpec((rows, LANE), lambda h: (0, off + h))
    zspec = pl.BlockSpec((2, HEAD_DIM_B, HEAD_DIM_B, nseq), lambda h: (h, 0, 0, 0))
    o, zout = pl.pallas_call(
        functools.partial(_rwkv_step_kernel, nseq=nseq, dseq=dseq),
        grid=(rb,),
        in_specs=[
            blk(m, cb), blk(m, cb + rb), blk(m, cb + 2 * rb), pl.BlockSpec((m, lora_p), lambda h: (0, 0)),
            blk(nseq, 0), blk(nseq, rb), blk(nseq, 2 * rb),
            blk(1, 0), blk(1, rb), blk(1, 2 * rb),
            *p_specs, zspec,
        ],
        out_specs=[blk(m, 0), zspec],
        out_shape=[jax.ShapeDtypeStruct((m, d_rwkv), BF16),
                   jax.ShapeDtypeStruct((nh, HEAD_DIM_B, HEAD_DIM_B, nseq), F32)],
        scratch_shapes=[pltpu.VMEM((6, LANE, nseq), F32), pltpu.VMEM((LANE, nseq), F32)],
        compiler_params=_cparams(("parallel",)),
        name="rwkv7_steps",
    )(pm2, pm2, pm2, lact, sp_main, sp_main, sp_main, mu_main, mu_main, mu_main, *p_ops, z0)
    return o, jnp.transpose(zout, (3, 0, 2, 1))


ATT_SUB = DILATION_PAIRS[0][0] // DILATION_PAIRS[0][1]
assert all(w // d == ATT_SUB for w, d in DILATION_PAIRS)
ATT_UNITS = 8


def _rel_bucket(dist):
    max_exact = NUM_BUCKETS // 2
    d_f = jnp.maximum(dist, 1).astype(F32)
    large = max_exact + (jnp.log(d_f / max_exact) / math.log(REL_MAX_DIST / max_exact)
                         * (NUM_BUCKETS - max_exact)).astype(jnp.int32)
    large = jnp.minimum(large, NUM_BUCKETS - 1)
    return jnp.where(dist < max_exact, dist, large)


def _bias_rows(rel_bias, dist):
    bucket = _rel_bucket(dist)[..., None]
    out = jnp.zeros(dist.shape + (rel_bias.shape[1],), F32)
    for b in range(NUM_BUCKETS):
        out = jnp.where(bucket == b, rel_bias[b].astype(F32), out)
    return out


def _prompt_bias(rel_bias):
    sub = ATT_SUB
    qi = jnp.arange(sub)[:, None]
    ki = jnp.arange(2 * sub)[None, :]
    dsub = qi + sub - ki
    ok = ((dsub >= 0) & (dsub <= sub))[..., None]
    tabs = []
    for _, dil in DILATION_PAIRS:
        b = _bias_rows(rel_bias, dil * jnp.clip(dsub, 0, sub))
        tabs.append(jnp.transpose(jnp.where(ok, b, NEG_INF), (2, 0, 1)))
    return jnp.stack(tabs)


def _head_rms(x, g):
    return x * lax.rsqrt(jnp.mean(x * x, axis=-1, keepdims=True) + RMS_EPS) * g


def _attn_prompt_kernel(q_ref, k_ref, v_ref, bias_ref, qg_ref, kg_ref, o_ref, kn_ref, vkeep_ref,
                        qs, ks, acc, m_s, l_s, *, t):
    sub = ATT_SUB
    keep = kn_ref.shape[0]
    qs[...] = _head_rms(q_ref[...], qg_ref[...]) * (1.0 / math.sqrt(HEAD_DIM_A))
    ks[...] = _head_rms(k_ref[...], kg_ref[...])
    kn_ref[...] = ks[pl.ds(t - keep, keep), :]
    vkeep_ref[...] = v_ref[pl.ds(t - keep, keep), :]
    m_s[...] = jnp.full(m_s.shape, NEG_INF, F32)
    l_s[...] = jnp.zeros(l_s.shape, F32)
    acc[...] = jnp.zeros(acc.shape, F32)
    for g, (_, dil) in enumerate(DILATION_PAIRS):
        span = dil * sub
        nblk = t // span
        bias_prev = bias_ref[g, :, 0:sub]
        bias_cur = bias_ref[g, :, sub:2 * sub]
        per_r = min(dil, ATT_UNITS)
        per_n = ATT_UNITS // per_r
        for r0 in range(0, dil, per_r):
            def body(it, _, dil=dil, span=span, r0=r0, per_r=per_r, per_n=per_n,
                     bias_prev=bias_prev, bias_cur=bias_cur):
                units = []
                for dr in range(per_r):
                    for dn in range(per_n):
                        n = it * per_n + dn
                        start = r0 + dr + span * n
                        startp = r0 + dr + span * jnp.maximum(n - 1, 0)
                        if dil == 1:
                            units.append((n, pl.ds(start, sub), pl.ds(startp, sub)))
                        else:
                            units.append((n, pl.ds(start, sub, stride=dil), pl.ds(startp, sub, stride=dil)))
                qb = [qs[rows, :].astype(BF16) for _, rows, _ in units]
                sc = [_dot_nt(q, ks[rows, :].astype(BF16)) + bias_cur for q, (_, rows, _) in zip(qb, units)]
                sp = [jnp.where(n > 0, _dot_nt(q, ks[rowsp, :].astype(BF16)) + bias_prev, NEG_INF)
                      for q, (n, _, rowsp) in zip(qb, units)]
                m_old = [m_s[rows, :] for _, rows, _ in units]
                m_new = [jnp.maximum(mo, jnp.maximum(jnp.max(a, axis=-1, keepdims=True),
                                                     jnp.max(b, axis=-1, keepdims=True)))
                         for mo, a, b in zip(m_old, sc, sp)]
                pc = [jnp.exp(a - mn) for a, mn in zip(sc, m_new)]
                pp = [jnp.exp(b - mn) for b, mn in zip(sp, m_new)]
                pv = [_dot(a.astype(BF16), v_ref[rows, :].astype(BF16))
                      + _dot(b.astype(BF16), v_ref[rowsp, :].astype(BF16))
                      for a, b, (_, rows, rowsp) in zip(pc, pp, units)]
                for i, (_, rows, _) in enumerate(units):
                    alpha = jnp.exp(m_old[i] - m_new[i])
                    l_s[rows, :] = (alpha * l_s[rows, :] + jnp.sum(pc[i], axis=-1, keepdims=True)
                                    + jnp.sum(pp[i], axis=-1, keepdims=True))
                    acc[rows, :] = alpha * acc[rows, :] + pv[i]
                    m_s[rows, :] = m_new[i]
                return 0

            lax.fori_loop(0, nblk // per_n, body, 0)
    o_ref[...] = (acc[...] / l_s[...]).astype(o_ref.dtype)


def _attn_prompt(pm3, rel_bias, qg, kg, d_att):
    n, t, _ = pm3.shape
    nh = d_att // HEAD_DIM_A
    assert t % MAX_WINDOW == 0
    keep = min(MAX_WINDOW, t)
    bias = _prompt_bias(rel_bias)
    blk = lambda off: pl.BlockSpec((None, t, HEAD_DIM_A), lambda i, h: (i, 0, off + h))
    kblk = pl.BlockSpec((None, keep, HEAD_DIM_A), lambda i, h: (i, 0, h))
    vec = pl.BlockSpec((1, HEAD_DIM_A), lambda i, h: (0, 0))
    return pl.pallas_call(
        functools.partial(_attn_prompt_kernel, t=t),
        grid=(n, nh),
        in_specs=[blk(0), blk(nh), blk(2 * nh),
                  pl.BlockSpec((len(DILATION_PAIRS), None, ATT_SUB, 2 * ATT_SUB), lambda i, h: (0, h, 0, 0)),
                  vec, vec],
        out_specs=[blk(0), kblk, kblk],
        out_shape=[jax.ShapeDtypeStruct((n, t, d_att), BF16)] + [jax.ShapeDtypeStruct((n, keep, d_att), F32)] * 2,
        scratch_shapes=[pltpu.VMEM((t, HEAD_DIM_A), F32)] * 5,
        compiler_params=_cparams(("parallel", "parallel")),
        name="attn_prompt",
    )(pm3, pm3, pm3, bias, qg.reshape(1, -1), kg.reshape(1, -1))


ROWS_PAD = 8


def _sample_bias(rel_bias, dseq):
    sub = ATT_SUB
    nh = rel_bias.shape[1]
    rb = lambda dist: _bias_rows(rel_bias, dist)
    c = jnp.arange(sub)
    tabs = [rb(dil * (sub - c)) for _, dil in reversed(DILATION_PAIRS[1:])]
    for s in range(dseq):
        tabs.append(jnp.where((c >= s)[:, None], rb(jnp.clip(sub + s - c, 0, sub)), NEG_INF))
    cache = jnp.broadcast_to(jnp.stack(tabs)[..., None], (len(tabs), sub, nh, LANE))
    s = jnp.arange(dseq)[:, None]
    sp = jnp.arange(dseq)[None, :]
    new = []
    for _, dil in DILATION_PAIRS:
        ok = (sp <= s) if dil == 1 else (sp == s)
        new.append(jnp.where(ok[..., None], rb(dil * jnp.clip(s - sp, 0, sub)), NEG_INF))
    new = jnp.broadcast_to(jnp.stack(new)[..., None], (len(new), dseq, dseq, nh, LANE))
    return cache, new


def _attn_sample_kernel(x_ref, k16_ref, k4_ref, v16_ref, v4_ref, bc_ref, bn_ref, qg_ref, kg_ref,
                        o_ref, kn_ref, *, dseq):
    nh, dh = x_ref.shape[-2], x_ref.shape[-1]
    sub = ATT_SUB
    d4, d16 = DILATION_PAIRS[1][1], DILATION_PAIRS[2][1]
    far = k16_ref.shape[0]
    qn = _head_rms(x_ref[0], qg_ref[...]) * (1.0 / math.sqrt(dh))
    kn = _head_rms(x_ref[1], kg_ref[...])
    vn = x_ref[2]
    kn_ref[...] = kn
    ones = jnp.ones((dh, LANE), F32)
    tail = sub // d4

    def lane_sum(x):
        keys = x.shape[0]
        return _dot(x.reshape(keys * nh, dh), ones).reshape(keys, nh, LANE)

    outs = []
    for s in range(dseq):
        q = qn[s]
        k1 = k4_ref[sub - tail:sub].reshape(sub, nh, dh)
        v1 = v4_ref[sub - tail:sub].reshape(sub, nh, dh)
        near = pl.ds(0, sub - far, stride=d16 // d4)
        cache = [(k16_ref[:, s], v16_ref[:, s], bc_ref[0, 0:far]),
                 (k4_ref[near, s], v4_ref[near, s], bc_ref[0, far:sub]),
                 (k4_ref[:, s], v4_ref[:, s], bc_ref[1]), (k1, v1, bc_ref[2 + s])]
        logits = [lane_sum(kk * q[None]) + bias for kk, _, bias in cache]
        new = []
        for s2 in range(s + 1):
            ln = jnp.broadcast_to(jnp.sum(q * kn[s2], axis=-1, keepdims=True), (nh, LANE))
            new += [ln + bn_ref[g, s, s2] for g in range(len(DILATION_PAIRS))]
        m = functools.reduce(jnp.maximum, [jnp.max(x, axis=0) for x in logits] + new)
        p = [jnp.exp(x - m[None]) for x in logits]
        pn = [jnp.exp(x - m) for x in new]
        denom = functools.reduce(jnp.add, [jnp.sum(x, axis=0) for x in p] + pn)
        o = functools.reduce(jnp.add, [jnp.sum(pi * vv, axis=0) for pi, (_, vv, _) in zip(p, cache)])
        for s2 in range(s + 1):
            w = functools.reduce(jnp.add, pn[s2 * len(DILATION_PAIRS):(s2 + 1) * len(DILATION_PAIRS)])
            o = o + w * vn[s2]
        outs.append(o / denom)
    outs += [jnp.zeros((nh, dh), F32)] * (ROWS_PAD - dseq)
    o_ref[...] = jnp.stack(outs)


def _attn_sample(x5, cache_k, cache_v, rel_bias, qg, kg, dseq):
    n, wb, nh, dh = cache_k.shape
    d4, d16 = DILATION_PAIRS[1][1], DILATION_PAIRS[2][1]
    assert wb == MAX_WINDOW == d16 * ATT_SUB and dseq <= d4 and DILATION_PAIRS[0][1] == 1 and dh == LANE
    bias_c, bias_n = _sample_bias(rel_bias, dseq)
    view = lambda c, d: c.reshape(n, wb // d, d, nh, dh)
    far = ATT_SUB - ATT_SUB * d4 // d16
    s16 = pl.BlockSpec((None, far, dseq, nh, dh), lambda i: (i, 0, 0, 0, 0))
    s4 = pl.BlockSpec((None, ATT_SUB, d4, nh, dh), lambda i: (i, wb // d4 // ATT_SUB - 1, 0, 0, 0))
    row_spec = pl.BlockSpec((None, ROWS_PAD, nh, dh), lambda i: (i, 0, 0, 0))
    vec = pl.BlockSpec((1, dh), lambda i: (0, 0))
    const = lambda a: pl.BlockSpec(a.shape, lambda i: (0,) * a.ndim, pipeline_mode=pl.Buffered(1))
    return pl.pallas_call(
        functools.partial(_attn_sample_kernel, dseq=dseq),
        grid=(n,),
        in_specs=[pl.BlockSpec((None, 3, ROWS_PAD, nh, dh), lambda i: (i, 0, 0, 0, 0)), s16, s4, s16, s4,
                  const(bias_c), const(bias_n), vec, vec],
        out_specs=[row_spec, row_spec],
        out_shape=[jax.ShapeDtypeStruct((n, ROWS_PAD, nh, dh), F32)] * 2,
        compiler_params=_cparams(("parallel",)),
        name="attn_sample",
    )(x5, view(cache_k, d16), view(cache_k, d4), view(cache_v, d16), view(cache_v, d4), bias_c, bias_n,
      qg.reshape(1, -1), kg.reshape(1, -1))


def _prep_weights(lp):
    (g_mix, w_in, q_norm_g, k_norm_g, mu_shift, w0, w2, a0, a2, g2, k_k, k_a, r_k, ln_x_w, ln_x_b, w_out,
     g_ffn, w_gate, w_up, conv_w, conv_b, w_down, g_ple, w_ple, w_ple_gate) = lp
    d_model = w_in.shape[0]
    d_rwkv = w0.shape[0]
    d_att = w_out.shape[0] - d_rwkv
    main = 3 * d_att + 3 * d_rwkv
    lora = w_in.shape[1] - main
    lora_p = _round_up(lora, LANE)
    d_ff = w_gate.shape[1]
    dffp = _round_up(d_ff, FF_ALIGN)
    assert main % lora_p == 0 and main % PROJ_TN == 0
    padc = lambda w, n: jnp.pad(w, ((0, 0), (0, n - w.shape[1])))
    wp = dict(
        d_model=d_model, d_att=d_att, d_rwkv=d_rwkv, lora=lora, lora_p=lora_p, lora_blk=main // lora_p,
        d_ff=d_ff, dffp=dffp,
        g_mix=g_mix, g_ffn=g_ffn, g_ple=g_ple, q_norm_g=q_norm_g, k_norm_g=k_norm_g,
        mu_shift=mu_shift, w0=w0, w2=w2, a0=a0, a2=a2, g2=g2, k_k=k_k, k_a=k_a, r_k=r_k,
        ln_x_w=ln_x_w, ln_x_b=ln_x_b,
        conv_w=padc(conv_w, dffp), conv_b=jnp.pad(conv_b, (0, dffp - d_ff)), w_ple=w_ple.astype(BF16),
    )
    big = dict(w_in_t=jnp.swapaxes(w_in, 0, 1), w_out=w_out, w_gate=w_gate, w_up=w_up, w_down=w_down,
               w_ple_gate=w_ple_gate)
    return wp, big


PROJ_TN = 512


def _layer(x2d, p2d, wp, big, mixer, *, tm, tf, down, shift, conv_prev, tiles_per_group):
    d_model, dffp = wp["d_model"], wp["dffp"]
    emit = big["w_in_t"].dtype != BF16
    w16 = dict(big)

    def mm(name, a, key, tm=tm, **kw):
        res = _matmul(a, big[key], tm=tm, name=name, **kw)
        if emit:
            res, w16[key] = res
        return res

    xn = _rmsnorm(x2d, wp["g_mix"], min(tm, 256))
    proj = mm("in_proj", xn, "w_in_t", tn=PROJ_TN, tk=d_model, b_t=True)
    mix, aux = mixer(proj)
    h1 = mm("out_proj", mix, "w_out", tn=PROJ_TN, tk=mix.shape[1], mode="residual", extras=(x2d,))
    hn = _rmsnorm(h1, wp["g_ffn"], min(tm, 256))
    cprev = jnp.pad(conv_prev, ((0, 0), (0, 0), (0, dffp - conv_prev.shape[-1])))
    res = _ffn_act(hn, big["w_gate"], big["w_up"], wp["conv_w"], wp["conv_b"], cprev,
                   tm=tm, tf=tf, shift=shift, tiles_per_group=tiles_per_group)
    act, conv_tail = res[0], res[1]
    if emit:
        w16["w_gate"], w16["w_up"] = res[2], res[3]
    h2 = mm("ffn_down", act, "w_down", mode="residual", extras=(h1,), **down)
    hn2 = _rmsnorm(h2, wp["g_ple"], min(tm, 256))
    y = mm("ple_gate", hn2, "w_ple_gate", tn=PROJ_TN, tk=d_model, mode="ple",
           extras=(h2, p2d.astype(BF16), wp["w_ple"]))
    conv_tail = conv_tail[tiles_per_group - 1::tiles_per_group, :, :wp["d_ff"]]
    return y, conv_tail, aux, w16


def kernel(x_prompt, x_sample, cache_k, cache_v, state_shift, state_wkv, state_conv, p_prompt, p_sample, rel_bias, g_mix, w_in, q_norm_g, k_norm_g, mu_shift, w0, w2, a0, a2, g2, k_k, k_a, r_k, ln_x_w, ln_x_b, w_out, g_ffn, w_gate, w_up, conv_w, conv_b, w_down, g_ple, w_ple, w_ple_gate):
    depth = g_mix.shape[0]
    nbp, seq, d_model = x_prompt.shape
    nbs, dseq, _ = x_sample.shape
    hp = x_prompt.reshape(nbp * seq, d_model)
    hs = jnp.swapaxes(x_sample, 0, 1).reshape(dseq * nbs, d_model)
    outs_p, outs_s = [], []
    for i in range(depth):
        lp = (g_mix[i], w_in[i], q_norm_g[i], k_norm_g[i], mu_shift[i], w0[i], w2[i], a0[i], a2[i], g2[i],
              k_k[i], k_a[i], r_k[i], ln_x_w[i], ln_x_b[i], w_out[i], g_ffn[i], w_gate[i], w_up[i],
              conv_w[i], conv_b[i], w_down[i], g_ple[i], w_ple[i], w_ple_gate[i])
        wp, big = _prep_weights(lp)
        d_att, d_rwkv, lora, d_ff, dffp = wp["d_att"], wp["d_rwkv"], wp["lora"], wp["d_ff"], wp["dffp"]
        nha = d_att // HEAD_DIM_A
        nhb = d_rwkv // HEAD_DIM_B
        qg, kg = q_norm_g[i], k_norm_g[i]
        state_cols = slice(3 * d_att, 3 * d_att + 3 * d_rwkv + lora)

        def mixer_prompt(proj, shift_prev, wkv_prev):
            pm = proj.reshape(nbp, seq, -1)
            o_att, k_keep, v_keep = _attn_prompt(pm, rel_bias, qg, kg, d_att)
            o_rwkv, wkv_new = _rwkv_chunked(pm, shift_prev, wkv_prev, wp, tb=min(seq, 512),
                                            nb=2 if nbp % 2 == 0 else 1)
            heads = lambda z: z.reshape(nbp, -1, nha, HEAD_DIM_A)
            mix = jnp.concatenate([o_att, o_rwkv], axis=-1).reshape(nbp * seq, -1)
            return mix, (heads(k_keep), heads(v_keep), pm[:, -1, state_cols], wkv_new)

        def mixer_sample(proj, ck, cv, shift_prev, wkv_prev):
            pm = proj.reshape(dseq, nbs, -1)
            x5 = jnp.transpose(pm[..., :3 * d_att].reshape(dseq, nbs, 3, nha, HEAD_DIM_A), (1, 2, 0, 3, 4))
            x5 = jnp.pad(x5, ((0, 0), (0, 0), (0, ROWS_PAD - dseq), (0, 0), (0, 0)))
            o_att, kn = _attn_sample(x5, ck, cv, rel_bias, qg, kg, dseq)
            o_att = jnp.swapaxes(o_att[:, :dseq], 0, 1).reshape(dseq * nbs, d_att).astype(BF16)
            o_rwkv, wkv_new = _rwkv_steps(proj, shift_prev, wkv_prev, wp, nseq=nbs, dseq=dseq)
            mix = jnp.concatenate([o_att, o_rwkv], axis=-1)
            v_new = jnp.swapaxes(pm[..., 2 * d_att:3 * d_att], 0, 1).reshape(nbs, dseq, nha, HEAD_DIM_A)
            return mix, (kn[:, :dseq], v_new, pm[-1, :, state_cols], wkv_new)

        mix_s = functools.partial(mixer_sample, ck=cache_k[i], cv=cache_v[i],
                                  shift_prev=state_shift[i], wkv_prev=state_wkv[i])
        conv_prev_s = jnp.swapaxes(state_conv[i], 0, 1).reshape(1, (CONV_WIDTH - 1) * nbs, d_ff)
        p_s = jnp.swapaxes(p_sample[i], 0, 1).reshape(dseq * nbs, -1)
        down_s = dict(tm=dseq * nbs, tn=min(1024, d_model), tk=2 * FF_ALIGN if dffp % (2 * FF_ALIGN) == 0 else FF_ALIGN)
        hs, conv_s, aux_s, w16 = _layer(hs, p_s, wp, big, mix_s, tm=dseq * nbs, tf=FF_ALIGN // 2,
                                        down=down_s, shift=nbs, conv_prev=conv_prev_s, tiles_per_group=1)
        conv_s = jnp.swapaxes(conv_s.reshape(CONV_WIDTH - 1, nbs, d_ff), 0, 1)
        outs_s.append((*aux_s, conv_s))

        tm_p = min(1024, seq)
        zero_shift = jnp.zeros((nbp, 3 * d_rwkv + lora), F32)
        zero_wkv = jnp.zeros((nbp, nhb, HEAD_DIM_B, HEAD_DIM_B), F32)
        zero_conv = jnp.zeros((nbp, CONV_WIDTH - 1, d_ff), F32)
        mix_p = functools.partial(mixer_prompt, shift_prev=zero_shift, wkv_prev=zero_wkv)
        down_p = dict(tm=tm_p // 2, tn=PROJ_TN, tk=dffp)
        hp, conv_p, aux_p, _ = _layer(hp, p_prompt[i].reshape(nbp * seq, -1), wp, w16, mix_p, tm=tm_p, tf=FF_ALIGN,
                                      down=down_p, shift=1, conv_prev=zero_conv,
                                      tiles_per_group=seq // tm_p)
        outs_p.append((*aux_p, conv_p))

    y_p = hp.reshape(nbp, seq, d_model)
    y_s = jnp.swapaxes(hs.reshape(dseq, nbs, d_model), 0, 1)
    st = lambda outs, idx: jnp.stack([o[idx] for o in outs])
    return (y_p, y_s, st(outs_p, 0), st(outs_p, 1), st(outs_p, 2), st(outs_p, 3), st(outs_p, 4),
            st(outs_s, 0), st(outs_s, 1), st(outs_s, 2), st(outs_s, 3), st(outs_s, 4))
```

```python
import functools
import math

import jax
import jax.numpy as jnp
from jax import lax
from jax.experimental import pallas as pl
from jax.experimental.pallas import tpu as pltpu

F32 = jnp.float32
BF16 = jnp.bfloat16

HEAD_DIM_A = 128
HEAD_DIM_B = 64
DILATION_PAIRS = ((128, 1), (512, 4), (2048, 16))
MAX_WINDOW = max(w for w, _ in DILATION_PAIRS)
NUM_BUCKETS = 32
REL_MAX_DIST = MAX_WINDOW
DECAY_LORA = 128
AAA_LORA = 128
CONV_WIDTH = 3
RMS_EPS = 1e-6
GN_EPS = 64e-5
NEG_INF = -1e30

LANE = 128
FF_ALIGN = 512
VMEM_LIMIT = 56 * 1024 * 1024


def _cparams(sem):
    return pltpu.CompilerParams(dimension_semantics=sem, vmem_limit_bytes=VMEM_LIMIT)


def _round_up(x, m):
    return -(-x // m) * m


def _dot(a, b, prec=None):
    return jnp.dot(a, b, preferred_element_type=F32, precision=prec)


def _dot_nt(a, b, prec=None):
    return lax.dot_general(a, b, (((1,), (1,)), ((), ())), preferred_element_type=F32, precision=prec)


def _bdot(a, b):
    return _dot(a.astype(BF16), b.astype(BF16))


def _rms_kernel(x_ref, g_ref, o_ref):
    x = x_ref[...]
    ms = jnp.mean(x * x, axis=-1, keepdims=True)
    o_ref[...] = (x * lax.rsqrt(ms + RMS_EPS) * g_ref[...]).astype(o_ref.dtype)


def _rmsnorm(x, g, tm):
    m, d = x.shape
    return pl.pallas_call(
        _rms_kernel,
        grid=(m // tm,),
        in_specs=[pl.BlockSpec((tm, d), lambda i: (i, 0)), pl.BlockSpec((1, d), lambda i: (0, 0))],
        out_specs=pl.BlockSpec((tm, d), lambda i: (i, 0)),
        out_shape=jax.ShapeDtypeStruct((m, d), BF16),
        compiler_params=_cparams(("parallel",)),
        name="rmsnorm",
    )(x, g.reshape(1, d))


_N_EXTRA = {"plain": 0, "residual": 1, "ple": 3}


def _cast_tile(b, row0, col0, k_valid, n_valid):
    b16 = b.astype(BF16)
    if k_valid is not None:
        r = row0 + lax.broadcasted_iota(jnp.int32, (b.shape[0], 1), 0)
        b16 = jnp.where(r < k_valid, b16, jnp.zeros_like(b16))
    if n_valid is not None:
        c = col0 + lax.broadcasted_iota(jnp.int32, (1, b.shape[1]), 1)
        b16 = jnp.where(c < n_valid, b16, jnp.zeros_like(b16))
    return b16


def _mm_kernel(*refs, nk, mode, emit, k_valid, n_valid, b_t, n_a):
    a_refs, b_ref = refs[:n_a], refs[n_a]
    refs = refs[n_a - 1:]
    n_extra = _N_EXTRA[mode]
    o_ref = refs[2 + n_extra]
    acc_ref = refs[-1] if nk > 1 else None

    def epilogue(acc):
        if mode == "plain":
            return acc
        if mode == "residual":
            return refs[2][...] + acc
        h_ref, p_ref, wp_ref = refs[2], refs[3], refs[4]
        ple = _dot(p_ref[...].astype(BF16), wp_ref[...])
        return h_ref[...] + ple * jax.nn.sigmoid(acc)

    b = b_ref[...]
    if emit and b_t:
        tn, tk = b.shape
        b = _cast_tile(b, pl.program_id(1) * tn, pl.program_id(2) * tk, n_valid, k_valid)
    elif emit:
        tk, tn = b.shape
        b = _cast_tile(b, pl.program_id(2) * tk, pl.program_id(1) * tn, k_valid, n_valid)
    if emit:
        refs[3 + n_extra][...] = b
    if n_a == 1:
        part = _dot_nt(a_refs[0][...], b) if b_t else _dot(a_refs[0][...], b)
    else:
        part, k0 = 0.0, 0
        for a_ref in a_refs:
            part = part + _dot(a_ref[...], b[k0:k0 + a_ref.shape[1]])
            k0 += a_ref.shape[1]
    if nk == 1:
        o_ref[...] = epilogue(part).astype(o_ref.dtype)
        return
    k = pl.program_id(2)

    @pl.when(k == 0)
    def _():
        acc_ref[...] = part

    @pl.when(k > 0)
    def _():
        acc_ref[...] += part

    @pl.when(k == nk - 1)
    def _():
        o_ref[...] = epilogue(acc_ref[...]).astype(o_ref.dtype)


def _matmul(a, b, *, tm, tn, tk, mode="plain", extras=(), b_t=False, name="matmul"):
    a_ops = a if isinstance(a, tuple) else (a,)
    m, kp = a_ops[0].shape[0], sum(x.shape[1] for x in a_ops)
    assert len(a_ops) == 1 or (tk == kp and not b_t)
    emit = b.dtype != BF16
    kb, nb_ = (b.shape[1], b.shape[0]) if b_t else b.shape
    np_ = _round_up(nb_, tn)
    assert kp % tk == 0 and m % tm == 0 and (emit or (kb, nb_) == (kp, np_)) and not (emit and m != tm)
    nk = kp // tk
    k_valid = kb if emit and kb != kp else None
    n_valid = nb_ if emit and nb_ != np_ else None
    last_k, last_j = pl.cdiv(kb, tk) - 1, pl.cdiv(nb_, tn) - 1
    if b_t:
        b_spec = pl.BlockSpec((tn, tk), lambda i, j, k: (jnp.minimum(j, last_j), jnp.minimum(k, last_k)))
        w16_spec, w16_shape = pl.BlockSpec((tn, tk), lambda i, j, k: (j, k)), (np_, kp)
    else:
        b_spec = pl.BlockSpec((tk, tn), lambda i, j, k: (jnp.minimum(k, last_k), jnp.minimum(j, last_j)))
        w16_spec, w16_shape = pl.BlockSpec((tk, tn), lambda i, j, k: (k, j)), (kp, np_)
    if len(a_ops) == 1:
        in_specs = [pl.BlockSpec((tm, tk), lambda i, j, k: (i, k)), b_spec]
    else:
        in_specs = [pl.BlockSpec((tm, x.shape[1]), lambda i, j, k: (i, 0)) for x in a_ops] + [b_spec]
    if mode == "residual":
        in_specs.append(pl.BlockSpec((tm, tn), lambda i, j, k: (i, j)))
    elif mode == "ple":
        pdim = extras[1].shape[1]
        in_specs += [pl.BlockSpec((tm, tn), lambda i, j, k: (i, j)),
                     pl.BlockSpec((tm, pdim), lambda i, j, k: (i, 0)),
                     pl.BlockSpec((pdim, tn), lambda i, j, k: (0, j))]
    out_specs = [pl.BlockSpec((tm, tn), lambda i, j, k: (i, j))]
    out_shape = [jax.ShapeDtypeStruct((m, np_), F32)]
    if emit:
        out_specs.append(w16_spec)
        out_shape.append(jax.ShapeDtypeStruct(w16_shape, BF16))
    scratch = [pltpu.VMEM((tm, tn), F32)] if nk > 1 else []
    res = pl.pallas_call(
        functools.partial(_mm_kernel, nk=nk, mode=mode, emit=emit, k_valid=k_valid, n_valid=n_valid, b_t=b_t,
                          n_a=len(a_ops)),
        grid=(m // tm, np_ // tn, nk),
        in_specs=in_specs,
        out_specs=out_specs,
        out_shape=out_shape,
        scratch_shapes=scratch,
        compiler_params=_cparams(("parallel", "parallel", "arbitrary")),
        name=name,
    )(*a_ops, b, *extras)
    return tuple(res) if emit else res[0]


FFN_SUB = 256


def _ffn_kernel(a_ref, wg_ref, wu_ref, cw_ref, cb_ref, prev_ref, act_ref, tail_ref, *rest,
                tm, tf, shift, tiles_per_group, emit, n_valid):
    i = pl.program_id(0)
    j = pl.program_id(1)
    hist = (CONV_WIDTH - 1) * shift
    pad = _round_up(hist, 8)
    ext_ref, carry_ref = rest[-2], rest[-1]
    a = a_ref[...]
    sw = min(tf, FFN_SUB)
    subs = [slice(s, s + sw) for s in range(0, tf, sw)]

    def weight(w_ref, w16_ref, cols):
        w = w_ref[:, cols]
        if emit:
            w = _cast_tile(w, 0, j * tf + cols.start, None, n_valid)
            w16_ref[:, cols] = w
        return w

    @pl.when((i == 0) & (j == 0))
    def _():
        ext_ref[...] = jnp.zeros(ext_ref.shape, F32)

    @pl.when(i % tiles_per_group == 0)
    def _():
        ext_ref[pl.ds(pad - hist, hist), :] = prev_ref[0]

    @pl.when(i % tiles_per_group != 0)
    def _():
        ext_ref[pl.ds(pad - hist, hist), :] = carry_ref[j]

    row8 = lax.broadcasted_iota(jnp.int32, (8, 1), 0)

    def delayed(u, k, cols):
        off = k * shift
        if off % 8 == 0:
            return jnp.concatenate([ext_ref[pl.ds(pad - off, off), cols], u[0:tm - off]], axis=0)
        rolled = pltpu.roll(u, off, 0)
        head = jnp.where(row8 < off, ext_ref[pl.ds(pad - off, 8), cols], rolled[0:8])
        return jnp.concatenate([head, rolled[8:]], axis=0)

    for cols in subs:
        u = _dot(a, weight(wg_ref, rest[0], cols))
        up = _dot(a, weight(wu_ref, rest[1], cols))
        c = (cb_ref[:, cols] + cw_ref[0:1, cols] * delayed(u, 2, cols) + cw_ref[1:2, cols] * delayed(u, 1, cols)
             + cw_ref[2:3, cols] * u)
        act_ref[:, cols] = (jax.nn.silu(c) * up).astype(act_ref.dtype)
        ext_ref[pl.ds(pad, pad), cols] = u[tm - pad:tm]
    tail = ext_ref[pl.ds(2 * pad - hist, hist), :]
    carry_ref[j] = tail
    tail_ref[0] = tail


def _ffn_act(xn, wg, wu, conv_w, conv_b, conv_prev, *, tm, tf, shift, tiles_per_group):
    m, d = xn.shape
    dffp = conv_w.shape[1]
    emit = wg.dtype != BF16
    assert dffp % tf == 0 and (emit or wg.shape[1] == dffp) and not (emit and m != tm)
    hist = (CONV_WIDTH - 1) * shift
    nj = dffp // tf
    n_valid = wg.shape[1] if emit and wg.shape[1] != dffp else None
    kern = functools.partial(_ffn_kernel, tm=tm, tf=tf, shift=shift, tiles_per_group=tiles_per_group,
                             emit=emit, n_valid=n_valid)
    last = pl.cdiv(wg.shape[1], tf) - 1
    wspec = pl.BlockSpec((d, tf), lambda i, j: (0, jnp.minimum(j, last)))
    out_specs = [pl.BlockSpec((tm, tf), lambda i, j: (i, j)), pl.BlockSpec((1, hist, tf), lambda i, j: (i, 0, j))]
    out_shape = [jax.ShapeDtypeStruct((m, dffp), BF16), jax.ShapeDtypeStruct((m // tm, hist, dffp), F32)]
    if emit:
        out_specs += [pl.BlockSpec((d, tf), lambda i, j: (0, j))] * 2
        out_shape += [jax.ShapeDtypeStruct((d, dffp), BF16)] * 2
    return pl.pallas_call(
        kern,
        grid=(m // tm, nj),
        in_specs=[
            pl.BlockSpec((tm, d), lambda i, j: (i, 0)), wspec, wspec,
            pl.BlockSpec((CONV_WIDTH, tf), lambda i, j: (0, j)),
            pl.BlockSpec((1, tf), lambda i, j: (0, j)),
            pl.BlockSpec((1, hist, tf), lambda i, j: (i // tiles_per_group, 0, j)),
        ],
        out_specs=out_specs,
        out_shape=out_shape,
        scratch_shapes=[pltpu.VMEM((2 * _round_up(hist, 8), tf), F32), pltpu.VMEM((nj, hist, tf), F32)],
        compiler_params=_cparams(("arbitrary", "arbitrary")),
        name="ffn_gate_up_conv",
    )(xn, wg, wu, conv_w, conv_b.reshape(1, dffp), conv_prev)


RWKV_CHUNK = 64


def _softplus(z):
    return jnp.maximum(z, 0.0) + jnp.log(1.0 + jnp.exp(-jnp.abs(z)))


def _pair_blockdiag():
    li = lax.broadcasted_iota(jnp.int32, (LANE, LANE), 0)
    lj = lax.broadcasted_iota(jnp.int32, (LANE, LANE), 1)
    return (li // HEAD_DIM_B == lj // HEAD_DIM_B).astype(F32)


def _head_sum(x, blockdiag):
    bd = blockdiag.astype(BF16)
    hi = x.astype(BF16)
    lo = (x - hi.astype(F32)).astype(BF16)
    return _dot(hi, bd) + _dot(lo, bd)


def _lora_act_kernel(x_ref, first_ref, mu_ref, o_ref, *, shift):
    x = x_ref[...]
    rows = x.shape[0]
    if shift % 8 == 0:
        prev = jnp.concatenate([first_ref[...], x[0:rows - shift]], axis=0)
    else:
        row = lax.broadcasted_iota(jnp.int32, (rows, 1), 0)
        prev = jnp.where(row < shift, first_ref[...], pltpu.roll(x, shift, 0))
    xs = x + mu_ref[...] * (prev - x)
    c1, c2 = DECAY_LORA, DECAY_LORA + AAA_LORA
    o_ref[:, 0:c1] = jnp.tanh(xs[:, 0:c1]).astype(o_ref.dtype)
    o_ref[:, c1:c2] = xs[:, c1:c2].astype(o_ref.dtype)
    o_ref[:, c2:] = jax.nn.sigmoid(xs[:, c2:]).astype(o_ref.dtype)


def _lora_act(x3, first3, wp, *, shift):
    g, r, _ = x3.shape
    lora, lora_p, lblk = wp["lora"], wp["lora_p"], wp["lora_blk"]
    assert shift == 1 or shift % 8 == 0
    first = jnp.pad(first3, ((0, 0), (0, 0), (0, lora_p - lora)))
    return pl.pallas_call(
        functools.partial(_lora_act_kernel, shift=shift),
        grid=(g,),
        in_specs=[pl.BlockSpec((None, r, lora_p), lambda i: (i, 0, lblk)),
                  pl.BlockSpec((None, shift, lora_p), lambda i: (i, 0, 0)),
                  pl.BlockSpec((1, lora_p), lambda i: (0, 0))],
        out_specs=pl.BlockSpec((None, r, lora_p), lambda i: (i, 0, 0)),
        out_shape=jax.ShapeDtypeStruct((g, r, lora_p), BF16),
        compiler_params=_cparams(("parallel",)),
        name="rwkv_lora_act",
    )(x3, first, _mu_split(wp)[1])


def _rwkv_pre(xk, la, prm, blockdiag):
    w0_ref, a0_ref, w2_ref, a2_ref, g2_ref, kk_ref, ka_ref = prm[:7]
    c1, c2 = DECAY_LORA, DECAY_LORA + AAA_LORA
    wl = w0_ref[...] + _dot(la[:, 0:c1], w2_ref[...])
    lw = -jnp.exp(-_softplus(-wl) - 0.5)
    a = jax.nn.sigmoid(a0_ref[...] + _dot(la[:, c1:c2], a2_ref[...]))
    g = _dot(la[:, c2:], g2_ref[...])
    kk = xk * kk_ref[...]
    kk = kk / jnp.maximum(jnp.sqrt(_head_sum(kk * kk, blockdiag)), 1e-12)
    kmod = xk * (1.0 + (a - 1.0) * ka_ref[...])
    return lw, -kk, kk * a, kmod, g


def _rwkv_post(y, xr, kmod, xv, g, prm, blockdiag):
    rk_ref, lnw_ref, lnb_ref = prm[7:10]
    inv = 1.0 / HEAD_DIM_B
    mean = _head_sum(y, blockdiag) * inv
    d = y - mean
    var = _head_sum(d * d, blockdiag) * inv
    yn = d * lax.rsqrt(var + GN_EPS) * lnw_ref[...] + lnb_ref[...]
    bonus = _head_sum(xr * kmod * rk_ref[...], blockdiag) * xv
    return (yn + bonus) * g


def _rwkv_chunk_kernel(pr_ref, pk_ref, pv_ref, la_ref, spr_ref, spk_ref, spv_ref,
                       mur_ref, muk_ref, muv_ref, *rest, tb, nb):
    prm = rest[:10]
    z0_ref, o_ref, zout_ref, z_s, cr_s, ck_s, cv_s = rest[10:]
    ti = pl.program_id(2)
    nt = pl.num_programs(2)
    cs = RWKV_CHUNK
    hd = HEAD_DIM_B

    lane = lax.broadcasted_iota(jnp.int32, (1, LANE), 1)
    m0 = (lane < hd).astype(F32)
    m1 = 1.0 - m0
    blockdiag = _pair_blockdiag()
    li = lax.broadcasted_iota(jnp.int32, (LANE, LANE), 0)
    lj = lax.broadcasted_iota(jnp.int32, (LANE, LANE), 1)
    eye = (li == lj).astype(F32)
    same = li // cs == lj // cs
    mask_incl = (same & (li >= lj)).astype(F32)
    mask_strict = (same & (li > lj)).astype(F32)
    ci = lax.broadcasted_iota(jnp.int32, (cs, cs), 0)
    cj = lax.broadcasted_iota(jnp.int32, (cs, cs), 1)
    tril_incl = (ci >= cj).astype(F32)
    row = lax.broadcasted_iota(jnp.int32, (tb, 1), 0)

    @pl.when(ti == 0)
    def _():
        z_s[...] = jnp.zeros(z_s.shape, F32)
        for q in range(nb):
            z_s[q, 0:hd, 0:hd] = z0_ref[q, 0]
            z_s[q, hd:2 * hd, hd:2 * hd] = z0_ref[q, 1]
        cr_s[...] = spr_ref[...]
        ck_s[...] = spk_ref[...]
        cv_s[...] = spv_ref[...]

    def shifted(x, carry_row, mu):
        prev = jnp.where(row == 0, carry_row, pltpu.roll(x, 1, 0))
        return x + mu * (prev - x)

    two = lambda x: jnp.concatenate([x * m0, x * m1], axis=0)
    fold = lambda x: x[0:cs] + x[cs:2 * cs]

    seqs = []
    for q in range(nb):
        pr, pk, pv = pr_ref[q], pk_ref[q], pv_ref[q]
        xr = shifted(pr, cr_s[q], mur_ref[...])
        xk = shifted(pk, ck_s[q], muk_ref[...])
        xv = shifted(pv, cv_s[q], muv_ref[...])
        cr_s[q] = pr[tb - 1:tb]
        ck_s[q] = pk[tb - 1:tb]
        cv_s[q] = pv[tb - 1:tb]
        lw, aneg, bb, kmod, g = _rwkv_pre(xk, la_ref[q], prm, blockdiag)
        seqs.append(dict(xr=xr, xv=xv, lw=lw, aneg=aneg, bb=bb, kmod=kmod, g=g))

    nchunk = tb // cs
    units = []
    for q in range(nb):
        for c in range(nchunk):
            rows = slice(c * cs, (c + 1) * cs)
            units.append({k: v[rows] for k, v in seqs[q].items() if k != "g"})
    tril16 = tril_incl.astype(BF16)
    for u in units:
        lw1 = u["lw"].astype(BF16)
        r1 = u["lw"] - lw1.astype(F32)
        lw2 = r1.astype(BF16)
        lw3 = (r1 - lw2.astype(F32)).astype(BF16)
        u["cum"] = _dot(tril16, lw1) + (_dot(tril16, lw2) + _dot(tril16, lw3))
    for u in units:
        cum = u["cum"]
        tot = cum[cs - 1:cs]
        e_inv = jnp.exp(-cum)
        e_end = jnp.exp(tot - cum)
        u["rt"] = u["xr"] * jnp.exp(cum)
        u["la"] = two(u["aneg"] * jnp.exp(cum - u["lw"]))
        u["mt"] = jnp.concatenate([u["bb"] * e_end, u["kmod"] * e_end], axis=0).T
        u["pc_col"] = jnp.broadcast_to(jnp.exp(tot), (LANE, LANE)).T
        u["gm"] = _dot_nt(jnp.concatenate([u["la"], two(u["rt"])], axis=0).astype(BF16),
                          jnp.concatenate([two(u["bb"] * e_inv), two(u["kmod"] * e_inv)], axis=0).astype(BF16))
    for u in units:
        gm = u.pop("gm")
        u["n_pow"] = gm[0:LANE, 0:LANE] * mask_strict
        u["tinv"] = eye + u["n_pow"]
        u["arb"] = gm[LANE:2 * LANE, 0:LANE] * mask_incl
        u["av"] = _bdot(jnp.concatenate([gm[0:LANE, LANE:2 * LANE] * mask_strict,
                                         gm[LANE:2 * LANE, LANE:2 * LANE] * mask_incl], axis=0), two(u["xv"]))
    for _ in range(int(math.log2(cs)) - 1):
        for u in units:
            u["n_pow"] = _bdot(u["n_pow"], u["n_pow"])
        for u in units:
            u["tinv"] = u["tinv"] + _bdot(u["tinv"], u["n_pow"])
    for u in units:
        wu = _bdot(u["tinv"], jnp.concatenate([u["la"], u["av"][0:LANE]], axis=1))
        u["w"] = fold(wu[:, 0:LANE])
        u["u0"] = fold(wu[:, LANE:2 * LANE])
        u["y0"] = fold(u["av"][LANE:2 * LANE])
    for u in units:
        vc = u["xv"]
        zz = _bdot(u["mt"], jnp.concatenate([jnp.concatenate([u["w"], u["u0"]], axis=1),
                                             jnp.concatenate([jnp.zeros_like(vc), vc], axis=1)], axis=0))
        u["zm"] = zz[:, 0:LANE] * blockdiag
        u["zc"] = zz[:, LANE:2 * LANE] * blockdiag

    zs = [z_s[q] for q in range(nb)]
    ys = [[] for _ in range(nb)]
    for c in range(nchunk):
        yus = [_bdot(jnp.concatenate([units[q * nchunk + c]["rt"], units[q * nchunk + c]["w"]], axis=0), zs[q])
               for q in range(nb)]
        zms = [_bdot(units[q * nchunk + c]["zm"], zs[q]) for q in range(nb)]
        for q in range(nb):
            u = units[q * nchunk + c]
            uu = yus[q][cs:2 * cs] + u["u0"]
            ys[q].append(yus[q][0:cs] + u["y0"] + fold(_bdot(u["arb"], two(uu))))
            zs[q] = u["pc_col"] * zs[q] + zms[q] + u["zc"]
    for q in range(nb):
        z_s[q] = zs[q]
        s = seqs[q]
        y = jnp.concatenate(ys[q], axis=0)
        o_ref[q] = _rwkv_post(y, s["xr"], s["kmod"], s["xv"], s["g"], prm, blockdiag).astype(o_ref.dtype)

    @pl.when(ti == nt - 1)
    def _():
        for q in range(nb):
            zout_ref[q, 0] = z_s[q, 0:hd, 0:hd]
            zout_ref[q, 1] = z_s[q, hd:2 * hd, hd:2 * hd]


def _rwkv_param_specs(wp, imap):
    d_rwkv, lora, lora_p = wp["d_rwkv"], wp["lora"], wp["lora_p"]
    gpad = lora_p - DECAY_LORA - AAA_LORA
    glora = lora - DECAY_LORA - AAA_LORA
    row2 = lambda v: v.reshape(1, d_rwkv)
    g2p = jnp.pad(wp["g2"], ((0, gpad - glora), (0, 0))).astype(BF16)
    vec = pl.BlockSpec((1, LANE), imap)
    ops = [row2(wp["w0"]), row2(wp["a0"]), wp["w2"].astype(BF16), wp["a2"].astype(BF16), g2p,
           row2(wp["k_k"]), row2(wp["k_a"]), row2(wp["r_k"]), row2(wp["ln_x_w"]), row2(wp["ln_x_b"])]
    specs = [vec, vec, pl.BlockSpec((DECAY_LORA, LANE), imap), pl.BlockSpec((AAA_LORA, LANE), imap),
             pl.BlockSpec((gpad, LANE), imap), vec, vec, vec, vec, vec]
    return ops, specs


def _mu_split(wp):
    d_rwkv, lora, lora_p = wp["d_rwkv"], wp["lora"], wp["lora_p"]
    mu = wp["mu_shift"]
    return mu[None, :3 * d_rwkv], jnp.pad(mu[None, 3 * d_rwkv:], ((0, 0), (0, lora_p - lora)))


def _rwkv_chunked(pm3, shift_prev, wkv_prev, wp, *, tb, nb):
    n, t, _ = pm3.shape
    d_att, d_rwkv, lora_p = wp["d_att"], wp["d_rwkv"], wp["lora_p"]
    nh = d_rwkv // HEAD_DIM_B
    cb = 3 * d_att // LANE
    rb = d_rwkv // LANE
    sp_main = shift_prev[:, None, :3 * d_rwkv]
    lact = _lora_act(pm3, shift_prev[:, None, 3 * d_rwkv:], wp, shift=1)
    mu_main, _ = _mu_split(wp)
    z0 = jnp.swapaxes(wkv_prev, -1, -2)
    p_ops, p_specs = _rwkv_param_specs(wp, lambda i, h, j: (0, h))

    blk3 = lambda off: pl.BlockSpec((nb, tb, LANE), lambda i, h, j: (i, j, off + h))
    sp3 = lambda off: pl.BlockSpec((nb, 1, LANE), lambda i, h, j: (i, 0, off + h))
    vec = lambda off: pl.BlockSpec((1, LANE), lambda i, h, j: (0, off + h))
    o, zout = pl.pallas_call(
        functools.partial(_rwkv_chunk_kernel, tb=tb, nb=nb),
        grid=(n // nb, rb, t // tb),
        in_specs=[
            blk3(cb), blk3(cb + rb), blk3(cb + 2 * rb),
            pl.BlockSpec((nb, tb, lora_p), lambda i, h, j: (i, j, 0)),
            sp3(0), sp3(rb), sp3(2 * rb),
            vec(0), vec(rb), vec(2 * rb),
            *p_specs,
            pl.BlockSpec((nb, 2, HEAD_DIM_B, HEAD_DIM_B), lambda i, h, j: (i, h, 0, 0)),
        ],
        out_specs=[
            pl.BlockSpec((nb, tb, LANE), lambda i, h, j: (i, j, h)),
            pl.BlockSpec((nb, 2, HEAD_DIM_B, HEAD_DIM_B), lambda i, h, j: (i, h, 0, 0)),
        ],
        out_shape=[jax.ShapeDtypeStruct((n, t, d_rwkv), BF16),
                   jax.ShapeDtypeStruct((n, nh, HEAD_DIM_B, HEAD_DIM_B), F32)],
        scratch_shapes=[pltpu.VMEM((nb, LANE, LANE), F32), pltpu.VMEM((nb, 1, LANE), F32),
                        pltpu.VMEM((nb, 1, LANE), F32), pltpu.VMEM((nb, 1, LANE), F32)],
        compiler_params=_cparams(("parallel", "parallel", "arbitrary")),
        name="rwkv7_chunked",
    )(pm3, pm3, pm3, lact, sp_main, sp_main, sp_main, mu_main, mu_main, mu_main, *p_ops, z0)
    return o, jnp.swapaxes(zout, -1, -2)


def _rwkv_step_kernel(pr_ref, pk_ref, pv_ref, la_ref, spr_ref, spk_ref, spv_ref,
                      mur_ref, muk_ref, muv_ref, *rest, nseq, dseq):
    prm = rest[:10]
    z0_ref, o_ref, z_ref, tr_s, y_s = rest[10:]
    hd = HEAD_DIM_B
    m = nseq * dseq
    blockdiag = _pair_blockdiag()

    def shifted(x, first, mu):
        prev = jnp.concatenate([first, x[0:m - nseq]], axis=0)
        return x + mu * (prev - x)

    xr = shifted(pr_ref[...], spr_ref[...], mur_ref[...])
    xk = shifted(pk_ref[...], spk_ref[...], muk_ref[...])
    xv = shifted(pv_ref[...], spv_ref[...], muv_ref[...])
    lw, aneg, bb, kmod, g = _rwkv_pre(xk, la_ref[...], prm, blockdiag)
    decay = jnp.exp(lw)
    z_ref[...] = z0_ref[...]
    ys = []
    for t in range(dseq):
        rows = slice(t * nseq, (t + 1) * nseq)
        for q, val in enumerate((aneg, bb, decay, kmod, xr, xv)):
            tr_s[q] = val[rows].T
        for hh in range(2):
            base = hh * hd
            vcol = tr_s[5, base:base + hd, :]

            def sa_body(k, acc, hh=hh, base=base):
                return acc + z_ref[hh, k] * tr_s[0, pl.ds(base + k, 1), :]

            sa = lax.fori_loop(0, hd, sa_body, jnp.zeros((hd, nseq), F32))

            def upd_body(k, y, hh=hh, base=base, sa=sa, vcol=vcol):
                zk = (z_ref[hh, k] * tr_s[2, pl.ds(base + k, 1), :] + sa * tr_s[1, pl.ds(base + k, 1), :]
                      + vcol * tr_s[3, pl.ds(base + k, 1), :])
                z_ref[hh, k] = zk
                return y + zk * tr_s[4, pl.ds(base + k, 1), :]

            y_s[base:base + hd, :] = lax.fori_loop(0, hd, upd_body, jnp.zeros((hd, nseq), F32))
        ys.append(y_s[...].T)
    y = jnp.concatenate(ys, axis=0)
    o_ref[...] = _rwkv_post(y, xr, kmod, xv, g, prm, blockdiag).astype(o_ref.dtype)


def _rwkv_steps(pm2, shift_prev, wkv_prev, wp, *, nseq, dseq):
    m = pm2.shape[0]
    d_att, d_rwkv, lora_p = wp["d_att"], wp["d_rwkv"], wp["lora_p"]
    nh = d_rwkv // HEAD_DIM_B
    cb = 3 * d_att // LANE
    rb = d_rwkv // LANE
    sp_main = shift_prev[:, :3 * d_rwkv]
    lact = _lora_act(pm2[None], shift_prev[None, :, 3 * d_rwkv:], wp, shift=nseq)[0]
    mu_main, _ = _mu_split(wp)
    z0 = jnp.transpose(wkv_prev, (1, 3, 2, 0))
    p_ops, p_specs = _rwkv_param_specs(wp, lambda h: (0, h))
    blk = lambda rows, off: pl.BlockSpec((rows, LANE), lambda h: (0, off + h))
    zspec = pl.BlockSpec((2, HEAD_DIM_B, HEAD_DIM_B, nseq), lambda h: (h, 0, 0, 0))
    o, zout = pl.pallas_call(
        functools.partial(_rwkv_step_kernel, nseq=nseq, dseq=dseq),
        grid=(rb,),
        in_specs=[
            blk(m, cb), blk(m, cb + rb), blk(m, cb + 2 * rb), pl.BlockSpec((m, lora_p), lambda h: (0, 0)),
            blk(nseq, 0), blk(nseq, rb), blk(nseq, 2 * rb),
            blk(1, 0), blk(1, rb), blk(1, 2 * rb),
            *p_specs, zspec,
        ],
        out_specs=[blk(m, 0), zspec],
        out_shape=[jax.ShapeDtypeStruct((m, d_rwkv), BF16),
                   jax.ShapeDtypeStruct((nh, HEAD_DIM_B, HEAD_DIM_B, nseq), F32)],
        scratch_shapes=[pltpu.VMEM((6, LANE, nseq), F32), pltpu.VMEM((LANE, nseq), F32)],
        compiler_params=_cparams(("parallel",)),
        name="rwkv7_steps",
    )(pm2, pm2, pm2, lact, sp_main, sp_main, sp_main, mu_main, mu_main, mu_main, *p_ops, z0)
    return o, jnp.transpose(zout, (3, 0, 2, 1))


ATT_SUB = DILATION_PAIRS[0][0] // DILATION_PAIRS[0][1]
assert all(w // d == ATT_SUB for w, d in DILATION_PAIRS)
ATT_UNITS = 8


def _rel_bucket(dist):
    max_exact = NUM_BUCKETS // 2
    d_f = jnp.maximum(dist, 1).astype(F32)
    large = max_exact + (jnp.log(d_f / max_exact) / math.log(REL_MAX_DIST / max_exact)
                         * (NUM_BUCKETS - max_exact)).astype(jnp.int32)
    large = jnp.minimum(large, NUM_BUCKETS - 1)
    return jnp.where(dist < max_exact, dist, large)


def _bias_rows(rel_bias, dist):
    bucket = _rel_bucket(dist)[..., None]
    out = jnp.zeros(dist.shape + (rel_bias.shape[1],), F32)
    for b in range(NUM_BUCKETS):
        out = jnp.where(bucket == b, rel_bias[b].astype(F32), out)
    return out


def _prompt_bias(rel_bias):
    sub = ATT_SUB
    qi = jnp.arange(sub)[:, None]
    ki = jnp.arange(2 * sub)[None, :]
    dsub = qi + sub - ki
    ok = ((dsub >= 0) & (dsub <= sub))[..., None]
    tabs = []
    for _, dil in DILATION_PAIRS:
        b = _bias_rows(rel_bias, dil * jnp.clip(dsub, 0, sub))
        tabs.append(jnp.transpose(jnp.where(ok, b, NEG_INF), (2, 0, 1)))
    return jnp.stack(tabs)


def _head_rms(x, g):
    return x * lax.rsqrt(jnp.mean(x * x, axis=-1, keepdims=True) + RMS_EPS) * g


def _attn_prompt_kernel(q_ref, k_ref, v_ref, bias_ref, qg_ref, kg_ref, o_ref, kn_ref, vkeep_ref,
                        qs, ks, acc, m_s, l_s, *, t):
    sub = ATT_SUB
    keep = kn_ref.shape[0]
    qs[...] = _head_rms(q_ref[...], qg_ref[...]) * (1.0 / math.sqrt(HEAD_DIM_A))
    ks[...] = _head_rms(k_ref[...], kg_ref[...])
    kn_ref[...] = ks[pl.ds(t - keep, keep), :]
    vkeep_ref[...] = v_ref[pl.ds(t - keep, keep), :]
    m_s[...] = jnp.full(m_s.shape, NEG_INF, F32)
    l_s[...] = jnp.zeros(l_s.shape, F32)
    acc[...] = jnp.zeros(acc.shape, F32)
    for g, (_, dil) in enumerate(DILATION_PAIRS):
        span = dil * sub
        nblk = t // span
        bias_prev = bias_ref[g, :, 0:sub]
        bias_cur = bias_ref[g, :, sub:2 * sub]
        per_r = min(dil, ATT_UNITS)
        per_n = ATT_UNITS // per_r
        for r0 in range(0, dil, per_r):
            def body(it, _, dil=dil, span=span, r0=r0, per_r=per_r, per_n=per_n,
                     bias_prev=bias_prev, bias_cur=bias_cur):
                units = []
                for dr in range(per_r):
                    for dn in range(per_n):
                        n = it * per_n + dn
                        start = r0 + dr + span * n
                        startp = r0 + dr + span * jnp.maximum(n - 1, 0)
                        if dil == 1:
                            units.append((n, pl.ds(start, sub), pl.ds(startp, sub)))
                        else:
                            units.append((n, pl.ds(start, sub, stride=dil), pl.ds(startp, sub, stride=dil)))
                qb = [qs[rows, :].astype(BF16) for _, rows, _ in units]
                sc = [_dot_nt(q, ks[rows, :].astype(BF16)) + bias_cur for q, (_, rows, _) in zip(qb, units)]
                sp = [jnp.where(n > 0, _dot_nt(q, ks[rowsp, :].astype(BF16)) + bias_prev, NEG_INF)
                      for q, (n, _, rowsp) in zip(qb, units)]
                m_old = [m_s[rows, :] for _, rows, _ in units]
                m_new = [jnp.maximum(mo, jnp.maximum(jnp.max(a, axis=-1, keepdims=True),
                                                     jnp.max(b, axis=-1, keepdims=True)))
                         for mo, a, b in zip(m_old, sc, sp)]
                pc = [jnp.exp(a - mn) for a, mn in zip(sc, m_new)]
                pp = [jnp.exp(b - mn) for b, mn in zip(sp, m_new)]
                pv = [_dot(a.astype(BF16), v_ref[rows, :].astype(BF16))
                      + _dot(b.astype(BF16), v_ref[rowsp, :].astype(BF16))
                      for a, b, (_, rows, rowsp) in zip(pc, pp, units)]
                for i, (_, rows, _) in enumerate(units):
                    alpha = jnp.exp(m_old[i] - m_new[i])
                    l_s[rows, :] = (alpha * l_s[rows, :] + jnp.sum(pc[i], axis=-1, keepdims=True)
                                    + jnp.sum(pp[i], axis=-1, keepdims=True))
                    acc[rows, :] = alpha * acc[rows, :] + pv[i]
                    m_s[rows, :] = m_new[i]
                return 0

            lax.fori_loop(0, nblk // per_n, body, 0)
    o_ref[...] = (acc[...] / l_s[...]).astype(o_ref.dtype)


def _attn_prompt(pm3, rel_bias, qg, kg, d_att):
    n, t, _ = pm3.shape
    nh = d_att // HEAD_DIM_A
    assert t % MAX_WINDOW == 0
    keep = min(MAX_WINDOW, t)
    bias = _prompt_bias(rel_bias)
    blk = lambda off: pl.BlockSpec((None, t, HEAD_DIM_A), lambda i, h: (i, 0, off + h))
    kblk = pl.BlockSpec((None, keep, HEAD_DIM_A), lambda i, h: (i, 0, h))
    vec = pl.BlockSpec((1, HEAD_DIM_A), lambda i, h: (0, 0))
    return pl.pallas_call(
        functools.partial(_attn_prompt_kernel, t=t),
        grid=(n, nh),
        in_specs=[blk(0), blk(nh), blk(2 * nh),
                  pl.BlockSpec((len(DILATION_PAIRS), None, ATT_SUB, 2 * ATT_SUB), lambda i, h: (0, h, 0, 0)),
                  vec, vec],
        out_specs=[blk(0), kblk, kblk],
        out_shape=[jax.ShapeDtypeStruct((n, t, d_att), BF16)] + [jax.ShapeDtypeStruct((n, keep, d_att), F32)] * 2,
        scratch_shapes=[pltpu.VMEM((t, HEAD_DIM_A), F32)] * 5,
        compiler_params=_cparams(("parallel", "parallel")),
        name="attn_prompt",
    )(pm3, pm3, pm3, bias, qg.reshape(1, -1), kg.reshape(1, -1))


ROWS_PAD = 8


def _sample_bias(rel_bias, dseq):
    sub = ATT_SUB
    nh = rel_bias.shape[1]
    rb = lambda dist: _bias_rows(rel_bias, dist)
    c = jnp.arange(sub)
    tabs = [rb(dil * (sub - c)) for _, dil in reversed(DILATION_PAIRS[1:])]
    for s in range(dseq):
        tabs.append(jnp.where((c >= s)[:, None], rb(jnp.clip(sub + s - c, 0, sub)), NEG_INF))
    cache = jnp.broadcast_to(jnp.stack(tabs)[..., None], (len(tabs), sub, nh, LANE))
    s = jnp.arange(dseq)[:, None]
    sp = jnp.arange(dseq)[None, :]
    new = []
    for _, dil in DILATION_PAIRS:
        ok = (sp <= s) if dil == 1 else (sp == s)
        new.append(jnp.where(ok[..., None], rb(dil * jnp.clip(s - sp, 0, sub)), NEG_INF))
    new = jnp.broadcast_to(jnp.stack(new)[..., None], (len(new), dseq, dseq, nh, LANE))
    return cache, new


def _attn_sample_kernel(x_ref, k16_ref, k4_ref, v16_ref, v4_ref, bc_ref, bn_ref, qg_ref, kg_ref,
                        o_ref, kn_ref, *, dseq):
    nh, dh = x_ref.shape[-2], x_ref.shape[-1]
    sub = ATT_SUB
    d4, d16 = DILATION_PAIRS[1][1], DILATION_PAIRS[2][1]
    far = k16_ref.shape[0]
    qn = _head_rms(x_ref[0], qg_ref[...]) * (1.0 / math.sqrt(dh))
    kn = _head_rms(x_ref[1], kg_ref[...])
    vn = x_ref[2]
    kn_ref[...] = kn
    ones = jnp.ones((dh, LANE), F32)
    tail = sub // d4

    def lane_sum(x):
        keys = x.shape[0]
        return _dot(x.reshape(keys * nh, dh), ones).reshape(keys, nh, LANE)

    outs = []
    for s in range(dseq):
        q = qn[s]
        k1 = k4_ref[sub - tail:sub].reshape(sub, nh, dh)
        v1 = v4_ref[sub - tail:sub].reshape(sub, nh, dh)
        near = pl.ds(0, sub - far, stride=d16 // d4)
        cache = [(k16_ref[:, s], v16_ref[:, s], bc_ref[0, 0:far]),
                 (k4_ref[near, s], v4_ref[near, s], bc_ref[0, far:sub]),
                 (k4_ref[:, s], v4_ref[:, s], bc_ref[1]), (k1, v1, bc_ref[2 + s])]
        logits = [lane_sum(kk * q[None]) + bias for kk, _, bias in cache]
        new = []
        for s2 in range(s + 1):
            ln = jnp.broadcast_to(jnp.sum(q * kn[s2], axis=-1, keepdims=True), (nh, LANE))
            new += [ln + bn_ref[g, s, s2] for g in range(len(DILATION_PAIRS))]
        m = functools.reduce(jnp.maximum, [jnp.max(x, axis=0) for x in logits] + new)
        p = [jnp.exp(x - m[None]) for x in logits]
        pn = [jnp.exp(x - m) for x in new]
        denom = functools.reduce(jnp.add, [jnp.sum(x, axis=0) for x in p] + pn)
        o = functools.reduce(jnp.add, [jnp.sum(pi * vv, axis=0) for pi, (_, vv, _) in zip(p, cache)])
        for s2 in range(s + 1):
            w = functools.reduce(jnp.add, pn[s2 * len(DILATION_PAIRS):(s2 + 1) * len(DILATION_PAIRS)])
            o = o + w * vn[s2]
        outs.append(o / denom)
    outs += [jnp.zeros((nh, dh), F32)] * (ROWS_PAD - dseq)
    o_ref[...] = jnp.stack(outs)


def _attn_sample(x5, cache_k, cache_v, rel_bias, qg, kg, dseq):
    n, wb, nh, dh = cache_k.shape
    d4, d16 = DILATION_PAIRS[1][1], DILATION_PAIRS[2][1]
    assert wb == MAX_WINDOW == d16 * ATT_SUB and dseq <= d4 and DILATION_PAIRS[0][1] == 1 and dh == LANE
    bias_c, bias_n = _sample_bias(rel_bias, dseq)
    view = lambda c, d: c.reshape(n, wb // d, d, nh, dh)
    far = ATT_SUB - ATT_SUB * d4 // d16
    s16 = pl.BlockSpec((None, far, dseq, nh, dh), lambda i: (i, 0, 0, 0, 0))
    s4 = pl.BlockSpec((None, ATT_SUB, d4, nh, dh), lambda i: (i, wb // d4 // ATT_SUB - 1, 0, 0, 0))
    row_spec = pl.BlockSpec((None, ROWS_PAD, nh, dh), lambda i: (i, 0, 0, 0))
    vec = pl.BlockSpec((1, dh), lambda i: (0, 0))
    const = lambda a: pl.BlockSpec(a.shape, lambda i: (0,) * a.ndim, pipeline_mode=pl.Buffered(1))
    return pl.pallas_call(
        functools.partial(_attn_sample_kernel, dseq=dseq),
        grid=(n,),
        in_specs=[pl.BlockSpec((None, 3, ROWS_PAD, nh, dh), lambda i: (i, 0, 0, 0, 0)), s16, s4, s16, s4,
                  const(bias_c), const(bias_n), vec, vec],
        out_specs=[row_spec, row_spec],
        out_shape=[jax.ShapeDtypeStruct((n, ROWS_PAD, nh, dh), F32)] * 2,
        compiler_params=_cparams(("parallel",)),
        name="attn_sample",
    )(x5, view(cache_k, d16), view(cache_k, d4), view(cache_v, d16), view(cache_v, d4), bias_c, bias_n,
      qg.reshape(1, -1), kg.reshape(1, -1))


def _split_heads_kernel(x_ref, o_ref):
    nh, dh = o_ref.shape[1], o_ref.shape[2]
    for h in range(nh):
        o_ref[:, h, :] = x_ref[:, h * dh:(h + 1) * dh]


def _split_heads(x, nh):
    n, r, d = x.shape
    dh = d // nh
    tr = min(r, 256)
    return pl.pallas_call(
        _split_heads_kernel,
        grid=(n, r // tr),
        in_specs=[pl.BlockSpec((None, tr, d), lambda i, j: (i, j, 0))],
        out_specs=pl.BlockSpec((None, tr, nh, dh), lambda i, j: (i, j, 0, 0)),
        out_shape=jax.ShapeDtypeStruct((n, r, nh, dh), x.dtype),
        compiler_params=_cparams(("parallel", "parallel")),
        name="split_heads",
    )(x)


def _prep_weights(lp):
    (g_mix, w_in, q_norm_g, k_norm_g, mu_shift, w0, w2, a0, a2, g2, k_k, k_a, r_k, ln_x_w, ln_x_b, w_out,
     g_ffn, w_gate, w_up, conv_w, conv_b, w_down, g_ple, w_ple, w_ple_gate) = lp
    d_model = w_in.shape[0]
    d_rwkv = w0.shape[0]
    d_att = w_out.shape[0] - d_rwkv
    main = 3 * d_att + 3 * d_rwkv
    lora = w_in.shape[1] - main
    lora_p = _round_up(lora, LANE)
    d_ff = w_gate.shape[1]
    dffp = _round_up(d_ff, FF_ALIGN)
    assert main % lora_p == 0 and main % PROJ_TN == 0
    padc = lambda w, n: jnp.pad(w, ((0, 0), (0, n - w.shape[1])))
    wp = dict(
        d_model=d_model, d_att=d_att, d_rwkv=d_rwkv, lora=lora, lora_p=lora_p, lora_blk=main // lora_p,
        d_ff=d_ff, dffp=dffp,
        g_mix=g_mix, g_ffn=g_ffn, g_ple=g_ple, q_norm_g=q_norm_g, k_norm_g=k_norm_g,
        mu_shift=mu_shift, w0=w0, w2=w2, a0=a0, a2=a2, g2=g2, k_k=k_k, k_a=k_a, r_k=r_k,
        ln_x_w=ln_x_w, ln_x_b=ln_x_b,
        conv_w=padc(conv_w, dffp), conv_b=jnp.pad(conv_b, (0, dffp - d_ff)), w_ple=w_ple.astype(BF16),
    )
    big = dict(w_in_t=jnp.swapaxes(w_in, 0, 1), w_out=w_out, w_gate=w_gate, w_up=w_up, w_down=w_down,
               w_ple_gate=w_ple_gate)
    return wp, big


PROJ_TN = 512


def _layer(x2d, p2d, wp, big, mixer, *, tm, tf, down, shift, conv_prev, tiles_per_group):
    d_model, dffp = wp["d_model"], wp["dffp"]
    emit = big["w_in_t"].dtype != BF16
    w16 = dict(big)

    def mm(name, a, key, tm=tm, **kw):
        res = _matmul(a, big[key], tm=tm, name=name, **kw)
        if emit:
            res, w16[key] = res
        return res

    xn = _rmsnorm(x2d, wp["g_mix"], min(tm, 256))
    proj = mm("in_proj", xn, "w_in_t", tn=PROJ_TN, tk=d_model, b_t=True)
    mix, aux = mixer(proj)
    h1 = mm("out_proj", mix, "w_out", tn=PROJ_TN, tk=sum(x.shape[1] for x in mix), mode="residual", extras=(x2d,))
    hn = _rmsnorm(h1, wp["g_ffn"], min(tm, 256))
    cprev = jnp.pad(conv_prev, ((0, 0), (0, 0), (0, dffp - conv_prev.shape[-1])))
    res = _ffn_act(hn, big["w_gate"], big["w_up"], wp["conv_w"], wp["conv_b"], cprev,
                   tm=tm, tf=tf, shift=shift, tiles_per_group=tiles_per_group)
    act, conv_tail = res[0], res[1]
    if emit:
        w16["w_gate"], w16["w_up"] = res[2], res[3]
    h2 = mm("ffn_down", act, "w_down", mode="residual", extras=(h1,), **down)
    hn2 = _rmsnorm(h2, wp["g_ple"], min(tm, 256))
    y = mm("ple_gate", hn2, "w_ple_gate", tn=PROJ_TN, tk=d_model, mode="ple",
           extras=(h2, p2d, wp["w_ple"]))
    conv_tail = conv_tail[tiles_per_group - 1::tiles_per_group, :, :wp["d_ff"]]
    return y, conv_tail, aux, w16


def kernel(x_prompt, x_sample, cache_k, cache_v, state_shift, state_wkv, state_conv, p_prompt, p_sample, rel_bias, g_mix, w_in, q_norm_g, k_norm_g, mu_shift, w0, w2, a0, a2, g2, k_k, k_a, r_k, ln_x_w, ln_x_b, w_out, g_ffn, w_gate, w_up, conv_w, conv_b, w_down, g_ple, w_ple, w_ple_gate):
    depth = g_mix.shape[0]
    nbp, seq, d_model = x_prompt.shape
    nbs, dseq, _ = x_sample.shape
    hp = x_prompt.reshape(nbp * seq, d_model)
    hs = jnp.swapaxes(x_sample, 0, 1).reshape(dseq * nbs, d_model)
    outs_p, outs_s = [], []
    for i in range(depth):
        lp = (g_mix[i], w_in[i], q_norm_g[i], k_norm_g[i], mu_shift[i], w0[i], w2[i], a0[i], a2[i], g2[i],
              k_k[i], k_a[i], r_k[i], ln_x_w[i], ln_x_b[i], w_out[i], g_ffn[i], w_gate[i], w_up[i],
              conv_w[i], conv_b[i], w_down[i], g_ple[i], w_ple[i], w_ple_gate[i])
        wp, big = _prep_weights(lp)
        d_att, d_rwkv, lora, d_ff, dffp = wp["d_att"], wp["d_rwkv"], wp["lora"], wp["d_ff"], wp["dffp"]
        nha = d_att // HEAD_DIM_A
        nhb = d_rwkv // HEAD_DIM_B
        qg, kg = q_norm_g[i], k_norm_g[i]
        state_cols = slice(3 * d_att, 3 * d_att + 3 * d_rwkv + lora)

        def mixer_prompt(proj, shift_prev, wkv_prev):
            pm = proj.reshape(nbp, seq, -1)
            o_att, k_keep, v_keep = _attn_prompt(pm, rel_bias, qg, kg, d_att)
            o_rwkv, wkv_new = _rwkv_chunked(pm, shift_prev, wkv_prev, wp, tb=min(seq, 512),
                                            nb=2 if nbp % 2 == 0 else 1)
            mix = (o_att.reshape(nbp * seq, -1), o_rwkv.reshape(nbp * seq, -1))
            return mix, (_split_heads(k_keep, nha), _split_heads(v_keep, nha), pm[:, -1, state_cols], wkv_new)

        def mixer_sample(proj, ck, cv, shift_prev, wkv_prev):
            pm = proj.reshape(dseq, nbs, -1)
            x5 = jnp.transpose(pm[..., :3 * d_att].reshape(dseq, nbs, 3, nha, HEAD_DIM_A), (1, 2, 0, 3, 4))
            x5 = jnp.pad(x5, ((0, 0), (0, 0), (0, ROWS_PAD - dseq), (0, 0), (0, 0)))
            o_att, kn = _attn_sample(x5, ck, cv, rel_bias, qg, kg, dseq)
            o_att = jnp.swapaxes(o_att[:, :dseq], 0, 1).reshape(dseq * nbs, d_att).astype(BF16)
            o_rwkv, wkv_new = _rwkv_steps(proj, shift_prev, wkv_prev, wp, nseq=nbs, dseq=dseq)
            mix = (o_att, o_rwkv)
            v_new = jnp.swapaxes(pm[..., 2 * d_att:3 * d_att], 0, 1).reshape(nbs, dseq, nha, HEAD_DIM_A)
            return mix, (kn[:, :dseq], v_new, pm[-1, :, state_cols], wkv_new)

        mix_s = functools.partial(mixer_sample, ck=cache_k[i], cv=cache_v[i],
                                  shift_prev=state_shift[i], wkv_prev=state_wkv[i])
        conv_prev_s = jnp.swapaxes(state_conv[i], 0, 1).reshape(1, (CONV_WIDTH - 1) * nbs, d_ff)
        p_s = jnp.swapaxes(p_sample[i], 0, 1).reshape(dseq * nbs, -1)
        down_s = dict(tm=dseq * nbs, tn=min(1024, d_model), tk=2 * FF_ALIGN if dffp % (2 * FF_ALIGN) == 0 else FF_ALIGN)
        hs, conv_s, aux_s, w16 = _layer(hs, p_s, wp, big, mix_s, tm=dseq * nbs, tf=FF_ALIGN // 2,
                                        down=down_s, shift=nbs, conv_prev=conv_prev_s, tiles_per_group=1)
        conv_s = jnp.swapaxes(conv_s.reshape(CONV_WIDTH - 1, nbs, d_ff), 0, 1)
        outs_s.append((*aux_s, conv_s))

        tm_p = min(1024, seq)
        zero_shift = jnp.zeros((nbp, 3 * d_rwkv + lora), F32)
        zero_wkv = jnp.zeros((nbp, nhb, HEAD_DIM_B, HEAD_DIM_B), F32)
        zero_conv = jnp.zeros((nbp, CONV_WIDTH - 1, d_ff), F32)
        mix_p = functools.partial(mixer_prompt, shift_prev=zero_shift, wkv_prev=zero_wkv)
        down_p = dict(tm=tm_p // 2, tn=PROJ_TN, tk=dffp)
        hp, conv_p, aux_p, _ = _layer(hp, p_prompt[i].reshape(nbp * seq, -1), wp, w16, mix_p, tm=tm_p, tf=FF_ALIGN,
                                      down=down_p, shift=1, conv_prev=zero_conv,
                                      tiles_per_group=seq // tm_p)
        outs_p.append((*aux_p, conv_p))

    y_p = hp.reshape(nbp, seq, d_model)
    y_s = jnp.swapaxes(hs.reshape(dseq, nbs, d_model), 0, 1)
    st = lambda outs, idx: jnp.stack([o[idx] for o in outs])
    return (y_p, y_s, st(outs_p, 0), st(outs_p, 1), st(outs_p, 2), st(outs_p, 3), st(outs_p, 4),
            st(outs_s, 0), st(outs_s, 1), st(outs_s, 2), st(outs_s, 3), st(outs_s, 4))
```
